```python
import math
import jax, jax.numpy as jnp
from jax import lax
import numpy as np

D_MODEL = 1024
BATCH = 4
SEQ = 8192
DEPTH = 1

GDN_HEAD_K = 128
GDN_HEAD_V = 128
GDN_HEADS = D_MODEL // GDN_HEAD_V
GDN_CONV = 4
GDN_CHUNK = 64
GDN_QK_DIM = GDN_HEADS * GDN_HEAD_K
GDN_V_DIM = GDN_HEADS * GDN_HEAD_V
GDN_CONV_DIM = 2 * GDN_QK_DIM + GDN_V_DIM

ATT_HEAD_DIM = 128
ATT_HEADS = D_MODEL // ATT_HEAD_DIM
ATT_KV_HEADS = 2
ATT_Q_DIM = ATT_HEADS * ATT_HEAD_DIM
ATT_KV_DIM = ATT_KV_HEADS * ATT_HEAD_DIM
IDX_HEADS = 8
IDX_DIM = 64
TOPK_MAX = 256
Q_BLOCK = 128

ROPE_THETA = 500000.0
ROPE_FRACTION = 4
D_FF = 4 * D_MODEL
EPS = 1e-6

IN_SPLITS = (GDN_CONV_DIM, GDN_V_DIM, GDN_HEADS, GDN_HEADS,
             ATT_Q_DIM, ATT_KV_DIM, ATT_KV_DIM,
             IDX_HEADS * IDX_DIM, IDX_DIM, IDX_HEADS,
             D_MODEL, D_MODEL)
IN_PROJ_DIM = (GDN_CONV_DIM + GDN_V_DIM + 2 * GDN_HEADS + ATT_Q_DIM + 2 * ATT_KV_DIM
               + IDX_HEADS * IDX_DIM + IDX_DIM + IDX_HEADS + 2 * D_MODEL)

kernel_name = "hybrid_gdn_dsa_gated_merge"


def rms_norm(x, g):
    xf = x.astype(jnp.float32)
    y = xf * lax.rsqrt(jnp.mean(xf * xf, axis=-1, keepdims=True) + EPS)
    return (y * g.astype(jnp.float32)).astype(x.dtype)


def l2_normalize(x):
    xf = x.astype(jnp.float32)
    return xf * lax.rsqrt(jnp.sum(xf * xf, axis=-1, keepdims=True) + EPS)


def rope_tables(seq, rot_dim):
    inv_freq = ROPE_THETA ** (-jnp.arange(0, rot_dim, 2, dtype=jnp.float32) / rot_dim)
    ang = jnp.arange(seq, dtype=jnp.float32)[:, None] * inv_freq[None, :]
    return jnp.cos(ang), jnp.sin(ang)


def apply_partial_rope(t, cos, sin):
    half = cos.shape[-1]
    shape = (t.shape[1],) + (1,) * (t.ndim - 3) + (half,)
    c, s = cos.reshape(shape), sin.reshape(shape)
    tf = t.astype(jnp.float32)
    x1, x2 = tf[..., :half], tf[..., half:2 * half]
    out = jnp.concatenate([x1 * c - x2 * s, x2 * c + x1 * s, tf[..., 2 * half:]], axis=-1)
    return out.astype(t.dtype)


def causal_short_conv(u, w):
    k = w.shape[0]
    s = u.shape[1]
    up = jnp.pad(u, ((0, 0), (k - 1, 0), (0, 0)))
    y = up[:, 0:s] * w[0]
    for j in range(1, k):
        y = y + up[:, j:j + s] * w[j]
    return y


def gated_delta_rule_chunked(q, k, v, g, beta):
    B, S, H, dk = q.shape
    dv = v.shape[-1]
    C = GDN_CHUNK
    N = S // C
    f32 = jnp.float32

    def chunks(t):
        t = jnp.swapaxes(t.astype(f32), 1, 2)
        return t.reshape((B, H, N, C) + t.shape[3:])

    q, k, v, g, beta = chunks(q), chunks(k), chunks(v), chunks(g), chunks(beta)
    g_cum = jnp.cumsum(g, axis=-1)
    pos = jnp.arange(C)
    incl = pos[:, None] >= pos[None, :]
    strict = pos[:, None] > pos[None, :]
    diff = g_cum[..., :, None] - g_cum[..., None, :]
    decay = jnp.where(incl, jnp.exp(jnp.where(incl, diff, 0.0)), 0.0)
    k_beta = k * beta[..., None]
    a_mat = jnp.where(strict, jnp.einsum('bhnid,bhnjd->bhnij', k_beta, k) * decay, 0.0)
    lhs = a_mat + jnp.eye(C, dtype=f32)
    rhs = jnp.concatenate([v * beta[..., None], k_beta * jnp.exp(g_cum)[..., None]], axis=-1)
    sol = lax.linalg.triangular_solve(lhs, rhs, left_side=True, lower=True, unit_diagonal=True)
    u, w = sol[..., :dv], sol[..., dv:]
    qk = jnp.where(incl, jnp.einsum('bhnid,bhnjd->bhnij', q, k) * decay, 0.0)
    q_dec = q * jnp.exp(g_cum)[..., None]
    g_last = g_cum[..., -1]
    k_dec = k * jnp.exp(g_last[..., None] - g_cum)[..., None]

    def step(state, xs):
        u_n, w_n, qk_n, q_n, k_n, gl_n = xs
        v_new = u_n - jnp.einsum('bhcd,bhde->bhce', w_n, state)
        o_n = (jnp.einsum('bhcd,bhde->bhce', q_n, state)
               + jnp.einsum('bhij,bhje->bhie', qk_n, v_new))
        state = state * jnp.exp(gl_n)[..., None, None] + jnp.einsum('bhcd,bhce->bhde', k_n, v_new)
        return state, o_n

    xs = tuple(jnp.moveaxis(t, 2, 0) for t in (u, w, qk, q_dec, k_dec, g_last))
    state0 = jnp.zeros((B, H, dk, dv), f32)
    _, o = lax.scan(step, state0, xs)
    o = jnp.moveaxis(o, 0, 2).reshape(B, H, S, dv)
    return jnp.swapaxes(o, 1, 2)


def dsa_sparse_attention(q, k, v, q_idx, k_idx, w_idx):
    B, S, H, hd = q.shape
    kvh = k.shape[2]
    grp = H // kvh
    n_blk = S // Q_BLOCK
    k_sel = min(TOPK_MAX, S // 4)
    scale = hd ** -0.5
    key_pos = jnp.arange(S, dtype=jnp.int32)
    k_idx_f = k_idx.astype(jnp.float32)

    def to_blocks(t):
        return jnp.moveaxis(t.reshape((B, n_blk, Q_BLOCK) + t.shape[2:]), 1, 0)

    def block(args):
        blk, qb, qib, wb = args
        t = blk * Q_BLOCK + jnp.arange(Q_BLOCK, dtype=jnp.int32)
        logits = jnp.einsum('bqhd,bsd->bqhs', qib.astype(jnp.float32), k_idx_f)
        score = jnp.einsum('bqh,bqhs->bqs', wb.astype(jnp.float32), jax.nn.relu(logits))
        causal = key_pos[None, :] <= t[:, None]
        score = jnp.where(causal[None], score, -jnp.inf)
        _, sel = lax.top_k(score, k_sel)
        valid = sel <= t[None, :, None]
        k_g = jax.vmap(lambda kb, ib: kb[ib])(k, sel)
        v_g = jax.vmap(lambda vb, ib: vb[ib])(v, sel)
        qg = qb.reshape(B, Q_BLOCK, kvh, grp, hd)
        s = jnp.einsum('bqcgd,bqncd->bqcgn', qg, k_g).astype(jnp.float32) * scale
        s = jnp.where(valid[:, :, None, None, :], s, -jnp.inf)
        p = jax.nn.softmax(s, axis=-1).astype(v_g.dtype)
        o = jnp.einsum('bqcgn,bqncd->bqcgd', p, v_g)
        return o.reshape(B, Q_BLOCK, H * hd)

    o = lax.map(block, (jnp.arange(n_blk, dtype=jnp.int32), to_blocks(q),
                        to_blocks(q_idx), to_blocks(w_idx)))
    return jnp.moveaxis(o, 0, 1).reshape(B, S, H * hd)


def hybrid_layer(x, norm_mix_g, w_in, conv_w, a_log, dt_bias, gdn_norm_g,
                 q_norm_g, k_norm_g, w_out, norm_mlp_g, w_mlp_up, w_mlp_down):
    B, S, _ = x.shape
    h = rms_norm(x, norm_mix_g)
    proj = h @ w_in
    points = np.cumsum(np.array(IN_SPLITS))[:-1].tolist()
    (g_qkv, g_z, g_a, g_b, a_q, a_k, a_v,
     i_q, i_k, i_w, gate_a, gate_b) = jnp.split(proj, points, axis=-1)

    qkv = jax.nn.silu(causal_short_conv(g_qkv, conv_w))
    gq, gk, gv = jnp.split(qkv, [GDN_QK_DIM, 2 * GDN_QK_DIM], axis=-1)
    gq = l2_normalize(gq.reshape(B, S, GDN_HEADS, GDN_HEAD_K)) * (GDN_HEAD_K ** -0.5)
    gk = l2_normalize(gk.reshape(B, S, GDN_HEADS, GDN_HEAD_K))
    gv = gv.reshape(B, S, GDN_HEADS, GDN_HEAD_V)
    beta = jax.nn.sigmoid(g_b.astype(jnp.float32))
    g = -jnp.exp(a_log.astype(jnp.float32)) * jax.nn.softplus(
        g_a.astype(jnp.float32) + dt_bias.astype(jnp.float32))
    o_gdn = gated_delta_rule_chunked(gq, gk, gv, g, beta)
    z = jax.nn.silu(g_z.reshape(B, S, GDN_HEADS, GDN_HEAD_V).astype(jnp.float32))
    o_gdn = (rms_norm(o_gdn, gdn_norm_g) * z).reshape(B, S, GDN_V_DIM).astype(x.dtype)

    cos, sin = rope_tables(S, ATT_HEAD_DIM // ROPE_FRACTION)
    a_q = apply_partial_rope(rms_norm(a_q.reshape(B, S, ATT_HEADS, ATT_HEAD_DIM), q_norm_g), cos, sin)
    a_k = apply_partial_rope(rms_norm(a_k.reshape(B, S, ATT_KV_HEADS, ATT_HEAD_DIM), k_norm_g), cos, sin)
    a_v = a_v.reshape(B, S, ATT_KV_HEADS, ATT_HEAD_DIM)
    cos_i, sin_i = rope_tables(S, IDX_DIM // ROPE_FRACTION)
    i_q = apply_partial_rope(i_q.reshape(B, S, IDX_HEADS, IDX_DIM), cos_i, sin_i)
    i_k = apply_partial_rope(i_k, cos_i, sin_i)
    i_w = i_w * (IDX_HEADS ** -0.5 * IDX_DIM ** -0.5)
    o_att = dsa_sparse_attention(a_q, a_k, a_v, i_q, i_k, i_w).astype(x.dtype)

    merged = jax.nn.sigmoid(gate_a) * o_gdn + jax.nn.sigmoid(gate_b) * o_att
    x = x + merged @ w_out

    h2 = rms_norm(x, norm_mlp_g)
    x = x + jnp.square(jax.nn.relu(h2 @ w_mlp_up)) @ w_mlp_down
    return x


def setup_inputs(seed: int = 0) -> dict:
    key = jax.random.key(seed)
    ks = jax.random.split(key, 14)
    f32 = jnp.float32

    def gain(k, n):
        return 1.0 + 0.02 * jax.random.normal(k, (DEPTH, n), f32)

    x = jax.random.normal(ks[0], (BATCH, SEQ, D_MODEL), f32)
    norm_mix_g = gain(ks[1], D_MODEL)
    w_in = jax.random.normal(ks[2], (DEPTH, D_MODEL, IN_PROJ_DIM), f32) * D_MODEL ** -0.5
    conv_w = jax.random.normal(ks[3], (DEPTH, GDN_CONV, GDN_CONV_DIM), f32) * GDN_CONV ** -0.5
    a_log = jnp.log(jax.random.uniform(ks[4], (DEPTH, GDN_HEADS), f32, 1.0, 16.0))
    dt = jnp.exp(jax.random.uniform(ks[5], (DEPTH, GDN_HEADS), f32, math.log(1e-3), math.log(1e-1)))
    dt_bias = dt + jnp.log(-jnp.expm1(-dt))
    gdn_norm_g = gain(ks[6], GDN_HEAD_V)
    q_norm_g = gain(ks[7], ATT_HEAD_DIM)
    k_norm_g = gain(ks[8], ATT_HEAD_DIM)
    w_out = jax.random.normal(ks[9], (DEPTH, D_MODEL, D_MODEL), f32) * D_MODEL ** -0.5
    norm_mlp_g = gain(ks[10], D_MODEL)
    w_mlp_up = jax.random.normal(ks[11], (DEPTH, D_MODEL, D_FF), f32) * D_MODEL ** -0.5
    w_mlp_down = jax.random.normal(ks[12], (DEPTH, D_FF, D_MODEL), f32) * D_FF ** -0.5
    return {"x": x, "norm_mix_g": norm_mix_g, "w_in": w_in, "conv_w": conv_w,
            "a_log": a_log, "dt_bias": dt_bias, "gdn_norm_g": gdn_norm_g,
            "q_norm_g": q_norm_g, "k_norm_g": k_norm_g, "w_out": w_out,
            "norm_mlp_g": norm_mlp_g, "w_mlp_up": w_mlp_up, "w_mlp_down": w_mlp_down}


def reference(x, norm_mix_g, w_in, conv_w, a_log, dt_bias, gdn_norm_g, q_norm_g, k_norm_g,
              w_out, norm_mlp_g, w_mlp_up, w_mlp_down):
    for layer in range(DEPTH):
        x = hybrid_layer(x, norm_mix_g[layer], w_in[layer], conv_w[layer], a_log[layer],
                         dt_bias[layer], gdn_norm_g[layer], q_norm_g[layer], k_norm_g[layer],
                         w_out[layer], norm_mlp_g[layer], w_mlp_up[layer], w_mlp_down[layer])
    return x
```

```python
import functools
import math

import numpy as np
import jax
import jax.numpy as jnp
from jax import lax
from jax.experimental import pallas as pl
from jax.experimental.pallas import tpu as pltpu

F32 = jnp.float32
BF16 = jnp.bfloat16
I32 = jnp.int32

D_MODEL = 1024
GDN_HEADS = 8
GDN_HEAD = 128
GDN_CONV = 4
GDN_CHUNK = 64
GDN_CONV_DIM = 3 * GDN_HEADS * GDN_HEAD
ATT_HEADS = 8
ATT_HEAD = 128
ATT_KV_HEADS = 2
ATT_GROUP = ATT_HEADS // ATT_KV_HEADS
IDX_HEADS = 8
IDX_DIM = 64
TOPK_MAX = 256
ROPE_THETA = 500000.0
ROPE_FRACTION = 4
D_FF = 4 * D_MODEL
EPS = 1e-6

IN_SPLITS = (GDN_CONV_DIM, D_MODEL, GDN_HEADS, GDN_HEADS, D_MODEL, 256, 256,
             IDX_HEADS * IDX_DIM, IDX_DIM, IDX_HEADS, D_MODEL, D_MODEL)

LANES = 128
COL_QKV = 0
COL_Z = 24
COL_AQ = 32
COL_AKV = 40
COL_IQ = 44
COL_GA = 48
COL_GB = 56
COL_SMALL = 64
N_COL_BLOCKS = 65
NP = N_COL_BLOCKS * LANES
SM_GA = 0
SM_GB = 8
SM_IK = 16
SM_IW = 80

NEG = -1e30
INT_MIN = -2 ** 31

TM_IN = 512
TN_IN = 13 * LANES
GDN_BLK = 512
TM_PREP = 512
QB = 128
KT = 512
TM_MERGE = 512
TM_MLP = 1024
TF_MLP = 512
VMEM_LIMIT = 56 * 2 ** 20


def _cparams(sem):
    return pltpu.CompilerParams(dimension_semantics=sem, vmem_limit_bytes=VMEM_LIMIT)


def _inproj_kernel(x_ref, g_ref, w_ref, o_ref):
    x = x_ref[...]
    ms = jnp.mean(x * x, axis=-1, keepdims=True)
    h = x * lax.rsqrt(ms + EPS) * g_ref[...]
    o_ref[...] = jnp.dot(h.astype(BF16), w_ref[...], preferred_element_type=F32)


def _inproj(x2, g_row, w_p):
    t = x2.shape[0]
    return pl.pallas_call(
        _inproj_kernel,
        grid=(NP // TN_IN, t // TM_IN),
        in_specs=[pl.BlockSpec((TM_IN, D_MODEL), lambda n, m: (m, 0)),
                  pl.BlockSpec((1, D_MODEL), lambda n, m: (0, 0)),
                  pl.BlockSpec((D_MODEL, TN_IN), lambda n, m: (0, n))],
        out_specs=pl.BlockSpec((TM_IN, TN_IN), lambda n, m: (m, n)),
        out_shape=jax.ShapeDtypeStruct((t, NP), F32),
        compiler_params=_cparams(("arbitrary", "arbitrary")),
        name="inproj",
    )(x2, g_row, w_p)


def _gdn_kernel(qkv_ref, z_ref, sm_ref, cw_ref, alog_ref, dtb_ref, gn_ref, o_ref,
                halo_ref, state_ref, qn_ref, kn_ref, vv_ref, gc_ref, bt_ref):
    blk = qkv_ref.shape[0]
    c = GDN_CHUNK

    @pl.when(pl.program_id(1) == 0)
    def _():
        halo_ref[...] = jnp.zeros_like(halo_ref)
        state_ref[...] = jnp.zeros_like(state_ref)

    row8 = lax.broadcasted_iota(I32, (8, LANES), 0)
    for j in range(GDN_CONV_DIM // LANES):
        lo = j * LANES
        u = qkv_ref[:, lo:lo + LANES]
        hal = halo_ref[:, lo:lo + LANES]
        w = cw_ref[:, lo:lo + LANES]
        y = u * w[GDN_CONV - 1:GDN_CONV, :]
        for sft in range(1, GDN_CONV):
            ur = pltpu.roll(u, sft, axis=0)
            hr = pltpu.roll(hal, sft, axis=0)
            top = jnp.where(row8 < sft, hr, ur[0:8, :])
            ush = jnp.concatenate([top, ur[8:, :]], axis=0)
            y = y + ush * w[GDN_CONV - 1 - sft:GDN_CONV - sft, :]
        halo_ref[:, lo:lo + LANES] = u[blk - 8:, :]
        a = y * jax.nn.sigmoid(y)
        head = j % GDN_HEADS
        if j < 2 * GDN_HEADS:
            a = a * lax.rsqrt(jnp.sum(a * a, axis=-1, keepdims=True) + EPS)
        if j < GDN_HEADS:
            qn_ref[:, head * LANES:(head + 1) * LANES] = a * (GDN_HEAD ** -0.5)
        elif j < 2 * GDN_HEADS:
            kn_ref[:, head * LANES:(head + 1) * LANES] = a
        else:
            vv_ref[:, head * LANES:(head + 1) * LANES] = a

    sm = sm_ref[...]
    xg = sm + dtb_ref[...]
    softplus = jnp.maximum(xg, 0.0) + jnp.log1p(jnp.exp(-jnp.abs(xg)))
    gc = -jnp.exp(alog_ref[...]) * softplus
    row_in_chunk = lax.broadcasted_iota(I32, (blk, LANES), 0) & (c - 1)
    d = 1
    while d < c:
        gc = gc + jnp.where(row_in_chunk >= d, pltpu.roll(gc, d, axis=0), 0.0)
        d *= 2
    gc_ref[...] = gc
    bt_ref[...] = jax.nn.sigmoid(sm)

    ii = lax.broadcasted_iota(I32, (c, c), 0)
    jj = lax.broadcasted_iota(I32, (c, c), 1)
    incl = ii >= jj
    strict = ii > jj
    gn = gn_ref[...]

    def chunk_body(ci, carry):
        r0 = pl.multiple_of(ci * c, c)
        gcc = gc_ref[pl.ds(r0, c), :]
        btc = bt_ref[pl.ds(r0, c), :]
        gct = gcc.T
        for h in range(GDN_HEADS):
            hs = slice(h * LANES, (h + 1) * LANES)
            gcol = gcc[:, SM_GA + h:SM_GA + h + 1]
            grow = gct[SM_GA + h:SM_GA + h + 1, :]
            glast = gcc[c - 1:c, SM_GA + h:SM_GA + h + 1]
            bcol = btc[:, SM_GB + h:SM_GB + h + 1]
            q = qn_ref[pl.ds(r0, c), hs]
            k = kn_ref[pl.ds(r0, c), hs]
            v = vv_ref[pl.ds(r0, c), hs]
            decay = jnp.where(incl, jnp.exp(jnp.where(incl, gcol - grow, 0.0)), 0.0)
            kb = k * bcol
            lhs = jnp.concatenate([kb, q], axis=0).astype(BF16)
            ab = lax.dot_general(lhs, k.astype(BF16), (((1,), (1,)), ((), ())),
                                 preferred_element_type=F32)
            a_mat = jnp.where(strict, ab[0:c, :] * decay, 0.0)
            qk = jnp.where(incl, ab[c:, :] * decay, 0.0)
            eg = jnp.exp(gcol)
            rhs = jnp.concatenate([v * bcol, kb * eg], axis=1)
            p = -a_mat
            r = p
            for _ in range(int(math.log2(c)) - 1):
                pb = p.astype(BF16)
                p = jnp.dot(pb, pb, preferred_element_type=F32)
                r = r + p + jnp.dot(r.astype(BF16), p.astype(BF16), preferred_element_type=F32)
            sol = rhs + jnp.dot(r.astype(BF16), rhs.astype(BF16), preferred_element_type=F32)
            u_n = sol[:, :LANES]
            w_n = sol[:, LANES:]
            q_dec = q * eg
            k_dec = k * jnp.exp(glast - gcol)
            st = state_ref[h]
            stb = st.astype(BF16)
            ws = jnp.dot(jnp.concatenate([w_n, q_dec], axis=0).astype(BF16), stb,
                         preferred_element_type=F32)
            v_new = u_n - ws[0:c, :]
            vnb = v_new.astype(BF16)
            o_n = ws[c:, :] + jnp.dot(qk.astype(BF16), vnb, preferred_element_type=F32)
            state_ref[h] = st * jnp.exp(glast) + jnp.dot(k_dec.T.astype(BF16), vnb,
                                                         preferred_element_type=F32)
            on = o_n * lax.rsqrt(jnp.mean(o_n * o_n, axis=-1, keepdims=True) + EPS) * gn
            zz = z_ref[pl.ds(r0, c), hs]
            o_ref[pl.ds(r0, c), hs] = on * (zz * jax.nn.sigmoid(zz))
        return carry

    lax.fori_loop(0, blk // c, chunk_body, 0)


def _gdn(proj, conv_w, alog_row, dtb_row, gn_row, batch, seq):
    t = proj.shape[0]
    blk = min(GDN_BLK, seq)
    ns = seq // blk
    row = lambda b, s: b * ns + s
    return pl.pallas_call(
        _gdn_kernel,
        grid=(batch, ns),
        in_specs=[pl.BlockSpec((blk, GDN_CONV_DIM), lambda b, s: (row(b, s), COL_QKV * LANES // GDN_CONV_DIM)),
                  pl.BlockSpec((blk, D_MODEL), lambda b, s: (row(b, s), COL_Z * LANES // D_MODEL)),
                  pl.BlockSpec((blk, LANES), lambda b, s: (row(b, s), COL_SMALL)),
                  pl.BlockSpec((GDN_CONV, GDN_CONV_DIM), lambda b, s: (0, 0)),
                  pl.BlockSpec((1, LANES), lambda b, s: (0, 0)),
                  pl.BlockSpec((1, LANES), lambda b, s: (0, 0)),
                  pl.BlockSpec((1, LANES), lambda b, s: (0, 0))],
        out_specs=pl.BlockSpec((blk, D_MODEL), lambda b, s: (row(b, s), 0)),
        out_shape=jax.ShapeDtypeStruct((t, D_MODEL), F32),
        scratch_shapes=[pltpu.VMEM((8, GDN_CONV_DIM), F32),
                        pltpu.VMEM((GDN_HEADS, GDN_HEAD, GDN_HEAD), F32),
                        pltpu.VMEM((blk, D_MODEL), F32),
                        pltpu.VMEM((blk, D_MODEL), F32),
                        pltpu.VMEM((blk, D_MODEL), F32),
                        pltpu.VMEM((blk, LANES), F32),
                        pltpu.VMEM((blk, LANES), F32)],
        compiler_params=_cparams(("arbitrary", "arbitrary")),
        name="gdn",
    )(proj, proj, proj, conv_w, alog_row, dtb_row, gn_row)


def _rope(y, cos, sin_lo, sin_hi, half):
    return y * cos + pltpu.roll(y, LANES - half, axis=1) * sin_lo + pltpu.roll(y, half, axis=1) * sin_hi


def _prep_kernel(aq_ref, akv_ref, iq_ref, sm_ref, ca_ref, sa1_ref, sa2_ref, ci_ref, si1_ref, si2_ref,
                 ck_ref, sk1_ref, sk2_ref, gq_ref, gk_ref,
                 q_out, k_out, v_out, qi_out, kit_out, w_out):
    ca, sa1, sa2 = ca_ref[...], sa1_ref[...], sa2_ref[...]
    half_a = ATT_HEAD // ROPE_FRACTION // 2
    half_i = IDX_DIM // ROPE_FRACTION // 2
    gq = gq_ref[...]
    gk = gk_ref[...]
    for h in range(ATT_HEADS):
        hs = slice(h * LANES, (h + 1) * LANES)
        xh = aq_ref[:, hs]
        y = xh * lax.rsqrt(jnp.mean(xh * xh, axis=-1, keepdims=True) + EPS) * gq
        q_out[:, hs] = _rope(y, ca, sa1, sa2, half_a).astype(BF16)
    for cc in range(ATT_KV_HEADS):
        hs = slice(cc * LANES, (cc + 1) * LANES)
        xk = akv_ref[:, hs]
        y = xk * lax.rsqrt(jnp.mean(xk * xk, axis=-1, keepdims=True) + EPS) * gk
        k_out[:, hs] = _rope(y, ca, sa1, sa2, half_a).astype(BF16)
        v_out[:, hs] = akv_ref[:, ATT_KV_HEADS * LANES + cc * LANES:ATT_KV_HEADS * LANES + (cc + 1) * LANES].astype(BF16)
    ci, si1, si2 = ci_ref[...], si1_ref[...], si2_ref[...]
    for j in range(IDX_HEADS * IDX_DIM // LANES):
        hs = slice(j * LANES, (j + 1) * LANES)
        qi_out[:, hs] = _rope(iq_ref[:, hs], ci, si1, si2, half_i).astype(BF16)
    sm = sm_ref[...]
    smr = _rope(sm, ck_ref[...], sk1_ref[...], sk2_ref[...], half_i)
    kit_out[...] = smr.T[SM_IK:SM_IK + IDX_DIM, :].astype(BF16)
    w_out[...] = sm * (IDX_HEADS ** -0.5 * IDX_DIM ** -0.5)


def _prep(proj, tables, gq_row, gk_row, batch, seq):
    t = proj.shape[0]
    tm = min(TM_PREP, seq)
    nsb = seq // tm
    tab_spec = pl.BlockSpec((tm, LANES), lambda i: (i % nsb, 0))
    row_spec = pl.BlockSpec((1, LANES), lambda i: (0, 0))
    return pl.pallas_call(
        _prep_kernel,
        grid=(t // tm,),
        in_specs=[pl.BlockSpec((tm, D_MODEL), lambda i: (i, COL_AQ * LANES // D_MODEL)),
                  pl.BlockSpec((tm, 4 * LANES), lambda i: (i, COL_AKV // 4)),
                  pl.BlockSpec((tm, 4 * LANES), lambda i: (i, COL_IQ // 4)),
                  pl.BlockSpec((tm, LANES), lambda i: (i, COL_SMALL))]
                 + [tab_spec] * 9 + [row_spec, row_spec],
        out_specs=[pl.BlockSpec((tm, D_MODEL), lambda i: (i, 0)),
                   pl.BlockSpec((tm, ATT_KV_HEADS * LANES), lambda i: (i, 0)),
                   pl.BlockSpec((tm, ATT_KV_HEADS * LANES), lambda i: (i, 0)),
                   pl.BlockSpec((tm, IDX_HEADS * IDX_DIM), lambda i: (i, 0)),
                   pl.BlockSpec((None, None, IDX_DIM, tm), lambda i: (i // nsb, i % nsb, 0, 0)),
                   pl.BlockSpec((tm, LANES), lambda i: (i, 0))],
        out_shape=[jax.ShapeDtypeStruct((t, D_MODEL), BF16),
                   jax.ShapeDtypeStruct((t, ATT_KV_HEADS * LANES), BF16),
                   jax.ShapeDtypeStruct((t, ATT_KV_HEADS * LANES), BF16),
                   jax.ShapeDtypeStruct((t, IDX_HEADS * IDX_DIM), BF16),
                   jax.ShapeDtypeStruct((batch, nsb, IDX_DIM, tm), BF16),
                   jax.ShapeDtypeStruct((t, LANES), F32)],
        compiler_params=_cparams(("arbitrary",)),
        name="prep",
    )(proj, proj, proj, proj, *tables, gq_row, gk_row)


def _select_kernel(qi_ref, w_ref, kit_ref, ustrict_ref, bias_ref, keys_ref, qh_ref, *, k_sel):
    qblk = pl.program_id(1)
    nt_all = keys_ref.shape[0]
    n_groups = KT // LANES
    nt = (qblk * QB + QB + KT - 1) // KT

    for h in range(IDX_HEADS):
        qh_ref[h] = qi_ref[:, h * IDX_DIM:(h + 1) * IDX_DIM]
    wb = [jnp.broadcast_to(w_ref[:, SM_IW + h:SM_IW + h + 1], (QB, LANES)) for h in range(IDX_HEADS)]
    row_g = qblk * QB + lax.broadcasted_iota(I32, (QB, LANES), 0)
    lane = lax.broadcasted_iota(I32, (QB, LANES), 1)

    def score_tile(j, carry):
        kt = kit_ref[j]
        acc = [jnp.zeros((QB, LANES), F32) for _ in range(n_groups)]
        for h in range(IDX_HEADS):
            lg = jnp.dot(qh_ref[h], kt, preferred_element_type=F32)
            for g in range(n_groups):
                acc[g] = acc[g] + jnp.maximum(lg[:, g * LANES:(g + 1) * LANES], 0.0) * wb[h]
        for g in range(n_groups):
            col = j * KT + g * LANES + lane
            keys_ref[j, :, g * LANES:(g + 1) * LANES] = jnp.where(col <= row_g, acc[g], -jnp.inf)
        return carry

    lax.fori_loop(0, nt, score_tile, 0)

    k_eff = jnp.minimum(k_sel, row_g[:, 0:1] + 1).astype(F32)

    def count(cand, strict):
        cb = jnp.broadcast_to(cand, (QB, LANES))

        def tile(j, cnt):
            kk = keys_ref[j]
            for g in range(n_groups):
                kg = kk[:, g * LANES:(g + 1) * LANES]
                cnt = cnt + jnp.where(kg > cb if strict else kg >= cb, 1.0, 0.0)
            return cnt

        cnt = lax.fori_loop(0, nt, tile, jnp.zeros((QB, LANES), F32))
        return jnp.sum(cnt, axis=1, keepdims=True)

    def as_float(prefix):
        key = prefix ^ INT_MIN
        return pltpu.bitcast(key ^ ((key >> 31) & 0x7FFFFFFF), F32)

    def bisect(it, prefix):
        cand_u = prefix | lax.shift_left(jnp.int32(1), 31 - it)
        cnt = count(as_float(cand_u), False)
        return jnp.where(cnt >= k_eff, cand_u, prefix)

    prefix = lax.fori_loop(0, 32, bisect, jnp.zeros((QB, 1), I32))
    vstar = as_float(prefix)
    need = k_eff - count(vstar, True)
    vsb = jnp.broadcast_to(vstar, (QB, KT))
    ustrict = ustrict_ref[...]

    def final_tile(j, carry):
        kk = keys_ref[j]
        tie = kk == vsb
        tie_b = jnp.where(tie, 1.0, 0.0).astype(BF16)
        rank = jnp.dot(tie_b, ustrict, preferred_element_type=F32) + carry
        keep_tie = jnp.where(rank < need, 0.0, NEG)
        bias = jnp.where(kk > vsb, 0.0, jnp.where(tie, keep_tie, NEG))
        bias_ref[j] = bias.astype(BF16)
        return carry + jnp.sum(jnp.where(tie, 1.0, 0.0), axis=1, keepdims=True)

    lax.fori_loop(0, nt, final_tile, jnp.zeros((QB, 1), F32))

    def fill_tile(j, carry):
        bias_ref[j] = jnp.full((QB, KT), NEG, BF16)
        return carry

    lax.fori_loop(nt, nt_all, fill_tile, 0)


def _select(qi, wsm, kit, ustrict, batch, seq, k_sel):
    nq = seq // QB
    nt = seq // KT
    return pl.pallas_call(
        functools.partial(_select_kernel, k_sel=k_sel),
        grid=(batch, nq),
        in_specs=[pl.BlockSpec((QB, IDX_HEADS * IDX_DIM), lambda b, q: (b * nq + q, 0)),
                  pl.BlockSpec((QB, LANES), lambda b, q: (b * nq + q, 0)),
                  pl.BlockSpec((None, nt, IDX_DIM, KT), lambda b, q: (b, 0, 0, 0)),
                  pl.BlockSpec((KT, KT), lambda b, q: (0, 0))],
        out_specs=pl.BlockSpec((None, None, nt, QB, KT), lambda b, q: (b, q, 0, 0, 0)),
        out_shape=jax.ShapeDtypeStruct((batch, nq, nt, QB, KT), BF16),
        scratch_shapes=[pltpu.VMEM((nt, QB, KT), F32),
                        pltpu.VMEM((IDX_HEADS, QB, IDX_DIM), BF16)],
        compiler_params=_cparams(("arbitrary", "arbitrary")),
        name="select",
    )(qi, wsm, kit, ustrict)


def _attn_kernel(q_ref, k_ref, v_ref, bias_ref, o_ref, m_ref, l_ref, acc_ref):
    qblk = pl.program_id(1)
    nt = (qblk * QB + QB + KT - 1) // KT
    scale = ATT_HEAD ** -0.5
    m_ref[...] = jnp.full_like(m_ref, NEG)
    l_ref[...] = jnp.zeros_like(l_ref)
    acc_ref[...] = jnp.zeros_like(acc_ref)
    q4 = [jnp.concatenate([q_ref[:, (cc * ATT_GROUP + g) * LANES:(cc * ATT_GROUP + g + 1) * LANES]
                           for g in range(ATT_GROUP)], axis=0) for cc in range(ATT_KV_HEADS)]

    def kv_tile(j, carry):
        r0 = pl.multiple_of(j * KT, KT)
        bias = bias_ref[j].astype(F32)
        bias4 = jnp.concatenate([bias] * ATT_GROUP, axis=0)
        for cc in range(ATT_KV_HEADS):
            kc = k_ref[pl.ds(r0, KT), cc * LANES:(cc + 1) * LANES]
            vc = v_ref[pl.ds(r0, KT), cc * LANES:(cc + 1) * LANES]
            s = lax.dot_general(q4[cc], kc, (((1,), (1,)), ((), ())), preferred_element_type=F32)
            s = s * scale + bias4
            m_prev = m_ref[cc]
            m_new = jnp.maximum(m_prev, jnp.max(s, axis=1, keepdims=True))
            alpha = jnp.exp(m_prev - m_new)
            p = jnp.exp(s - m_new)
            l_ref[cc] = alpha * l_ref[cc] + jnp.sum(p, axis=1, keepdims=True)
            acc_ref[cc] = alpha * acc_ref[cc] + jnp.dot(p.astype(BF16), vc, preferred_element_type=F32)
            m_ref[cc] = m_new
        return carry

    lax.fori_loop(0, nt, kv_tile, 0)
    for cc in range(ATT_KV_HEADS):
        o = acc_ref[cc] / l_ref[cc]
        for g in range(ATT_GROUP):
            h = cc * ATT_GROUP + g
            o_ref[:, h * LANES:(h + 1) * LANES] = o[g * QB:(g + 1) * QB, :]


def _attention(q, k, v, bias, batch, seq):
    t = q.shape[0]
    nq = seq // QB
    nt = seq // KT
    rows = ATT_GROUP * QB
    return pl.pallas_call(
        _attn_kernel,
        grid=(batch, nq),
        in_specs=[pl.BlockSpec((QB, D_MODEL), lambda b, i: (b * nq + i, 0)),
                  pl.BlockSpec((seq, ATT_KV_HEADS * LANES), lambda b, i: (b, 0)),
                  pl.BlockSpec((seq, ATT_KV_HEADS * LANES), lambda b, i: (b, 0)),
                  pl.BlockSpec((None, None, nt, QB, KT), lambda b, i: (b, i, 0, 0, 0))],
        out_specs=pl.BlockSpec((QB, D_MODEL), lambda b, i: (b * nq + i, 0)),
        out_shape=jax.ShapeDtypeStruct((t, D_MODEL), F32),
        scratch_shapes=[pltpu.VMEM((ATT_KV_HEADS, rows, 1), F32),
                        pltpu.VMEM((ATT_KV_HEADS, rows, 1), F32),
                        pltpu.VMEM((ATT_KV_HEADS, rows, LANES), F32)],
        compiler_params=_cparams(("arbitrary", "arbitrary")),
        name="attention",
    )(q, k, v, bias)


def _merge_kernel(ga_ref, gb_ref, og_ref, oa_ref, x_ref, w_ref, o_ref):
    merged = jax.nn.sigmoid(ga_ref[...]) * og_ref[...] + jax.nn.sigmoid(gb_ref[...]) * oa_ref[...]
    o_ref[...] = x_ref[...] + jnp.dot(merged.astype(BF16), w_ref[...], preferred_element_type=F32)


def _merge(proj, o_gdn, o_att, x2, w_out):
    t = x2.shape[0]
    tm = TM_MERGE
    blk = lambda c: pl.BlockSpec((tm, D_MODEL), lambda i: (i, c))
    return pl.pallas_call(
        _merge_kernel,
        grid=(t // tm,),
        in_specs=[blk(COL_GA * LANES // D_MODEL), blk(COL_GB * LANES // D_MODEL), blk(0), blk(0), blk(0),
                  pl.BlockSpec((D_MODEL, D_MODEL), lambda i: (0, 0))],
        out_specs=blk(0),
        out_shape=jax.ShapeDtypeStruct((t, D_MODEL), F32),
        compiler_params=_cparams(("arbitrary",)),
        name="merge",
    )(proj, proj, o_gdn, o_att, x2, w_out)


def _mlp_kernel(x_ref, g_ref, wu_ref, wd_ref, o_ref, h_ref, acc_ref):
    f = pl.program_id(1)

    @pl.when(f == 0)
    def _():
        x = x_ref[...]
        ms = jnp.mean(x * x, axis=-1, keepdims=True)
        h_ref[...] = (x * lax.rsqrt(ms + EPS) * g_ref[...]).astype(BF16)
        acc_ref[...] = x

    up = jnp.dot(h_ref[...], wu_ref[...], preferred_element_type=F32)
    act = jnp.square(jnp.maximum(up, 0.0))
    acc_ref[...] += jnp.dot(act.astype(BF16), wd_ref[...], preferred_element_type=F32)

    @pl.when(f == pl.num_programs(1) - 1)
    def _():
        o_ref[...] = acc_ref[...]


def _mlp(x1, g_row, w_up, w_down):
    t = x1.shape[0]
    tm = min(TM_MLP, t)
    return pl.pallas_call(
        _mlp_kernel,
        grid=(t // tm, D_FF // TF_MLP),
        in_specs=[pl.BlockSpec((tm, D_MODEL), lambda i, f: (i, 0)),
                  pl.BlockSpec((1, D_MODEL), lambda i, f: (0, 0)),
                  pl.BlockSpec((D_MODEL, TF_MLP), lambda i, f: (0, f)),
                  pl.BlockSpec((TF_MLP, D_MODEL), lambda i, f: (f, 0))],
        out_specs=pl.BlockSpec((tm, D_MODEL), lambda i, f: (i, 0)),
        out_shape=jax.ShapeDtypeStruct((t, D_MODEL), F32),
        scratch_shapes=[pltpu.VMEM((tm, D_MODEL), BF16), pltpu.VMEM((tm, D_MODEL), F32)],
        compiler_params=_cparams(("arbitrary", "arbitrary")),
        name="mlp",
    )(x1, g_row, w_up, w_down)


def _pack_w_in(w_in):
    pts = np.cumsum(np.array(IN_SPLITS))[:-1].tolist()
    (g_qkv, g_z, g_a, g_b, a_q, a_k, a_v, i_q, i_k, i_w, gate_a, gate_b) = jnp.split(w_in, pts, axis=-1)
    small = jnp.concatenate([g_a, g_b, i_k, i_w], axis=-1)
    small = jnp.pad(small, ((0, 0), (0, LANES - small.shape[-1])))
    return jnp.concatenate([g_qkv, g_z, a_q, a_k, a_v, i_q, gate_a, gate_b, small], axis=-1).astype(BF16)


def _lane_row(vec, offset=0):
    return jnp.zeros((1, LANES), F32).at[0, offset:offset + vec.shape[0]].set(vec.astype(F32))


def _rope_tables(seq, rot_dim, period, offset):
    half = rot_dim // 2
    inv_freq = ROPE_THETA ** (-jnp.arange(0, rot_dim, 2, dtype=F32) / rot_dim)
    ang = jnp.arange(seq, dtype=F32)[:, None] * inv_freq[None, :]
    cos, sin = jnp.cos(ang), jnp.sin(ang)
    c = jnp.ones((seq, LANES), F32)
    s_lo = jnp.zeros((seq, LANES), F32)
    s_hi = jnp.zeros((seq, LANES), F32)
    for base in range(offset, LANES - 2 * half + 1, period):
        c = c.at[:, base:base + half].set(cos).at[:, base + half:base + 2 * half].set(cos)
        s_lo = s_lo.at[:, base:base + half].set(-sin)
        s_hi = s_hi.at[:, base + half:base + 2 * half].set(sin)
    return c, s_lo, s_hi


def _layer(x2, batch, seq, norm_mix_g, w_in, conv_w, a_log, dt_bias, gdn_norm_g, q_norm_g, k_norm_g,
           w_out, norm_mlp_g, w_mlp_up, w_mlp_down):
    k_sel = min(TOPK_MAX, seq // 4)
    proj = _inproj(x2, norm_mix_g.reshape(1, D_MODEL), _pack_w_in(w_in))

    o_gdn = _gdn(proj, conv_w, _lane_row(a_log, SM_GA), _lane_row(dt_bias, SM_GA),
                 gdn_norm_g.reshape(1, LANES), batch, seq)

    tab_a = _rope_tables(seq, ATT_HEAD // ROPE_FRACTION, LANES, 0)
    tab_i = _rope_tables(seq, IDX_DIM // ROPE_FRACTION, IDX_DIM, 0)
    tab_k = _rope_tables(seq, IDX_DIM // ROPE_FRACTION, LANES, SM_IK)
    q_att, k_att, v_att, q_idx, k_idx_t, w_idx = _prep(
        proj, tab_a + tab_i + tab_k, q_norm_g.reshape(1, LANES), k_norm_g.reshape(1, LANES), batch, seq)

    ii = np.arange(KT)
    ustrict = jnp.asarray((ii[:, None] < ii[None, :]).astype(np.float32), dtype=BF16)
    bias = _select(q_idx, w_idx, k_idx_t, ustrict, batch, seq, k_sel)
    o_att = _attention(q_att, k_att, v_att, bias, batch, seq)

    x1 = _merge(proj, o_gdn, o_att, x2, w_out.astype(BF16))
    return _mlp(x1, norm_mlp_g.reshape(1, D_MODEL), w_mlp_up.astype(BF16), w_mlp_down.astype(BF16))


def kernel(x, norm_mix_g, w_in, conv_w, a_log, dt_bias, gdn_norm_g, q_norm_g, k_norm_g, w_out,
           norm_mlp_g, w_mlp_up, w_mlp_down):
    batch, seq, _ = x.shape
    x2 = x.reshape(batch * seq, D_MODEL)
    for layer in range(norm_mix_g.shape[0]):
        x2 = _layer(x2, batch, seq, norm_mix_g[layer], w_in[layer], conv_w[layer], a_log[layer],
                    dt_bias[layer], gdn_norm_g[layer], q_norm_g[layer], k_norm_g[layer], w_out[layer],
                    norm_mlp_g[layer], w_mlp_up[layer], w_mlp_down[layer])
    return x2.reshape(batch, seq, D_MODEL)
```

```python
import functools
import math

import numpy as np
import jax
import jax.numpy as jnp
from jax import lax
from jax.experimental import pallas as pl
from jax.experimental.pallas import tpu as pltpu

F32 = jnp.float32
BF16 = jnp.bfloat16
I32 = jnp.int32

D_MODEL = 1024
GDN_HEADS = 8
GDN_HEAD = 128
GDN_CONV = 4
GDN_CHUNK = 64
GDN_CONV_DIM = 3 * GDN_HEADS * GDN_HEAD
ATT_HEADS = 8
ATT_HEAD = 128
ATT_KV_HEADS = 2
ATT_GROUP = ATT_HEADS // ATT_KV_HEADS
IDX_HEADS = 8
IDX_DIM = 64
TOPK_MAX = 256
ROPE_THETA = 500000.0
ROPE_FRACTION = 4
D_FF = 4 * D_MODEL
EPS = 1e-6

IN_SPLITS = (GDN_CONV_DIM, D_MODEL, GDN_HEADS, GDN_HEADS, D_MODEL, 256, 256,
             IDX_HEADS * IDX_DIM, IDX_DIM, IDX_HEADS, D_MODEL, D_MODEL)

LANES = 128
COL_QKV = 0
COL_Z = 24
COL_AQ = 32
COL_AKV = 40
COL_IQ = 44
COL_GA = 48
COL_GB = 56
COL_SMALL = 64
N_COL_BLOCKS = 65
NP = N_COL_BLOCKS * LANES
SM_GA = 0
SM_GB = 8
SM_IK = 16
SM_IW = 80
TAB_SIN_A = 2 * (ATT_HEAD // ROPE_FRACTION // 2)
TAB_COS_I = 4 * (ATT_HEAD // ROPE_FRACTION // 2)
TAB_SIN_I = TAB_COS_I + 2 * (IDX_DIM // ROPE_FRACTION // 2)

NEG = -1e30
INT_MIN = -2 ** 31
LOG2E = 1.4426950408889634
SAFE_LOGIT_BOUND = 60.0

TM_IN = 512
TN_IN = 13 * LANES
GDN_BLK = 512
TM_PREP = 512
QB = 128
KT = 512
TM_MERGE = 512
TM_MLP = 1024
TF_MLP = 512
VMEM_LIMIT = 56 * 2 ** 20


def _cparams(sem):
    return pltpu.CompilerParams(dimension_semantics=sem, vmem_limit_bytes=VMEM_LIMIT)


def _inproj_kernel(x_ref, g_ref, w_ref, o_ref):
    x = x_ref[...]
    ms = jnp.mean(x * x, axis=-1, keepdims=True)
    h = x * lax.rsqrt(ms + EPS) * g_ref[...]
    o_ref[...] = jnp.dot(h.astype(BF16), w_ref[...], preferred_element_type=F32)


def _inproj(x2, g_row, w_p):
    t = x2.shape[0]
    return pl.pallas_call(
        _inproj_kernel,
        grid=(NP // TN_IN, t // TM_IN),
        in_specs=[pl.BlockSpec((TM_IN, D_MODEL), lambda n, m: (m, 0)),
                  pl.BlockSpec((1, D_MODEL), lambda n, m: (0, 0)),
                  pl.BlockSpec((D_MODEL, TN_IN), lambda n, m: (0, n))],
        out_specs=pl.BlockSpec((TM_IN, TN_IN), lambda n, m: (m, n)),
        out_shape=jax.ShapeDtypeStruct((t, NP), F32),
        compiler_params=_cparams(("arbitrary", "arbitrary")),
        name="inproj",
    )(x2, g_row, w_p)


def _gdn_kernel(qkv_ref, z_ref, sm_ref, cw_ref, alog_ref, dtb_ref, gn_ref, o_ref,
                halo_ref, state_ref, qn_ref, kn_ref, vv_ref, gc_ref, bt_ref):
    blk = qkv_ref.shape[0]
    c = GDN_CHUNK

    @pl.when(pl.program_id(1) == 0)
    def _():
        halo_ref[...] = jnp.zeros_like(halo_ref)
        state_ref[...] = jnp.zeros_like(state_ref)

    row8 = lax.broadcasted_iota(I32, (8, LANES), 0)
    for j in range(GDN_CONV_DIM // LANES):
        lo = j * LANES
        u = qkv_ref[:, lo:lo + LANES]
        hal = halo_ref[:, lo:lo + LANES]
        w = cw_ref[:, lo:lo + LANES]
        y = u * w[GDN_CONV - 1:GDN_CONV, :]
        for sft in range(1, GDN_CONV):
            ur = pltpu.roll(u, sft, axis=0)
            hr = pltpu.roll(hal, sft, axis=0)
            top = jnp.where(row8 < sft, hr, ur[0:8, :])
            ush = jnp.concatenate([top, ur[8:, :]], axis=0)
            y = y + ush * w[GDN_CONV - 1 - sft:GDN_CONV - sft, :]
        halo_ref[:, lo:lo + LANES] = u[blk - 8:, :]
        a = y * jax.nn.sigmoid(y)
        head = j % GDN_HEADS
        if j < 2 * GDN_HEADS:
            a = a * lax.rsqrt(jnp.sum(a * a, axis=-1, keepdims=True) + EPS)
        if j < GDN_HEADS:
            qn_ref[:, head * LANES:(head + 1) * LANES] = a * (GDN_HEAD ** -0.5)
        elif j < 2 * GDN_HEADS:
            kn_ref[:, head * LANES:(head + 1) * LANES] = a
        else:
            vv_ref[:, head * LANES:(head + 1) * LANES] = a

    sm = sm_ref[...]
    xg = sm + dtb_ref[...]
    softplus = jnp.maximum(xg, 0.0) + jnp.log1p(jnp.exp(-jnp.abs(xg)))
    gc = -jnp.exp(alog_ref[...]) * softplus
    row_in_chunk = lax.broadcasted_iota(I32, (blk, LANES), 0) & (c - 1)
    d = 1
    while d < c:
        gc = gc + jnp.where(row_in_chunk >= d, pltpu.roll(gc, d, axis=0), 0.0)
        d *= 2
    gc_ref[...] = gc
    bt_ref[...] = jax.nn.sigmoid(sm)

    ii = lax.broadcasted_iota(I32, (c, c), 0)
    jj = lax.broadcasted_iota(I32, (c, c), 1)
    incl = ii >= jj
    strict = ii > jj
    gn = gn_ref[...]

    def chunk_body(ci, carry):
        r0 = pl.multiple_of(ci * c, c)
        gcc = gc_ref[pl.ds(r0, c), :]
        btc = bt_ref[pl.ds(r0, c), :]
        gct = gcc.T
        heads = range(GDN_HEADS)
        hs = [slice(h * LANES, (h + 1) * LANES) for h in heads]
        mm = lambda a, b: jnp.dot(a.astype(BF16), b.astype(BF16), preferred_element_type=F32)
        gcol = [gcc[:, SM_GA + h:SM_GA + h + 1] for h in heads]
        grow = [gct[SM_GA + h:SM_GA + h + 1, :] for h in heads]
        glast = [gcc[c - 1:c, SM_GA + h:SM_GA + h + 1] for h in heads]
        bcol = [btc[:, SM_GB + h:SM_GB + h + 1] for h in heads]
        q = [qn_ref[pl.ds(r0, c), hs[h]] for h in heads]
        k = [kn_ref[pl.ds(r0, c), hs[h]] for h in heads]
        v = [vv_ref[pl.ds(r0, c), hs[h]] for h in heads]
        kb = [k[h] * bcol[h] for h in heads]
        ab = [lax.dot_general(jnp.concatenate([kb[h], q[h]], axis=0).astype(BF16), k[h].astype(BF16),
                              (((1,), (1,)), ((), ())), preferred_element_type=F32) for h in heads]
        decay = [jnp.where(incl, jnp.exp(jnp.where(incl, gcol[h] - grow[h], 0.0)), 0.0) for h in heads]
        qk = [jnp.where(incl, ab[h][c:, :] * decay[h], 0.0) for h in heads]
        eg = [jnp.exp(gcol[h]) for h in heads]
        rhs = [jnp.concatenate([v[h] * bcol[h], kb[h] * eg[h]], axis=1) for h in heads]
        p = [-jnp.where(strict, ab[h][0:c, :] * decay[h], 0.0) for h in heads]
        r = list(p)
        for _ in range(int(math.log2(c)) - 1):
            p = [mm(p[h], p[h]) for h in heads]
            rp = [mm(r[h], p[h]) for h in heads]
            r = [r[h] + p[h] + rp[h] for h in heads]
        sol = [rhs[h] + mm(r[h], rhs[h]) for h in heads]
        q_dec = [q[h] * eg[h] for h in heads]
        k_dec_t = [(k[h] * jnp.exp(glast[h] - gcol[h])).T for h in heads]
        st = [state_ref[h] for h in heads]
        ws = [mm(jnp.concatenate([sol[h][:, LANES:], q_dec[h]], axis=0), st[h]) for h in heads]
        v_new = [sol[h][:, :LANES] - ws[h][0:c, :] for h in heads]
        o_n = [ws[h][c:, :] + mm(qk[h], v_new[h]) for h in heads]
        st_new = [st[h] * jnp.exp(glast[h]) + mm(k_dec_t[h], v_new[h]) for h in heads]
        for h in heads:
            state_ref[h] = st_new[h]
            on = o_n[h] * lax.rsqrt(jnp.mean(o_n[h] * o_n[h], axis=-1, keepdims=True) + EPS) * gn
            zz = z_ref[pl.ds(r0, c), hs[h]]
            o_ref[pl.ds(r0, c), hs[h]] = on * (zz * jax.nn.sigmoid(zz))
        return carry

    lax.fori_loop(0, blk // c, chunk_body, 0)


def _gdn(proj, conv_w, alog_row, dtb_row, gn_row, batch, seq):
    t = proj.shape[0]
    blk = min(GDN_BLK, seq)
    ns = seq // blk
    row = lambda b, s: b * ns + s
    return pl.pallas_call(
        _gdn_kernel,
        grid=(batch, ns),
        in_specs=[pl.BlockSpec((blk, GDN_CONV_DIM), lambda b, s: (row(b, s), COL_QKV * LANES // GDN_CONV_DIM)),
                  pl.BlockSpec((blk, D_MODEL), lambda b, s: (row(b, s), COL_Z * LANES // D_MODEL)),
                  pl.BlockSpec((blk, LANES), lambda b, s: (row(b, s), COL_SMALL)),
                  pl.BlockSpec((GDN_CONV, GDN_CONV_DIM), lambda b, s: (0, 0)),
                  pl.BlockSpec((1, LANES), lambda b, s: (0, 0)),
                  pl.BlockSpec((1, LANES), lambda b, s: (0, 0)),
                  pl.BlockSpec((1, LANES), lambda b, s: (0, 0))],
        out_specs=pl.BlockSpec((blk, D_MODEL), lambda b, s: (row(b, s), 0)),
        out_shape=jax.ShapeDtypeStruct((t, D_MODEL), F32),
        scratch_shapes=[pltpu.VMEM((8, GDN_CONV_DIM), F32),
                        pltpu.VMEM((GDN_HEADS, GDN_HEAD, GDN_HEAD), F32),
                        pltpu.VMEM((blk, D_MODEL), F32),
                        pltpu.VMEM((blk, D_MODEL), F32),
                        pltpu.VMEM((blk, D_MODEL), F32),
                        pltpu.VMEM((blk, LANES), F32),
                        pltpu.VMEM((blk, LANES), F32)],
        compiler_params=_cparams(("arbitrary", "arbitrary")),
        name="gdn",
    )(proj, proj, proj, conv_w, alog_row, dtb_row, gn_row)


def _rope(y, cos, sin_lo, sin_hi, half):
    return y * cos + pltpu.roll(y, LANES - half, axis=1) * sin_lo + pltpu.roll(y, half, axis=1) * sin_hi


def _prep_kernel(aq_ref, akv_ref, iq_ref, sm_ref, tab_ref, gq_ref, gk_ref,
                 q_out, k_out, v_out, qi_out, kit_out, w_out):
    half_a = ATT_HEAD // ROPE_FRACTION // 2
    half_i = IDX_DIM // ROPE_FRACTION // 2
    t = tab_ref[...]
    lane = lax.broadcasted_iota(I32, t.shape, 1)
    from_lane = lambda src: pltpu.roll(t, (-src) % LANES, axis=1)
    between = lambda lo, hi: (lane >= lo) & (lane < hi)
    ca = jnp.where(lane < 2 * half_a, t, 1.0)
    sa = from_lane(TAB_SIN_A)
    sa1 = jnp.where(lane < half_a, sa, 0.0)
    sa2 = jnp.where(between(half_a, 2 * half_a), sa, 0.0)
    lm = lane & (IDX_DIM - 1)
    low = lane < IDX_DIM
    ci = jnp.where(lm < 2 * half_i, jnp.where(low, from_lane(TAB_COS_I), from_lane(TAB_COS_I - IDX_DIM)), 1.0)
    si = jnp.where(low, from_lane(TAB_SIN_I), from_lane(TAB_SIN_I - IDX_DIM))
    si1 = jnp.where(lm < half_i, si, 0.0)
    si2 = jnp.where((lm >= half_i) & (lm < 2 * half_i), si, 0.0)
    ck = jnp.where(between(SM_IK, SM_IK + 2 * half_i), from_lane(TAB_COS_I - SM_IK), 1.0)
    sk = from_lane(TAB_SIN_I - SM_IK)
    sk1 = jnp.where(between(SM_IK, SM_IK + half_i), sk, 0.0)
    sk2 = jnp.where(between(SM_IK + half_i, SM_IK + 2 * half_i), sk, 0.0)
    gq = gq_ref[...]
    gk = gk_ref[...]
    for h in range(ATT_HEADS):
        hs = slice(h * LANES, (h + 1) * LANES)
        xh = aq_ref[:, hs]
        y = xh * lax.rsqrt(jnp.mean(xh * xh, axis=-1, keepdims=True) + EPS) * gq
        q_out[:, hs] = _rope(y, ca, sa1, sa2, half_a).astype(BF16)
    for cc in range(ATT_KV_HEADS):
        hs = slice(cc * LANES, (cc + 1) * LANES)
        xk = akv_ref[:, hs]
        y = xk * lax.rsqrt(jnp.mean(xk * xk, axis=-1, keepdims=True) + EPS) * gk
        k_out[:, hs] = _rope(y, ca, sa1, sa2, half_a).astype(BF16)
        v_out[:, hs] = akv_ref[:, ATT_KV_HEADS * LANES + cc * LANES:ATT_KV_HEADS * LANES + (cc + 1) * LANES].astype(BF16)
    for j in range(IDX_HEADS * IDX_DIM // LANES):
        hs = slice(j * LANES, (j + 1) * LANES)
        qi_out[:, hs] = _rope(iq_ref[:, hs], ci, si1, si2, half_i).astype(BF16)
    sm = sm_ref[...]
    smr = _rope(sm, ck, sk1, sk2, half_i)
    kit_out[...] = smr.T[SM_IK:SM_IK + IDX_DIM, :].astype(BF16)
    w_out[...] = sm * (IDX_HEADS ** -0.5 * IDX_DIM ** -0.5)


def _prep(proj, table, gq_row, gk_row, batch, seq):
    t = proj.shape[0]
    tm = min(TM_PREP, seq)
    nsb = seq // tm
    tab_spec = pl.BlockSpec((tm, LANES), lambda i: (i % nsb, 0))
    row_spec = pl.BlockSpec((1, LANES), lambda i: (0, 0))
    return pl.pallas_call(
        _prep_kernel,
        grid=(t // tm,),
        in_specs=[pl.BlockSpec((tm, D_MODEL), lambda i: (i, COL_AQ * LANES // D_MODEL)),
                  pl.BlockSpec((tm, 4 * LANES), lambda i: (i, COL_AKV // 4)),
                  pl.BlockSpec((tm, 4 * LANES), lambda i: (i, COL_IQ // 4)),
                  pl.BlockSpec((tm, LANES), lambda i: (i, COL_SMALL))]
                 + [tab_spec, row_spec, row_spec],
        out_specs=[pl.BlockSpec((tm, D_MODEL), lambda i: (i, 0)),
                   pl.BlockSpec((tm, ATT_KV_HEADS * LANES), lambda i: (i, 0)),
                   pl.BlockSpec((tm, ATT_KV_HEADS * LANES), lambda i: (i, 0)),
                   pl.BlockSpec((tm, IDX_HEADS * IDX_DIM), lambda i: (i, 0)),
                   pl.BlockSpec((None, None, IDX_DIM, tm), lambda i: (i // nsb, i % nsb, 0, 0)),
                   pl.BlockSpec((tm, LANES), lambda i: (i, 0))],
        out_shape=[jax.ShapeDtypeStruct((t, D_MODEL), BF16),
                   jax.ShapeDtypeStruct((t, ATT_KV_HEADS * LANES), BF16),
                   jax.ShapeDtypeStruct((t, ATT_KV_HEADS * LANES), BF16),
                   jax.ShapeDtypeStruct((t, IDX_HEADS * IDX_DIM), BF16),
                   jax.ShapeDtypeStruct((batch, nsb, IDX_DIM, tm), BF16),
                   jax.ShapeDtypeStruct((t, LANES), F32)],
        compiler_params=_cparams(("arbitrary",)),
        name="prep",
    )(proj, proj, proj, proj, table, gq_row, gk_row)


def _select_kernel(qi_ref, w_ref, kit_ref, ustrict_ref, bias_ref, keys_ref, qh_ref, *, k_sel):
    qblk = pl.program_id(1)
    nt_all = keys_ref.shape[0]
    n_groups = KT // LANES
    nt = (qblk * QB + QB + KT - 1) // KT

    for h in range(IDX_HEADS):
        qh_ref[h] = qi_ref[:, h * IDX_DIM:(h + 1) * IDX_DIM]
    wb = [jnp.broadcast_to(w_ref[:, SM_IW + h:SM_IW + h + 1], (QB, LANES)) for h in range(IDX_HEADS)]
    row_g = qblk * QB + lax.broadcasted_iota(I32, (QB, LANES), 0)
    lane = lax.broadcasted_iota(I32, (QB, LANES), 1)

    def score_tile(j, carry):
        kt = kit_ref[j]
        acc = [jnp.zeros((QB, LANES), F32) for _ in range(n_groups)]
        for h in range(IDX_HEADS):
            lg = jnp.dot(qh_ref[h], kt, preferred_element_type=F32)
            for g in range(n_groups):
                acc[g] = acc[g] + jnp.maximum(lg[:, g * LANES:(g + 1) * LANES], 0.0) * wb[h]
        for g in range(n_groups):
            col = j * KT + g * LANES + lane
            keys_ref[j, :, g * LANES:(g + 1) * LANES] = jnp.where(col <= row_g, acc[g], -jnp.inf)
        return carry

    lax.fori_loop(0, nt, score_tile, 0)

    k_eff = jnp.minimum(k_sel, row_g[:, 0:1] + 1).astype(F32)

    def count(cand, strict):
        cb = jnp.broadcast_to(cand, (QB, LANES))

        def tile(j, cnt):
            kk = keys_ref[j]
            for g in range(n_groups):
                kg = kk[:, g * LANES:(g + 1) * LANES]
                cnt = cnt + jnp.where(kg > cb if strict else kg >= cb, 1.0, 0.0)
            return cnt

        cnt = lax.fori_loop(0, nt, tile, jnp.zeros((QB, LANES), F32))
        return jnp.sum(cnt, axis=1, keepdims=True)

    def as_float(prefix):
        key = prefix ^ INT_MIN
        return pltpu.bitcast(key ^ ((key >> 31) & 0x7FFFFFFF), F32)

    def bisect(it, prefix):
        cand_u = prefix | lax.shift_left(jnp.int32(1), 31 - it)
        cnt = count(as_float(cand_u), False)
        return jnp.where(cnt >= k_eff, cand_u, prefix)

    prefix = lax.fori_loop(0, 32, bisect, jnp.zeros((QB, 1), I32))
    vstar = as_float(prefix)
    need = k_eff - count(vstar, True)
    vsb = jnp.broadcast_to(vstar, (QB, KT))
    ustrict = ustrict_ref[...]

    def final_tile(j, carry):
        kk = keys_ref[j]
        tie = kk == vsb
        tie_b = jnp.where(tie, 1.0, 0.0).astype(BF16)
        rank = jnp.dot(tie_b, ustrict, preferred_element_type=F32) + carry
        keep_tie = jnp.where(rank < need, 0.0, NEG)
        bias = jnp.where(kk > vsb, 0.0, jnp.where(tie, keep_tie, NEG))
        bias_ref[j] = bias.astype(BF16)
        return carry + jnp.sum(jnp.where(tie, 1.0, 0.0), axis=1, keepdims=True)

    lax.fori_loop(0, nt, final_tile, jnp.zeros((QB, 1), F32))

    def fill_tile(j, carry):
        bias_ref[j] = jnp.full((QB, KT), NEG, BF16)
        return carry

    lax.fori_loop(nt, nt_all, fill_tile, 0)


def _select(qi, wsm, kit, ustrict, batch, seq, k_sel):
    nq = seq // QB
    nt = seq // KT
    return pl.pallas_call(
        functools.partial(_select_kernel, k_sel=k_sel),
        grid=(batch, nq),
        in_specs=[pl.BlockSpec((QB, IDX_HEADS * IDX_DIM), lambda b, q: (b * nq + q, 0)),
                  pl.BlockSpec((QB, LANES), lambda b, q: (b * nq + q, 0)),
                  pl.BlockSpec((None, nt, IDX_DIM, KT), lambda b, q: (b, 0, 0, 0)),
                  pl.BlockSpec((KT, KT), lambda b, q: (0, 0))],
        out_specs=pl.BlockSpec((None, None, nt, QB, KT), lambda b, q: (b, q, 0, 0, 0)),
        out_shape=jax.ShapeDtypeStruct((batch, nq, nt, QB, KT), BF16),
        scratch_shapes=[pltpu.VMEM((nt, QB, KT), F32),
                        pltpu.VMEM((IDX_HEADS, QB, IDX_DIM), BF16)],
        compiler_params=_cparams(("arbitrary", "arbitrary")),
        name="select",
    )(qi, wsm, kit, ustrict)


def _attn_kernel(q_ref, k_ref, v_ref, bias_ref, bound_ref, o_ref, m_ref, l_ref, acc_ref, accw_ref):
    qblk = pl.program_id(1)
    nt = (qblk * QB + QB + KT - 1) // KT
    scale = ATT_HEAD ** -0.5
    q4 = [jnp.concatenate([q_ref[:, (cc * ATT_GROUP + g) * LANES:(cc * ATT_GROUP + g + 1) * LANES]
                           for g in range(ATT_GROUP)], axis=0) for cc in range(ATT_KV_HEADS)]

    def write_out(cc, o):
        for g in range(ATT_GROUP):
            h = cc * ATT_GROUP + g
            o_ref[:, h * LANES:(h + 1) * LANES] = o[g * QB:(g + 1) * QB, :]

    safe = bound_ref[0] <= SAFE_LOGIT_BOUND

    @pl.when(safe)
    def _():
        accw_ref[...] = jnp.zeros_like(accw_ref)
        ones = jnp.ones((KT, LANES), BF16)

        def kv_tile(j, carry):
            r0 = pl.multiple_of(j * KT, KT)
            bias4 = jnp.concatenate([bias_ref[j].astype(F32)] * ATT_GROUP, axis=0)
            for cc in range(ATT_KV_HEADS):
                kc = k_ref[pl.ds(r0, KT), cc * LANES:(cc + 1) * LANES]
                vc = v_ref[pl.ds(r0, KT), cc * LANES:(cc + 1) * LANES]
                s = lax.dot_general(q4[cc], kc, (((1,), (1,)), ((), ())), preferred_element_type=F32)
                p = jnp.exp2(s * (scale * LOG2E) + bias4)
                accw_ref[cc] += jnp.dot(p.astype(BF16), jnp.concatenate([vc, ones], axis=1),
                                        preferred_element_type=F32)
            return carry

        lax.fori_loop(0, nt, kv_tile, 0)
        for cc in range(ATT_KV_HEADS):
            aw = accw_ref[cc]
            write_out(cc, aw[:, :LANES] / aw[:, LANES:])

    @pl.when(jnp.logical_not(safe))
    def _():
        m_ref[...] = jnp.full_like(m_ref, NEG)
        l_ref[...] = jnp.zeros_like(l_ref)
        acc_ref[...] = jnp.zeros_like(acc_ref)

        def kv_tile(j, carry):
            r0 = pl.multiple_of(j * KT, KT)
            bias4 = jnp.concatenate([bias_ref[j].astype(F32)] * ATT_GROUP, axis=0)
            for cc in range(ATT_KV_HEADS):
                kc = k_ref[pl.ds(r0, KT), cc * LANES:(cc + 1) * LANES]
                vc = v_ref[pl.ds(r0, KT), cc * LANES:(cc + 1) * LANES]
                s = lax.dot_general(q4[cc], kc, (((1,), (1,)), ((), ())), preferred_element_type=F32)
                s = s * scale + bias4
                m_prev = m_ref[cc]
                m_new = jnp.maximum(m_prev, jnp.max(s, axis=1, keepdims=True))
                alpha = jnp.exp(m_prev - m_new)
                p = jnp.exp(s - m_new)
                l_ref[cc] = alpha * l_ref[cc] + jnp.sum(p, axis=1, keepdims=True)
                acc_ref[cc] = alpha * acc_ref[cc] + jnp.dot(p.astype(BF16), vc, preferred_element_type=F32)
                m_ref[cc] = m_new
            return carry

        lax.fori_loop(0, nt, kv_tile, 0)
        for cc in range(ATT_KV_HEADS):
            write_out(cc, acc_ref[cc] / l_ref[cc])


def _attention(q, k, v, bias, logit_bound, batch, seq):
    t = q.shape[0]
    nq = seq // QB
    nt = seq // KT
    rows = ATT_GROUP * QB
    return pl.pallas_call(
        _attn_kernel,
        grid=(batch, nq),
        in_specs=[pl.BlockSpec((QB, D_MODEL), lambda b, i: (b * nq + i, 0)),
                  pl.BlockSpec((seq, ATT_KV_HEADS * LANES), lambda b, i: (b, 0)),
                  pl.BlockSpec((seq, ATT_KV_HEADS * LANES), lambda b, i: (b, 0)),
                  pl.BlockSpec((None, None, nt, QB, KT), lambda b, i: (b, i, 0, 0, 0)),
                  pl.BlockSpec(memory_space=pltpu.SMEM)],
        out_specs=pl.BlockSpec((QB, D_MODEL), lambda b, i: (b * nq + i, 0)),
        out_shape=jax.ShapeDtypeStruct((t, D_MODEL), F32),
        scratch_shapes=[pltpu.VMEM((ATT_KV_HEADS, rows, 1), F32),
                        pltpu.VMEM((ATT_KV_HEADS, rows, 1), F32),
                        pltpu.VMEM((ATT_KV_HEADS, rows, LANES), F32),
                        pltpu.VMEM((ATT_KV_HEADS, rows, 2 * LANES), F32)],
        compiler_params=_cparams(("arbitrary", "arbitrary")),
        name="attention",
    )(q, k, v, bias, logit_bound)


def _merge_kernel(ga_ref, gb_ref, og_ref, oa_ref, x_ref, w_ref, o_ref):
    merged = jax.nn.sigmoid(ga_ref[...]) * og_ref[...] + jax.nn.sigmoid(gb_ref[...]) * oa_ref[...]
    o_ref[...] = x_ref[...] + jnp.dot(merged.astype(BF16), w_ref[...], preferred_element_type=F32)


def _merge(proj, o_gdn, o_att, x2, w_out):
    t = x2.shape[0]
    tm = TM_MERGE
    blk = lambda c: pl.BlockSpec((tm, D_MODEL), lambda i: (i, c))
    return pl.pallas_call(
        _merge_kernel,
        grid=(t // tm,),
        in_specs=[blk(COL_GA * LANES // D_MODEL), blk(COL_GB * LANES // D_MODEL), blk(0), blk(0), blk(0),
                  pl.BlockSpec((D_MODEL, D_MODEL), lambda i: (0, 0))],
        out_specs=blk(0),
        out_shape=jax.ShapeDtypeStruct((t, D_MODEL), F32),
        compiler_params=_cparams(("arbitrary",)),
        name="merge",
    )(proj, proj, o_gdn, o_att, x2, w_out)


def _mlp_kernel(x_ref, g_ref, wu_ref, wd_ref, o_ref, h_ref, acc_ref):
    f = pl.program_id(1)

    @pl.when(f == 0)
    def _():
        x = x_ref[...]
        ms = jnp.mean(x * x, axis=-1, keepdims=True)
        h_ref[...] = (x * lax.rsqrt(ms + EPS) * g_ref[...]).astype(BF16)
        acc_ref[...] = x

    up = jnp.dot(h_ref[...], wu_ref[...], preferred_element_type=F32)
    act = jnp.square(jnp.maximum(up, 0.0))
    acc_ref[...] += jnp.dot(act.astype(BF16), wd_ref[...], preferred_element_type=F32)

    @pl.when(f == pl.num_programs(1) - 1)
    def _():
        o_ref[...] = acc_ref[...]


def _mlp(x1, g_row, w_up, w_down):
    t = x1.shape[0]
    tm = min(TM_MLP, t)
    return pl.pallas_call(
        _mlp_kernel,
        grid=(t // tm, D_FF // TF_MLP),
        in_specs=[pl.BlockSpec((tm, D_MODEL), lambda i, f: (i, 0)),
                  pl.BlockSpec((1, D_MODEL), lambda i, f: (0, 0)),
                  pl.BlockSpec((D_MODEL, TF_MLP), lambda i, f: (0, f)),
                  pl.BlockSpec((TF_MLP, D_MODEL), lambda i, f: (f, 0))],
        out_specs=pl.BlockSpec((tm, D_MODEL), lambda i, f: (i, 0)),
        out_shape=jax.ShapeDtypeStruct((t, D_MODEL), F32),
        scratch_shapes=[pltpu.VMEM((tm, D_MODEL), BF16), pltpu.VMEM((tm, D_MODEL), F32)],
        compiler_params=_cparams(("arbitrary", "arbitrary")),
        name="mlp",
    )(x1, g_row, w_up, w_down)


def _pack_w_in(w_in):
    pts = np.cumsum(np.array(IN_SPLITS))[:-1].tolist()
    (g_qkv, g_z, g_a, g_b, a_q, a_k, a_v, i_q, i_k, i_w, gate_a, gate_b) = jnp.split(w_in, pts, axis=-1)
    small = jnp.concatenate([g_a, g_b, i_k, i_w], axis=-1)
    small = jnp.pad(small, ((0, 0), (0, LANES - small.shape[-1])))
    return jnp.concatenate([g_qkv, g_z, a_q, a_k, a_v, i_q, gate_a, gate_b, small], axis=-1).astype(BF16)


def _lane_row(vec, offset=0):
    return jnp.zeros((1, LANES), F32).at[0, offset:offset + vec.shape[0]].set(vec.astype(F32))


def _rope_table(seq):
    tab = np.zeros((seq, LANES), np.float64)
    pos = np.arange(seq, dtype=np.float64)[:, None]
    for rot_dim, base in ((ATT_HEAD // ROPE_FRACTION, 0), (IDX_DIM // ROPE_FRACTION, TAB_COS_I)):
        half = rot_dim // 2
        ang = pos * ROPE_THETA ** (-np.arange(0, rot_dim, 2, dtype=np.float64) / rot_dim)[None, :]
        tab[:, base:base + half] = tab[:, base + half:base + 2 * half] = np.cos(ang)
        tab[:, base + 2 * half:base + 3 * half] = -np.sin(ang)
        tab[:, base + 3 * half:base + 4 * half] = np.sin(ang)
    return jnp.asarray(tab, dtype=F32)


def _layer(x2, batch, seq, norm_mix_g, w_in, conv_w, a_log, dt_bias, gdn_norm_g, q_norm_g, k_norm_g,
           w_out, norm_mlp_g, w_mlp_up, w_mlp_down):
    k_sel = min(TOPK_MAX, seq // 4)
    proj = _inproj(x2, norm_mix_g.reshape(1, D_MODEL), _pack_w_in(w_in))

    o_gdn = _gdn(proj, conv_w, _lane_row(a_log, SM_GA), _lane_row(dt_bias, SM_GA),
                 gdn_norm_g.reshape(1, LANES), batch, seq)

    q_att, k_att, v_att, q_idx, k_idx_t, w_idx = _prep(
        proj, _rope_table(seq), q_norm_g.reshape(1, LANES), k_norm_g.reshape(1, LANES), batch, seq)

    ii = np.arange(KT)
    ustrict = jnp.asarray((ii[:, None] < ii[None, :]).astype(np.float32), dtype=BF16)
    bias = _select(q_idx, w_idx, k_idx_t, ustrict, batch, seq, k_sel)
    logit_bound = (1.01 * ATT_HEAD * ATT_HEAD ** -0.5) * jnp.max(jnp.abs(q_norm_g)) * jnp.max(jnp.abs(k_norm_g))
    o_att = _attention(q_att, k_att, v_att, bias, logit_bound.reshape(1).astype(F32), batch, seq)

    x1 = _merge(proj, o_gdn, o_att, x2, w_out.astype(BF16))
    return _mlp(x1, norm_mlp_g.reshape(1, D_MODEL), w_mlp_up.astype(BF16), w_mlp_down.astype(BF16))


def kernel(x, norm_mix_g, w_in, conv_w, a_log, dt_bias, gdn_norm_g, q_norm_g, k_norm_g, w_out,
           norm_mlp_g, w_mlp_up, w_mlp_down):
    batch, seq, _ = x.shape
    x2 = x.reshape(batch * seq, D_MODEL)
    for layer in range(norm_mix_g.shape[0]):
        x2 = _layer(x2, batch, seq, norm_mix_g[layer], w_in[layer], conv_w[layer], a_log[layer],
                    dt_bias[layer], gdn_norm_g[layer], q_norm_g[layer], k_norm_g[layer], w_out[layer],
                    norm_mlp_g[layer], w_mlp_up[layer], w_mlp_down[layer])
    return x2.reshape(batch, seq, D_MODEL)
```

```python
import functools
import math

import numpy as np
import jax
import jax.numpy as jnp
from jax import lax
from jax.experimental import pallas as pl
from jax.experimental.pallas import tpu as pltpu

F32 = jnp.float32
BF16 = jnp.bfloat16
I32 = jnp.int32

D_MODEL = 1024
GDN_HEADS = 8
GDN_HEAD = 128
GDN_CONV = 4
GDN_CHUNK = 64
GDN_CONV_DIM = 3 * GDN_HEADS * GDN_HEAD
ATT_HEADS = 8
ATT_HEAD = 128
ATT_KV_HEADS = 2
ATT_GROUP = ATT_HEADS // ATT_KV_HEADS
IDX_HEADS = 8
IDX_DIM = 64
TOPK_MAX = 256
ROPE_THETA = 500000.0
ROPE_FRACTION = 4
D_FF = 4 * D_MODEL
EPS = 1e-6

IN_SPLITS = (GDN_CONV_DIM, D_MODEL, GDN_HEADS, GDN_HEADS, D_MODEL, 256, 256,
             IDX_HEADS * IDX_DIM, IDX_DIM, IDX_HEADS, D_MODEL, D_MODEL)

LANES = 128
COL_QKV = 0
COL_Z = 24
COL_AQ = 32
COL_AKV = 40
COL_IQ = 44
COL_GA = 48
COL_GB = 56
COL_SMALL = 64
N_COL_BLOCKS = 65
NP = N_COL_BLOCKS * LANES
SM_GA = 0
SM_GB = 8
SM_IK = 16
SM_IW = 80
TAB_SIN_A = 2 * (ATT_HEAD // ROPE_FRACTION // 2)
TAB_COS_I = 4 * (ATT_HEAD // ROPE_FRACTION // 2)
TAB_SIN_I = TAB_COS_I + 2 * (IDX_DIM // ROPE_FRACTION // 2)

NEG = -1e30
FLT_MAX = 3.4028234663852886e38
MAX_SEARCH_STEPS = 254 + 24 + 8
LOG2E = 1.4426950408889634
SAFE_LOGIT_BOUND = 60.0

TM_IN = 512
TN_IN = 13 * LANES
GDN_BLK = 512
TM_PREP = 512
QB = 128
QA = 256
KT = 512
TM_MERGE = 512
TM_MLP = 1024
TF_MLP = 512
VMEM_LIMIT = 56 * 2 ** 20


def _cparams(sem):
    return pltpu.CompilerParams(dimension_semantics=sem, vmem_limit_bytes=VMEM_LIMIT)


def _inproj_kernel(x_ref, g_ref, w_ref, o_ref):
    x = x_ref[...]
    ms = jnp.mean(x * x, axis=-1, keepdims=True)
    h = x * lax.rsqrt(ms + EPS) * g_ref[...]
    o_ref[...] = jnp.dot(h.astype(BF16), w_ref[...], preferred_element_type=F32)


def _inproj(x2, g_row, w_p):
    t = x2.shape[0]
    return pl.pallas_call(
        _inproj_kernel,
        grid=(NP // TN_IN, t // TM_IN),
        in_specs=[pl.BlockSpec((TM_IN, D_MODEL), lambda n, m: (m, 0)),
                  pl.BlockSpec((1, D_MODEL), lambda n, m: (0, 0)),
                  pl.BlockSpec((D_MODEL, TN_IN), lambda n, m: (0, n))],
        out_specs=pl.BlockSpec((TM_IN, TN_IN), lambda n, m: (m, n)),
        out_shape=jax.ShapeDtypeStruct((t, NP), F32),
        compiler_params=_cparams(("arbitrary", "arbitrary")),
        name="inproj",
    )(x2, g_row, w_p)


def _gdn_kernel(qkv_ref, z_ref, sm_ref, cw_ref, alog_ref, dtb_ref, gn_ref, o_ref,
                halo_ref, state_ref, qn_ref, kn_ref, vv_ref, gc_ref, bt_ref):
    blk = qkv_ref.shape[0]
    c = GDN_CHUNK

    @pl.when(pl.program_id(1) == 0)
    def _():
        halo_ref[...] = jnp.zeros_like(halo_ref)
        state_ref[...] = jnp.zeros_like(state_ref)

    row8 = lax.broadcasted_iota(I32, (8, LANES), 0)
    for j in range(GDN_CONV_DIM // LANES):
        lo = j * LANES
        u = qkv_ref[:, lo:lo + LANES]
        hal = halo_ref[:, lo:lo + LANES]
        w = cw_ref[:, lo:lo + LANES]
        y = u * w[GDN_CONV - 1:GDN_CONV, :]
        for sft in range(1, GDN_CONV):
            ur = pltpu.roll(u, sft, axis=0)
            hr = pltpu.roll(hal, sft, axis=0)
            top = jnp.where(row8 < sft, hr, ur[0:8, :])
            ush = jnp.concatenate([top, ur[8:, :]], axis=0)
            y = y + ush * w[GDN_CONV - 1 - sft:GDN_CONV - sft, :]
        halo_ref[:, lo:lo + LANES] = u[blk - 8:, :]
        a = y * jax.nn.sigmoid(y)
        head = j % GDN_HEADS
        if j < 2 * GDN_HEADS:
            a = a * lax.rsqrt(jnp.sum(a * a, axis=-1, keepdims=True) + EPS)
        if j < GDN_HEADS:
            qn_ref[:, head * LANES:(head + 1) * LANES] = a * (GDN_HEAD ** -0.5)
        elif j < 2 * GDN_HEADS:
            kn_ref[:, head * LANES:(head + 1) * LANES] = a
        else:
            vv_ref[:, head * LANES:(head + 1) * LANES] = a

    sm = sm_ref[...]
    xg = sm + dtb_ref[...]
    softplus = jnp.maximum(xg, 0.0) + jnp.log1p(jnp.exp(-jnp.abs(xg)))
    gc = -jnp.exp(alog_ref[...]) * softplus
    row_in_chunk = lax.broadcasted_iota(I32, (blk, LANES), 0) & (c - 1)
    d = 1
    while d < c:
        gc = gc + jnp.where(row_in_chunk >= d, pltpu.roll(gc, d, axis=0), 0.0)
        d *= 2
    gc_ref[...] = gc
    bt_ref[...] = jax.nn.sigmoid(sm)

    ii = lax.broadcasted_iota(I32, (c, c), 0)
    jj = lax.broadcasted_iota(I32, (c, c), 1)
    incl = ii >= jj
    strict = ii > jj
    gn = gn_ref[...]

    def chunk_body(ci, carry):
        r0 = pl.multiple_of(ci * c, c)
        gcc = gc_ref[pl.ds(r0, c), :]
        btc = bt_ref[pl.ds(r0, c), :]
        gct = gcc.T
        heads = range(GDN_HEADS)
        hs = [slice(h * LANES, (h + 1) * LANES) for h in heads]
        mm = lambda a, b: jnp.dot(a.astype(BF16), b.astype(BF16), preferred_element_type=F32)
        gcol = [gcc[:, SM_GA + h:SM_GA + h + 1] for h in heads]
        grow = [gct[SM_GA + h:SM_GA + h + 1, :] for h in heads]
        glast = [gcc[c - 1:c, SM_GA + h:SM_GA + h + 1] for h in heads]
        bcol = [btc[:, SM_GB + h:SM_GB + h + 1] for h in heads]
        q = [qn_ref[pl.ds(r0, c), hs[h]] for h in heads]
        k = [kn_ref[pl.ds(r0, c), hs[h]] for h in heads]
        v = [vv_ref[pl.ds(r0, c), hs[h]] for h in heads]
        kb = [k[h] * bcol[h] for h in heads]
        ab = [lax.dot_general(jnp.concatenate([kb[h], q[h]], axis=0).astype(BF16), k[h].astype(BF16),
                              (((1,), (1,)), ((), ())), preferred_element_type=F32) for h in heads]
        decay = [jnp.where(incl, jnp.exp(jnp.where(incl, gcol[h] - grow[h], 0.0)), 0.0) for h in heads]
        qk = [jnp.where(incl, ab[h][c:, :] * decay[h], 0.0) for h in heads]
        eg = [jnp.exp(gcol[h]) for h in heads]
        rhs = [jnp.concatenate([v[h] * bcol[h], kb[h] * eg[h]], axis=1) for h in heads]
        p = [-jnp.where(strict, ab[h][0:c, :] * decay[h], 0.0) for h in heads]
        r = list(p)
        for _ in range(int(math.log2(c)) - 1):
            p = [mm(p[h], p[h]) for h in heads]
            rp = [mm(r[h], p[h]) for h in heads]
            r = [r[h] + p[h] + rp[h] for h in heads]
        sol = [rhs[h] + mm(r[h], rhs[h]) for h in heads]
        q_dec = [q[h] * eg[h] for h in heads]
        k_dec_t = [(k[h] * jnp.exp(glast[h] - gcol[h])).T for h in heads]
        st = [state_ref[h] for h in heads]
        ws = [mm(jnp.concatenate([sol[h][:, LANES:], q_dec[h]], axis=0), st[h]) for h in heads]
        v_new = [sol[h][:, :LANES] - ws[h][0:c, :] for h in heads]
        o_n = [ws[h][c:, :] + mm(qk[h], v_new[h]) for h in heads]
        st_new = [st[h] * jnp.exp(glast[h]) + mm(k_dec_t[h], v_new[h]) for h in heads]
        for h in heads:
            state_ref[h] = st_new[h]
            on = o_n[h] * lax.rsqrt(jnp.mean(o_n[h] * o_n[h], axis=-1, keepdims=True) + EPS) * gn
            zz = z_ref[pl.ds(r0, c), hs[h]]
            o_ref[pl.ds(r0, c), hs[h]] = on * (zz * jax.nn.sigmoid(zz))
        return carry

    lax.fori_loop(0, blk // c, chunk_body, 0)


def _gdn(proj, conv_w, alog_row, dtb_row, gn_row, batch, seq):
    t = proj.shape[0]
    blk = min(GDN_BLK, seq)
    ns = seq // blk
    row = lambda b, s: b * ns + s
    return pl.pallas_call(
        _gdn_kernel,
        grid=(batch, ns),
        in_specs=[pl.BlockSpec((blk, GDN_CONV_DIM), lambda b, s: (row(b, s), COL_QKV * LANES // GDN_CONV_DIM)),
                  pl.BlockSpec((blk, D_MODEL), lambda b, s: (row(b, s), COL_Z * LANES // D_MODEL)),
                  pl.BlockSpec((blk, LANES), lambda b, s: (row(b, s), COL_SMALL)),
                  pl.BlockSpec((GDN_CONV, GDN_CONV_DIM), lambda b, s: (0, 0)),
                  pl.BlockSpec((1, LANES), lambda b, s: (0, 0)),
                  pl.BlockSpec((1, LANES), lambda b, s: (0, 0)),
                  pl.BlockSpec((1, LANES), lambda b, s: (0, 0))],
        out_specs=pl.BlockSpec((blk, D_MODEL), lambda b, s: (row(b, s), 0)),
        out_shape=jax.ShapeDtypeStruct((t, D_MODEL), F32),
        scratch_shapes=[pltpu.VMEM((8, GDN_CONV_DIM), F32),
                        pltpu.VMEM((GDN_HEADS, GDN_HEAD, GDN_HEAD), F32),
                        pltpu.VMEM((blk, D_MODEL), F32),
                        pltpu.VMEM((blk, D_MODEL), F32),
                        pltpu.VMEM((blk, D_MODEL), F32),
                        pltpu.VMEM((blk, LANES), F32),
                        pltpu.VMEM((blk, LANES), F32)],
        compiler_params=_cparams(("arbitrary", "arbitrary")),
        name="gdn",
    )(proj, proj, proj, conv_w, alog_row, dtb_row, gn_row)


def _rope(y, cos, sin_lo, sin_hi, half):
    return y * cos + pltpu.roll(y, LANES - half, axis=1) * sin_lo + pltpu.roll(y, half, axis=1) * sin_hi


def _prep_kernel(aq_ref, akv_ref, iq_ref, sm_ref, tab_ref, gq_ref, gk_ref,
                 q_out, k_out, v_out, qi_out, kit_out, w_out):
    half_a = ATT_HEAD // ROPE_FRACTION // 2
    half_i = IDX_DIM // ROPE_FRACTION // 2
    t = tab_ref[...]
    lane = lax.broadcasted_iota(I32, t.shape, 1)
    from_lane = lambda src: pltpu.roll(t, (-src) % LANES, axis=1)
    between = lambda lo, hi: (lane >= lo) & (lane < hi)
    ca = jnp.where(lane < 2 * half_a, t, 1.0)
    sa = from_lane(TAB_SIN_A)
    sa1 = jnp.where(lane < half_a, sa, 0.0)
    sa2 = jnp.where(between(half_a, 2 * half_a), sa, 0.0)
    lm = lane & (IDX_DIM - 1)
    low = lane < IDX_DIM
    ci = jnp.where(lm < 2 * half_i, jnp.where(low, from_lane(TAB_COS_I), from_lane(TAB_COS_I - IDX_DIM)), 1.0)
    si = jnp.where(low, from_lane(TAB_SIN_I), from_lane(TAB_SIN_I - IDX_DIM))
    si1 = jnp.where(lm < half_i, si, 0.0)
    si2 = jnp.where((lm >= half_i) & (lm < 2 * half_i), si, 0.0)
    ck = jnp.where(between(SM_IK, SM_IK + 2 * half_i), from_lane(TAB_COS_I - SM_IK), 1.0)
    sk = from_lane(TAB_SIN_I - SM_IK)
    sk1 = jnp.where(between(SM_IK, SM_IK + half_i), sk, 0.0)
    sk2 = jnp.where(between(SM_IK + half_i, SM_IK + 2 * half_i), sk, 0.0)
    gq = gq_ref[...]
    gk = gk_ref[...]
    for h in range(ATT_HEADS):
        hs = slice(h * LANES, (h + 1) * LANES)
        xh = aq_ref[:, hs]
        y = xh * lax.rsqrt(jnp.mean(xh * xh, axis=-1, keepdims=True) + EPS) * gq
        q_out[:, hs] = _rope(y, ca, sa1, sa2, half_a).astype(BF16)
    for cc in range(ATT_KV_HEADS):
        hs = slice(cc * LANES, (cc + 1) * LANES)
        xk = akv_ref[:, hs]
        y = xk * lax.rsqrt(jnp.mean(xk * xk, axis=-1, keepdims=True) + EPS) * gk
        k_out[:, hs] = _rope(y, ca, sa1, sa2, half_a).astype(BF16)
        v_out[:, hs] = akv_ref[:, ATT_KV_HEADS * LANES + cc * LANES:ATT_KV_HEADS * LANES + (cc + 1) * LANES].astype(BF16)
    for j in range(IDX_HEADS * IDX_DIM // LANES):
        hs = slice(j * LANES, (j + 1) * LANES)
        qi_out[:, hs] = _rope(iq_ref[:, hs], ci, si1, si2, half_i).astype(BF16)
    sm = sm_ref[...]
    smr = _rope(sm, ck, sk1, sk2, half_i)
    kit_out[...] = smr.T[SM_IK:SM_IK + IDX_DIM, :].astype(BF16)
    w_out[...] = sm * (IDX_HEADS ** -0.5 * IDX_DIM ** -0.5)


def _prep(proj, table, gq_row, gk_row, batch, seq):
    t = proj.shape[0]
    tm = min(TM_PREP, seq)
    nsb = seq // tm
    tab_spec = pl.BlockSpec((tm, LANES), lambda i: (i % nsb, 0))
    row_spec = pl.BlockSpec((1, LANES), lambda i: (0, 0))
    return pl.pallas_call(
        _prep_kernel,
        grid=(t // tm,),
        in_specs=[pl.BlockSpec((tm, D_MODEL), lambda i: (i, COL_AQ * LANES // D_MODEL)),
                  pl.BlockSpec((tm, 4 * LANES), lambda i: (i, COL_AKV // 4)),
                  pl.BlockSpec((tm, 4 * LANES), lambda i: (i, COL_IQ // 4)),
                  pl.BlockSpec((tm, LANES), lambda i: (i, COL_SMALL))]
                 + [tab_spec, row_spec, row_spec],
        out_specs=[pl.BlockSpec((tm, D_MODEL), lambda i: (i, 0)),
                   pl.BlockSpec((tm, ATT_KV_HEADS * LANES), lambda i: (i, 0)),
                   pl.BlockSpec((tm, ATT_KV_HEADS * LANES), lambda i: (i, 0)),
                   pl.BlockSpec((tm, IDX_HEADS * IDX_DIM), lambda i: (i, 0)),
                   pl.BlockSpec((None, None, IDX_DIM, tm), lambda i: (i // nsb, i % nsb, 0, 0)),
                   pl.BlockSpec((tm, LANES), lambda i: (i, 0))],
        out_shape=[jax.ShapeDtypeStruct((t, D_MODEL), BF16),
                   jax.ShapeDtypeStruct((t, ATT_KV_HEADS * LANES), BF16),
                   jax.ShapeDtypeStruct((t, ATT_KV_HEADS * LANES), BF16),
                   jax.ShapeDtypeStruct((t, IDX_HEADS * IDX_DIM), BF16),
                   jax.ShapeDtypeStruct((batch, nsb, IDX_DIM, tm), BF16),
                   jax.ShapeDtypeStruct((t, LANES), F32)],
        compiler_params=_cparams(("arbitrary",)),
        name="prep",
    )(proj, proj, proj, proj, table, gq_row, gk_row)


def _select_kernel(qi_ref, w_ref, kit_ref, ustrict_ref, bias_ref, keys_ref, qh_ref, *, k_sel):
    qblk = pl.program_id(1)
    nt_all = keys_ref.shape[0]
    n_groups = KT // LANES
    nt = (qblk * QB + QB + KT - 1) // KT

    for h in range(IDX_HEADS):
        qh_ref[h * QB:(h + 1) * QB, :] = qi_ref[:, h * IDX_DIM:(h + 1) * IDX_DIM]
    wb = [jnp.broadcast_to(w_ref[:, SM_IW + h:SM_IW + h + 1], (QB, LANES)) for h in range(IDX_HEADS)]
    row_g = qblk * QB + lax.broadcasted_iota(I32, (QB, LANES), 0)
    lane = lax.broadcasted_iota(I32, (QB, LANES), 1)

    def score_tile(j, carry):
        mx, mn, ge0, gt0 = carry
        lg = jnp.dot(qh_ref[...], kit_ref[j], preferred_element_type=F32)
        acc = [jnp.zeros((QB, LANES), F32) for _ in range(n_groups)]
        for h in range(IDX_HEADS):
            for g in range(n_groups):
                acc[g] = acc[g] + jnp.maximum(lg[h * QB:(h + 1) * QB, g * LANES:(g + 1) * LANES], 0.0) * wb[h]
        for g in range(n_groups):
            causal = (j * KT + g * LANES + lane) <= row_g
            sc = jnp.where(causal, acc[g], -jnp.inf)
            keys_ref[j, :, g * LANES:(g + 1) * LANES] = sc
            mx = jnp.maximum(mx, sc)
            mn = jnp.minimum(mn, jnp.where(causal, acc[g], jnp.inf))
            ge0 = ge0 + jnp.where(sc >= 0.0, 1.0, 0.0)
            gt0 = gt0 + jnp.where(sc > 0.0, 1.0, 0.0)
        return mx, mn, ge0, gt0

    zeros = jnp.zeros((QB, LANES), F32)
    mx, mn, ge0, gt0 = lax.fori_loop(0, nt, score_tile, (jnp.full((QB, LANES), -jnp.inf, F32),
                                                         jnp.full((QB, LANES), jnp.inf, F32), zeros, zeros))
    row_max = jnp.max(mx, axis=1, keepdims=True)
    row_min = jnp.min(mn, axis=1, keepdims=True)
    n_ge0 = jnp.sum(ge0, axis=1, keepdims=True)
    n_gt0 = jnp.sum(gt0, axis=1, keepdims=True)

    n_causal = (row_g[:, 0:1] + 1).astype(F32)
    k_eff = jnp.minimum(float(k_sel), n_causal)

    def count(cand, strict):
        cb = jnp.broadcast_to(cand, (QB, LANES))

        def tile(j, cnt):
            kk = keys_ref[j]
            for g in range(n_groups):
                kg = kk[:, g * LANES:(g + 1) * LANES]
                cnt = cnt + jnp.where(kg > cb if strict else kg >= cb, 1.0, 0.0)
            return cnt

        cnt = lax.fori_loop(0, nt, tile, jnp.zeros((QB, LANES), F32))
        return jnp.sum(cnt, axis=1, keepdims=True)

    to_key = lambda x: (lambda b: b ^ ((b >> 31) & 0x7FFFFFFF))(pltpu.bitcast(x, I32))
    to_val = lambda kx: pltpu.bitcast(kx ^ ((kx >> 31) & 0x7FFFFFFF), F32)

    def search_cond(st):
        return jnp.logical_and(st[0] < MAX_SEARCH_STEPS, st[1] > 0.0)

    def unsettled(lo, hi):
        mid = 0.5 * lo + 0.5 * hi
        settled = jnp.logical_or(lo == hi, jnp.logical_or(mid == lo, mid == hi))
        return jnp.max(jnp.where(settled, 0.0, 1.0))

    def search_body(st):
        it, _, lo, hi = st
        mid = 0.5 * lo + 0.5 * hi
        n = count(mid, False)
        lo = jnp.where(n >= k_eff, mid, lo)
        hi = jnp.where(n <= k_eff, mid, hi)
        return it + 1, unsettled(lo, hi), lo, hi

    take_all = n_causal <= k_eff
    above = n_gt0 > k_eff
    below = n_ge0 < k_eff
    lo0 = jnp.where(below, to_val(to_key(row_min) - 1), 0.0)
    hi0 = jnp.where(above, to_val(to_key(row_max) + 1), 0.0)
    lo0 = jnp.where(take_all, -FLT_MAX, lo0)
    hi0 = jnp.where(take_all, -FLT_MAX, hi0)
    st = lax.while_loop(search_cond, search_body, (jnp.int32(0), unsettled(lo0, hi0), lo0, hi0))
    vstar, hi_end = st[2], st[3]
    vsb = jnp.broadcast_to(vstar, (QB, KT))
    at_zero = jnp.logical_not(jnp.logical_or(jnp.logical_or(above, below), take_all))
    tied = jnp.logical_or(vstar != hi_end, jnp.logical_and(at_zero, n_ge0 > k_eff))
    any_tied = jnp.max(jnp.where(tied, 1.0, 0.0)) > 0.0

    @pl.when(any_tied)
    def _():
        need = k_eff - count(vstar, True)
        ustrict = ustrict_ref[...]

        def final_tile(j, carry):
            kk = keys_ref[j]
            tie = kk == vsb
            tie_b = jnp.where(tie, 1.0, 0.0).astype(BF16)
            rank = jnp.dot(tie_b, ustrict, preferred_element_type=F32) + carry
            keep_tie = jnp.where(rank < need, 0.0, NEG)
            bias = jnp.where(kk > vsb, 0.0, jnp.where(tie, keep_tie, NEG))
            bias_ref[j] = bias.astype(BF16)
            return carry + jnp.sum(jnp.where(tie, 1.0, 0.0), axis=1, keepdims=True)

        lax.fori_loop(0, nt, final_tile, jnp.zeros((QB, 1), F32))

    @pl.when(jnp.logical_not(any_tied))
    def _():
        def final_tile(j, carry):
            bias_ref[j] = jnp.where(keys_ref[j] >= vsb, 0.0, NEG).astype(BF16)
            return carry

        lax.fori_loop(0, nt, final_tile, 0)

    def fill_tile(j, carry):
        bias_ref[j] = jnp.full((QB, KT), NEG, BF16)
        return carry

    lax.fori_loop(nt, nt_all, fill_tile, 0)


def _select(qi, wsm, kit, ustrict, batch, seq, k_sel):
    nq = seq // QB
    nt = seq // KT
    return pl.pallas_call(
        functools.partial(_select_kernel, k_sel=k_sel),
        grid=(batch, nq),
        in_specs=[pl.BlockSpec((QB, IDX_HEADS * IDX_DIM), lambda b, q: (b * nq + q, 0)),
                  pl.BlockSpec((QB, LANES), lambda b, q: (b * nq + q, 0)),
                  pl.BlockSpec((None, nt, IDX_DIM, KT), lambda b, q: (b, 0, 0, 0)),
                  pl.BlockSpec((KT, KT), lambda b, q: (0, 0))],
        out_specs=pl.BlockSpec((None, None, nt, QB, KT), lambda b, q: (b, q, 0, 0, 0)),
        out_shape=jax.ShapeDtypeStruct((batch, nq, nt, QB, KT), BF16),
        scratch_shapes=[pltpu.VMEM((nt, QB, KT), F32),
                        pltpu.VMEM((IDX_HEADS * QB, IDX_DIM), BF16)],
        compiler_params=_cparams(("arbitrary", "arbitrary")),
        name="select",
    )(qi, wsm, kit, ustrict)


def _attn_kernel(q_ref, k_ref, v_ref, bias_ref, bound_ref, o_ref, m_ref, l_ref, acc_ref, accw_ref):
    qblk = pl.program_id(1)
    nt = (qblk * QA + QA + KT - 1) // KT
    scale = ATT_HEAD ** -0.5
    q4 = [jnp.concatenate([q_ref[:, (cc * ATT_GROUP + g) * LANES:(cc * ATT_GROUP + g + 1) * LANES]
                           for g in range(ATT_GROUP)], axis=0) for cc in range(ATT_KV_HEADS)]

    def bias_rows(j):
        b = jnp.concatenate([bias_ref[r, j] for r in range(QA // QB)], axis=0).astype(F32)
        return jnp.concatenate([b] * ATT_GROUP, axis=0)

    def write_out(cc, o):
        for g in range(ATT_GROUP):
            h = cc * ATT_GROUP + g
            o_ref[:, h * LANES:(h + 1) * LANES] = o[g * QA:(g + 1) * QA, :]

    safe = bound_ref[0] <= SAFE_LOGIT_BOUND

    @pl.when(safe)
    def _():
        accw_ref[...] = jnp.zeros_like(accw_ref)
        ones = jnp.ones((KT, LANES), BF16)

        def kv_tile(j, carry):
            r0 = pl.multiple_of(j * KT, KT)
            bias4 = bias_rows(j)
            for cc in range(ATT_KV_HEADS):
                kc = k_ref[pl.ds(r0, KT), cc * LANES:(cc + 1) * LANES]
                vc = v_ref[pl.ds(r0, KT), cc * LANES:(cc + 1) * LANES]
                s = lax.dot_general(q4[cc], kc, (((1,), (1,)), ((), ())), preferred_element_type=F32)
                p = jnp.exp2(s * (scale * LOG2E) + bias4)
                accw_ref[cc] += jnp.dot(p.astype(BF16), jnp.concatenate([vc, ones], axis=1),
                                        preferred_element_type=F32)
            return carry

        lax.fori_loop(0, nt, kv_tile, 0)
        for cc in range(ATT_KV_HEADS):
            aw = accw_ref[cc]
            write_out(cc, aw[:, :LANES] / aw[:, LANES:])

    @pl.when(jnp.logical_not(safe))
    def _():
        m_ref[...] = jnp.full_like(m_ref, NEG)
        l_ref[...] = jnp.zeros_like(l_ref)
        acc_ref[...] = jnp.zeros_like(acc_ref)

        def kv_tile(j, carry):
            r0 = pl.multiple_of(j * KT, KT)
            bias4 = bias_rows(j)
            for cc in range(ATT_KV_HEADS):
                kc = k_ref[pl.ds(r0, KT), cc * LANES:(cc + 1) * LANES]
                vc = v_ref[pl.ds(r0, KT), cc * LANES:(cc + 1) * LANES]
                s = lax.dot_general(q4[cc], kc, (((1,), (1,)), ((), ())), preferred_element_type=F32)
                s = s * scale + bias4
                m_prev = m_ref[cc]
                m_new = jnp.maximum(m_prev, jnp.max(s, axis=1, keepdims=True))
                alpha = jnp.exp(m_prev - m_new)
                p = jnp.exp(s - m_new)
                l_ref[cc] = alpha * l_ref[cc] + jnp.sum(p, axis=1, keepdims=True)
                acc_ref[cc] = alpha * acc_ref[cc] + jnp.dot(p.astype(BF16), vc, preferred_element_type=F32)
                m_ref[cc] = m_new
            return carry

        lax.fori_loop(0, nt, kv_tile, 0)
        for cc in range(ATT_KV_HEADS):
            write_out(cc, acc_ref[cc] / l_ref[cc])


def _attention(q, k, v, bias, logit_bound, batch, seq):
    t = q.shape[0]
    nq = seq // QA
    nt = seq // KT
    rows = ATT_GROUP * QA
    return pl.pallas_call(
        _attn_kernel,
        grid=(batch, nq),
        in_specs=[pl.BlockSpec((QA, D_MODEL), lambda b, i: (b * nq + i, 0)),
                  pl.BlockSpec((seq, ATT_KV_HEADS * LANES), lambda b, i: (b, 0)),
                  pl.BlockSpec((seq, ATT_KV_HEADS * LANES), lambda b, i: (b, 0)),
                  pl.BlockSpec((None, QA // QB, nt, QB, KT), lambda b, i: (b, i, 0, 0, 0)),
                  pl.BlockSpec(memory_space=pltpu.SMEM)],
        out_specs=pl.BlockSpec((QA, D_MODEL), lambda b, i: (b * nq + i, 0)),
        out_shape=jax.ShapeDtypeStruct((t, D_MODEL), F32),
        scratch_shapes=[pltpu.VMEM((ATT_KV_HEADS, rows, 1), F32),
                        pltpu.VMEM((ATT_KV_HEADS, rows, 1), F32),
                        pltpu.VMEM((ATT_KV_HEADS, rows, LANES), F32),
                        pltpu.VMEM((ATT_KV_HEADS, rows, 2 * LANES), F32)],
        compiler_params=_cparams(("arbitrary", "arbitrary")),
        name="attention",
    )(q, k, v, bias, logit_bound)


def _merge_kernel(ga_ref, gb_ref, og_ref, oa_ref, x_ref, w_ref, o_ref):
    merged = jax.nn.sigmoid(ga_ref[...]) * og_ref[...] + jax.nn.sigmoid(gb_ref[...]) * oa_ref[...]
    o_ref[...] = x_ref[...] + jnp.dot(merged.astype(BF16), w_ref[...], preferred_element_type=F32)


def _merge(proj, o_gdn, o_att, x2, w_out):
    t = x2.shape[0]
    tm = TM_MERGE
    blk = lambda c: pl.BlockSpec((tm, D_MODEL), lambda i: (i, c))
    return pl.pallas_call(
        _merge_kernel,
        grid=(t // tm,),
        in_specs=[blk(COL_GA * LANES // D_MODEL), blk(COL_GB * LANES // D_MODEL), blk(0), blk(0), blk(0),
                  pl.BlockSpec((D_MODEL, D_MODEL), lambda i: (0, 0))],
        out_specs=blk(0),
        out_shape=jax.ShapeDtypeStruct((t, D_MODEL), F32),
        compiler_params=_cparams(("arbitrary",)),
        name="merge",
    )(proj, proj, o_gdn, o_att, x2, w_out)


def _mlp_kernel(x_ref, g_ref, wu_ref, wd_ref, o_ref, h_ref, acc_ref):
    f = pl.program_id(1)

    @pl.when(f == 0)
    def _():
        x = x_ref[...]
        ms = jnp.mean(x * x, axis=-1, keepdims=True)
        h_ref[...] = (x * lax.rsqrt(ms + EPS) * g_ref[...]).astype(BF16)
        acc_ref[...] = x

    up = jnp.dot(h_ref[...], wu_ref[...], preferred_element_type=F32)
    act = jnp.square(jnp.maximum(up, 0.0))
    acc_ref[...] += jnp.dot(act.astype(BF16), wd_ref[...], preferred_element_type=F32)

    @pl.when(f == pl.num_programs(1) - 1)
    def _():
        o_ref[...] = acc_ref[...]


def _mlp(x1, g_row, w_up, w_down):
    t = x1.shape[0]
    tm = min(TM_MLP, t)
    return pl.pallas_call(
        _mlp_kernel,
        grid=(t // tm, D_FF // TF_MLP),
        in_specs=[pl.BlockSpec((tm, D_MODEL), lambda i, f: (i, 0)),
                  pl.BlockSpec((1, D_MODEL), lambda i, f: (0, 0)),
                  pl.BlockSpec((D_MODEL, TF_MLP), lambda i, f: (0, f)),
                  pl.BlockSpec((TF_MLP, D_MODEL), lambda i, f: (f, 0))],
        out_specs=pl.BlockSpec((tm, D_MODEL), lambda i, f: (i, 0)),
        out_shape=jax.ShapeDtypeStruct((t, D_MODEL), F32),
        scratch_shapes=[pltpu.VMEM((tm, D_MODEL), BF16), pltpu.VMEM((tm, D_MODEL), F32)],
        compiler_params=_cparams(("arbitrary", "arbitrary")),
        name="mlp",
    )(x1, g_row, w_up, w_down)


def _pack_w_in(w_in):
    pts = np.cumsum(np.array(IN_SPLITS))[:-1].tolist()
    (g_qkv, g_z, g_a, g_b, a_q, a_k, a_v, i_q, i_k, i_w, gate_a, gate_b) = jnp.split(w_in, pts, axis=-1)
    small = jnp.concatenate([g_a, g_b, i_k, i_w], axis=-1)
    small = jnp.pad(small, ((0, 0), (0, LANES - small.shape[-1])))
    return jnp.concatenate([g_qkv, g_z, a_q, a_k, a_v, i_q, gate_a, gate_b, small], axis=-1).astype(BF16)


def _lane_row(vec, offset=0):
    return jnp.zeros((1, LANES), F32).at[0, offset:offset + vec.shape[0]].set(vec.astype(F32))


def _rope_table(seq):
    tab = np.zeros((seq, LANES), np.float64)
    pos = np.arange(seq, dtype=np.float64)[:, None]
    for rot_dim, base in ((ATT_HEAD // ROPE_FRACTION, 0), (IDX_DIM // ROPE_FRACTION, TAB_COS_I)):
        half = rot_dim // 2
        ang = pos * ROPE_THETA ** (-np.arange(0, rot_dim, 2, dtype=np.float64) / rot_dim)[None, :]
        tab[:, base:base + half] = tab[:, base + half:base + 2 * half] = np.cos(ang)
        tab[:, base + 2 * half:base + 3 * half] = -np.sin(ang)
        tab[:, base + 3 * half:base + 4 * half] = np.sin(ang)
    return jnp.asarray(tab, dtype=F32)


def _layer(x2, batch, seq, norm_mix_g, w_in, conv_w, a_log, dt_bias, gdn_norm_g, q_norm_g, k_norm_g,
           w_out, norm_mlp_g, w_mlp_up, w_mlp_down):
    k_sel = min(TOPK_MAX, seq // 4)
    proj = _inproj(x2, norm_mix_g.reshape(1, D_MODEL), _pack_w_in(w_in))

    o_gdn = _gdn(proj, conv_w, _lane_row(a_log, SM_GA), _lane_row(dt_bias, SM_GA),
                 gdn_norm_g.reshape(1, LANES), batch, seq)

    q_att, k_att, v_att, q_idx, k_idx_t, w_idx = _prep(
        proj, _rope_table(seq), q_norm_g.reshape(1, LANES), k_norm_g.reshape(1, LANES), batch, seq)

    ii = np.arange(KT)
    ustrict = jnp.asarray((ii[:, None] < ii[None, :]).astype(np.float32), dtype=BF16)
    bias = _select(q_idx, w_idx, k_idx_t, ustrict, batch, seq, k_sel)
    logit_bound = (1.01 * ATT_HEAD * ATT_HEAD ** -0.5) * jnp.max(jnp.abs(q_norm_g)) * jnp.max(jnp.abs(k_norm_g))
    o_att = _attention(q_att, k_att, v_att, bias, logit_bound.reshape(1).astype(F32), batch, seq)

    x1 = _merge(proj, o_gdn, o_att, x2, w_out.astype(BF16))
    return _mlp(x1, norm_mlp_g.reshape(1, D_MODEL), w_mlp_up.astype(BF16), w_mlp_down.astype(BF16))


def kernel(x, norm_mix_g, w_in, conv_w, a_log, dt_bias, gdn_norm_g, q_norm_g, k_norm_g, w_out,
           norm_mlp_g, w_mlp_up, w_mlp_down):
    batch, seq, _ = x.shape
    x2 = x.reshape(batch * seq, D_MODEL)
    for layer in range(norm_mix_g.shape[0]):
        x2 = _layer(x2, batch, seq, norm_mix_g[layer], w_in[layer], conv_w[layer], a_log[layer],
                    dt_bias[layer], gdn_norm_g[layer], q_norm_g[layer], k_norm_g[layer], w_out[layer],
                    norm_mlp_g[layer], w_mlp_up[layer], w_mlp_down[layer])
    return x2.reshape(batch, seq, D_MODEL)
```

```python
import functools
import math

import numpy as np
import jax
import jax.numpy as jnp
from jax import lax
from jax.experimental import pallas as pl
from jax.experimental.pallas import tpu as pltpu

F32 = jnp.float32
BF16 = jnp.bfloat16
I32 = jnp.int32

D_MODEL = 1024
GDN_HEADS = 8
GDN_HEAD = 128
GDN_CONV = 4
GDN_CHUNK = 64
GDN_CONV_DIM = 3 * GDN_HEADS * GDN_HEAD
ATT_HEADS = 8
ATT_HEAD = 128
ATT_KV_HEADS = 2
ATT_GROUP = ATT_HEADS // ATT_KV_HEADS
IDX_HEADS = 8
IDX_DIM = 64
TOPK_MAX = 256
ROPE_THETA = 500000.0
ROPE_FRACTION = 4
D_FF = 4 * D_MODEL
EPS = 1e-6

IN_SPLITS = (GDN_CONV_DIM, D_MODEL, GDN_HEADS, GDN_HEADS, D_MODEL, 256, 256,
             IDX_HEADS * IDX_DIM, IDX_DIM, IDX_HEADS, D_MODEL, D_MODEL)

LANES = 128
COL_QKV = 0
COL_Z = 24
COL_AQ = 32
COL_AKV = 40
COL_IQ = 44
COL_GA = 48
COL_GB = 56
SM_GA = 0
SM_GB = 8
SM_IK = 16
SM_IW = 80
TAB_SIN_A = 2 * (ATT_HEAD // ROPE_FRACTION // 2)
TAB_COS_I = 4 * (ATT_HEAD // ROPE_FRACTION // 2)
TAB_SIN_I = TAB_COS_I + 2 * (IDX_DIM // ROPE_FRACTION // 2)

NEG = -1e30
FLT_MAX = 3.4028234663852886e38
MAX_SEARCH_STEPS = 254 + 24 + 8
LOG2E = 1.4426950408889634
SAFE_LOGIT_BOUND = 60.0

TM_IN = 512
TN_IN = 16 * LANES
GDN_BLK = 512
TM_PREP = 512
QB = 128
QA = 256
KT = 512
TM_MERGE = 512
TM_MLP = 1024
TF_MLP = 512
VMEM_LIMIT = 56 * 2 ** 20


def _cparams(sem):
    return pltpu.CompilerParams(dimension_semantics=sem, vmem_limit_bytes=VMEM_LIMIT)


def _inproj_kernel(x_ref, g_ref, w_ref, o_ref):
    x = x_ref[...]
    ms = jnp.mean(x * x, axis=-1, keepdims=True)
    h = x * lax.rsqrt(ms + EPS) * g_ref[...]
    o_ref[...] = jnp.dot(h.astype(BF16), w_ref[...], preferred_element_type=F32).astype(o_ref.dtype)


def _inproj(x2, g_row, w_p, tn, out_dtype, name):
    t = x2.shape[0]
    n_cols = w_p.shape[1]
    return pl.pallas_call(
        _inproj_kernel,
        grid=(n_cols // tn, t // TM_IN),
        in_specs=[pl.BlockSpec((TM_IN, D_MODEL), lambda n, m: (m, 0)),
                  pl.BlockSpec((1, D_MODEL), lambda n, m: (0, 0)),
                  pl.BlockSpec((D_MODEL, tn), lambda n, m: (0, n))],
        out_specs=pl.BlockSpec((TM_IN, tn), lambda n, m: (m, n)),
        out_shape=jax.ShapeDtypeStruct((t, n_cols), out_dtype),
        compiler_params=_cparams(("arbitrary", "arbitrary")),
        name=name,
    )(x2, g_row, w_p)


def _gdn_kernel(qkv_ref, z_ref, sm_ref, cw_ref, alog_ref, dtb_ref, gn_ref, o_ref,
                halo_ref, state_ref, qn_ref, kn_ref, vv_ref, gc_ref, bt_ref):
    blk = qkv_ref.shape[0]
    c = GDN_CHUNK

    @pl.when(pl.program_id(1) == 0)
    def _():
        halo_ref[...] = jnp.zeros_like(halo_ref)
        state_ref[...] = jnp.zeros_like(state_ref)

    row8 = lax.broadcasted_iota(I32, (8, LANES), 0)
    for j in range(GDN_CONV_DIM // LANES):
        lo = j * LANES
        u = qkv_ref[:, lo:lo + LANES].astype(F32)
        hal = halo_ref[:, lo:lo + LANES]
        w = cw_ref[:, lo:lo + LANES]
        y = u * w[GDN_CONV - 1:GDN_CONV, :]
        for sft in range(1, GDN_CONV):
            ur = pltpu.roll(u, sft, axis=0)
            hr = pltpu.roll(hal, sft, axis=0)
            top = jnp.where(row8 < sft, hr, ur[0:8, :])
            ush = jnp.concatenate([top, ur[8:, :]], axis=0)
            y = y + ush * w[GDN_CONV - 1 - sft:GDN_CONV - sft, :]
        halo_ref[:, lo:lo + LANES] = u[blk - 8:, :]
        a = y * jax.nn.sigmoid(y)
        head = j % GDN_HEADS
        if j < 2 * GDN_HEADS:
            a = a * lax.rsqrt(jnp.sum(a * a, axis=-1, keepdims=True) + EPS)
        if j < GDN_HEADS:
            qn_ref[:, head * LANES:(head + 1) * LANES] = a * (GDN_HEAD ** -0.5)
        elif j < 2 * GDN_HEADS:
            kn_ref[:, head * LANES:(head + 1) * LANES] = a
        else:
            vv_ref[:, head * LANES:(head + 1) * LANES] = a

    sm = sm_ref[...]
    xg = sm + dtb_ref[...]
    softplus = jnp.maximum(xg, 0.0) + jnp.log1p(jnp.exp(-jnp.abs(xg)))
    gc = -jnp.exp(alog_ref[...]) * softplus
    row_in_chunk = lax.broadcasted_iota(I32, (blk, LANES), 0) & (c - 1)
    d = 1
    while d < c:
        gc = gc + jnp.where(row_in_chunk >= d, pltpu.roll(gc, d, axis=0), 0.0)
        d *= 2
    gc_ref[...] = gc
    bt_ref[...] = jax.nn.sigmoid(sm)

    ii = lax.broadcasted_iota(I32, (c, c), 0)
    jj = lax.broadcasted_iota(I32, (c, c), 1)
    incl = ii >= jj
    strict = ii > jj
    gn = gn_ref[...]

    def chunk_body(ci, carry):
        r0 = pl.multiple_of(ci * c, c)
        gcc = gc_ref[pl.ds(r0, c), :]
        btc = bt_ref[pl.ds(r0, c), :]
        gct = gcc.T
        heads = range(GDN_HEADS)
        hs = [slice(h * LANES, (h + 1) * LANES) for h in heads]
        mm = lambda a, b: jnp.dot(a.astype(BF16), b.astype(BF16), preferred_element_type=F32)
        gcol = [gcc[:, SM_GA + h:SM_GA + h + 1] for h in heads]
        grow = [gct[SM_GA + h:SM_GA + h + 1, :] for h in heads]
        glast = [gcc[c - 1:c, SM_GA + h:SM_GA + h + 1] for h in heads]
        bcol = [btc[:, SM_GB + h:SM_GB + h + 1] for h in heads]
        q = [qn_ref[pl.ds(r0, c), hs[h]] for h in heads]
        k = [kn_ref[pl.ds(r0, c), hs[h]] for h in heads]
        v = [vv_ref[pl.ds(r0, c), hs[h]] for h in heads]
        kb = [k[h] * bcol[h] for h in heads]
        ab = [lax.dot_general(jnp.concatenate([kb[h], q[h]], axis=0).astype(BF16), k[h].astype(BF16),
                              (((1,), (1,)), ((), ())), preferred_element_type=F32) for h in heads]
        decay = [jnp.where(incl, jnp.exp(jnp.where(incl, gcol[h] - grow[h], 0.0)), 0.0) for h in heads]
        qk = [jnp.where(incl, ab[h][c:, :] * decay[h], 0.0) for h in heads]
        eg = [jnp.exp(gcol[h]) for h in heads]
        rhs = [jnp.concatenate([v[h] * bcol[h], kb[h] * eg[h]], axis=1) for h in heads]
        p = [-jnp.where(strict, ab[h][0:c, :] * decay[h], 0.0) for h in heads]
        r = list(p)
        for _ in range(int(math.log2(c)) - 1):
            p = [mm(p[h], p[h]) for h in heads]
            rp = [mm(r[h], p[h]) for h in heads]
            r = [r[h] + p[h] + rp[h] for h in heads]
        sol = [rhs[h] + mm(r[h], rhs[h]) for h in heads]
        q_dec = [q[h] * eg[h] for h in heads]
        k_dec_t = [(k[h] * jnp.exp(glast[h] - gcol[h])).T for h in heads]
        st = [state_ref[h] for h in heads]
        ws = [mm(jnp.concatenate([sol[h][:, LANES:], q_dec[h]], axis=0), st[h]) for h in heads]
        v_new = [sol[h][:, :LANES] - ws[h][0:c, :] for h in heads]
        o_n = [ws[h][c:, :] + mm(qk[h], v_new[h]) for h in heads]
        st_new = [st[h] * jnp.exp(glast[h]) + mm(k_dec_t[h], v_new[h]) for h in heads]
        for h in heads:
            state_ref[h] = st_new[h]
            on = o_n[h] * lax.rsqrt(jnp.mean(o_n[h] * o_n[h], axis=-1, keepdims=True) + EPS) * gn
            zz = z_ref[pl.ds(r0, c), hs[h]].astype(F32)
            o_ref[pl.ds(r0, c), hs[h]] = (on * (zz * jax.nn.sigmoid(zz))).astype(o_ref.dtype)
        return carry

    lax.fori_loop(0, blk // c, chunk_body, 0)


def _gdn(proj, small, conv_w, alog_row, dtb_row, gn_row, batch, seq):
    t = proj.shape[0]
    blk = min(GDN_BLK, seq)
    ns = seq // blk
    row = lambda b, s: b * ns + s
    return pl.pallas_call(
        _gdn_kernel,
        grid=(batch, ns),
        in_specs=[pl.BlockSpec((blk, GDN_CONV_DIM), lambda b, s: (row(b, s), COL_QKV * LANES // GDN_CONV_DIM)),
                  pl.BlockSpec((blk, D_MODEL), lambda b, s: (row(b, s), COL_Z * LANES // D_MODEL)),
                  pl.BlockSpec((blk, LANES), lambda b, s: (row(b, s), 0)),
                  pl.BlockSpec((GDN_CONV, GDN_CONV_DIM), lambda b, s: (0, 0)),
                  pl.BlockSpec((1, LANES), lambda b, s: (0, 0)),
                  pl.BlockSpec((1, LANES), lambda b, s: (0, 0)),
                  pl.BlockSpec((1, LANES), lambda b, s: (0, 0))],
        out_specs=pl.BlockSpec((blk, D_MODEL), lambda b, s: (row(b, s), 0)),
        out_shape=jax.ShapeDtypeStruct((t, D_MODEL), BF16),
        scratch_shapes=[pltpu.VMEM((8, GDN_CONV_DIM), F32),
                        pltpu.VMEM((GDN_HEADS, GDN_HEAD, GDN_HEAD), F32),
                        pltpu.VMEM((blk, D_MODEL), F32),
                        pltpu.VMEM((blk, D_MODEL), F32),
                        pltpu.VMEM((blk, D_MODEL), F32),
                        pltpu.VMEM((blk, LANES), F32),
                        pltpu.VMEM((blk, LANES), F32)],
        compiler_params=_cparams(("arbitrary", "arbitrary")),
        name="gdn",
    )(proj, proj, small, conv_w, alog_row, dtb_row, gn_row)


def _rope(y, cos, sin_lo, sin_hi, half):
    return y * cos + pltpu.roll(y, LANES - half, axis=1) * sin_lo + pltpu.roll(y, half, axis=1) * sin_hi


def _prep_kernel(aq_ref, akv_ref, iq_ref, sm_ref, tab_ref, gq_ref, gk_ref,
                 q_out, k_out, v_out, qi_out, kit_out, w_out):
    half_a = ATT_HEAD // ROPE_FRACTION // 2
    half_i = IDX_DIM // ROPE_FRACTION // 2
    t = tab_ref[...]
    lane = lax.broadcasted_iota(I32, t.shape, 1)
    from_lane = lambda src: pltpu.roll(t, (-src) % LANES, axis=1)
    between = lambda lo, hi: (lane >= lo) & (lane < hi)
    ca = jnp.where(lane < 2 * half_a, t, 1.0)
    sa = from_lane(TAB_SIN_A)
    sa1 = jnp.where(lane < half_a, sa, 0.0)
    sa2 = jnp.where(between(half_a, 2 * half_a), sa, 0.0)
    lm = lane & (IDX_DIM - 1)
    low = lane < IDX_DIM
    ci = jnp.where(lm < 2 * half_i, jnp.where(low, from_lane(TAB_COS_I), from_lane(TAB_COS_I - IDX_DIM)), 1.0)
    si = jnp.where(low, from_lane(TAB_SIN_I), from_lane(TAB_SIN_I - IDX_DIM))
    si1 = jnp.where(lm < half_i, si, 0.0)
    si2 = jnp.where((lm >= half_i) & (lm < 2 * half_i), si, 0.0)
    ck = jnp.where(between(SM_IK, SM_IK + 2 * half_i), from_lane(TAB_COS_I - SM_IK), 1.0)
    sk = from_lane(TAB_SIN_I - SM_IK)
    sk1 = jnp.where(between(SM_IK, SM_IK + half_i), sk, 0.0)
    sk2 = jnp.where(between(SM_IK + half_i, SM_IK + 2 * half_i), sk, 0.0)
    gq = gq_ref[...]
    gk = gk_ref[...]
    for h in range(ATT_HEADS):
        hs = slice(h * LANES, (h + 1) * LANES)
        xh = aq_ref[:, hs].astype(F32)
        y = xh * lax.rsqrt(jnp.mean(xh * xh, axis=-1, keepdims=True) + EPS) * gq
        q_out[:, hs] = _rope(y, ca, sa1, sa2, half_a).astype(BF16)
    for cc in range(ATT_KV_HEADS):
        hs = slice(cc * LANES, (cc + 1) * LANES)
        xk = akv_ref[:, hs].astype(F32)
        y = xk * lax.rsqrt(jnp.mean(xk * xk, axis=-1, keepdims=True) + EPS) * gk
        k_out[:, hs] = _rope(y, ca, sa1, sa2, half_a).astype(BF16)
        v_out[:, hs] = akv_ref[:, ATT_KV_HEADS * LANES + cc * LANES:ATT_KV_HEADS * LANES + (cc + 1) * LANES].astype(BF16)
    for j in range(IDX_HEADS * IDX_DIM // LANES):
        hs = slice(j * LANES, (j + 1) * LANES)
        qi_out[:, hs] = _rope(iq_ref[:, hs].astype(F32), ci, si1, si2, half_i).astype(BF16)
    sm = sm_ref[...]
    smr = _rope(sm, ck, sk1, sk2, half_i)
    kit_out[...] = smr.T[SM_IK:SM_IK + IDX_DIM, :].astype(BF16)
    w_out[...] = sm * (IDX_HEADS ** -0.5 * IDX_DIM ** -0.5)


def _prep(proj, small, table, gq_row, gk_row, batch, seq):
    t = proj.shape[0]
    tm = min(TM_PREP, seq)
    nsb = seq // tm
    tab_spec = pl.BlockSpec((tm, LANES), lambda i: (i % nsb, 0))
    row_spec = pl.BlockSpec((1, LANES), lambda i: (0, 0))
    return pl.pallas_call(
        _prep_kernel,
        grid=(t // tm,),
        in_specs=[pl.BlockSpec((tm, D_MODEL), lambda i: (i, COL_AQ * LANES // D_MODEL)),
                  pl.BlockSpec((tm, 4 * LANES), lambda i: (i, COL_AKV // 4)),
                  pl.BlockSpec((tm, 4 * LANES), lambda i: (i, COL_IQ // 4)),
                  pl.BlockSpec((tm, LANES), lambda i: (i, 0))]
                 + [tab_spec, row_spec, row_spec],
        out_specs=[pl.BlockSpec((tm, D_MODEL), lambda i: (i, 0)),
                   pl.BlockSpec((tm, ATT_KV_HEADS * LANES), lambda i: (i, 0)),
                   pl.BlockSpec((tm, ATT_KV_HEADS * LANES), lambda i: (i, 0)),
                   pl.BlockSpec((tm, IDX_HEADS * IDX_DIM), lambda i: (i, 0)),
                   pl.BlockSpec((None, None, IDX_DIM, tm), lambda i: (i // nsb, i % nsb, 0, 0)),
                   pl.BlockSpec((tm, LANES), lambda i: (i, 0))],
        out_shape=[jax.ShapeDtypeStruct((t, D_MODEL), BF16),
                   jax.ShapeDtypeStruct((t, ATT_KV_HEADS * LANES), BF16),
                   jax.ShapeDtypeStruct((t, ATT_KV_HEADS * LANES), BF16),
                   jax.ShapeDtypeStruct((t, IDX_HEADS * IDX_DIM), BF16),
                   jax.ShapeDtypeStruct((batch, nsb, IDX_DIM, tm), BF16),
                   jax.ShapeDtypeStruct((t, LANES), F32)],
        compiler_params=_cparams(("arbitrary",)),
        name="prep",
    )(proj, proj, proj, small, table, gq_row, gk_row)


def _select_kernel(qi_ref, w_ref, kit_ref, ustrict_ref, bias_ref, keys_ref, qh_ref, *, k_sel):
    qblk = pl.program_id(1)
    nt_all = keys_ref.shape[0]
    n_groups = KT // LANES
    nt = (qblk * QB + QB + KT - 1) // KT

    for h in range(IDX_HEADS):
        qh_ref[h * QB:(h + 1) * QB, :] = qi_ref[:, h * IDX_DIM:(h + 1) * IDX_DIM]
    wb = [jnp.broadcast_to(w_ref[:, SM_IW + h:SM_IW + h + 1], (QB, LANES)) for h in range(IDX_HEADS)]
    row_g = qblk * QB + lax.broadcasted_iota(I32, (QB, LANES), 0)
    lane = lax.broadcasted_iota(I32, (QB, LANES), 1)

    def score_tile(j, carry):
        mx, mn, ge0, gt0 = carry
        lg = jnp.dot(qh_ref[...], kit_ref[j], preferred_element_type=F32)
        acc = [jnp.zeros((QB, LANES), F32) for _ in range(n_groups)]
        for h in range(IDX_HEADS):
            for g in range(n_groups):
                acc[g] = acc[g] + jnp.maximum(lg[h * QB:(h + 1) * QB, g * LANES:(g + 1) * LANES], 0.0) * wb[h]
        for g in range(n_groups):
            causal = (j * KT + g * LANES + lane) <= row_g
            sc = jnp.where(causal, acc[g], -jnp.inf)
            keys_ref[j, :, g * LANES:(g + 1) * LANES] = sc
            mx = jnp.maximum(mx, sc)
            mn = jnp.minimum(mn, jnp.where(causal, acc[g], jnp.inf))
            ge0 = ge0 + jnp.where(sc >= 0.0, 1.0, 0.0)
            gt0 = gt0 + jnp.where(sc > 0.0, 1.0, 0.0)
        return mx, mn, ge0, gt0

    zeros = jnp.zeros((QB, LANES), F32)
    mx, mn, ge0, gt0 = lax.fori_loop(0, nt, score_tile, (jnp.full((QB, LANES), -jnp.inf, F32),
                                                         jnp.full((QB, LANES), jnp.inf, F32), zeros, zeros))
    row_max = jnp.max(mx, axis=1, keepdims=True)
    row_min = jnp.min(mn, axis=1, keepdims=True)
    n_ge0 = jnp.sum(ge0, axis=1, keepdims=True)
    n_gt0 = jnp.sum(gt0, axis=1, keepdims=True)

    n_causal = (row_g[:, 0:1] + 1).astype(F32)
    k_eff = jnp.minimum(float(k_sel), n_causal)

    def count(cand, strict):
        cb = jnp.broadcast_to(cand, (QB, LANES))

        def tile(j, cnt):
            kk = keys_ref[j]
            for g in range(n_groups):
                kg = kk[:, g * LANES:(g + 1) * LANES]
                cnt = cnt + jnp.where(kg > cb if strict else kg >= cb, 1.0, 0.0)
            return cnt

        cnt = lax.fori_loop(0, nt, tile, jnp.zeros((QB, LANES), F32))
        return jnp.sum(cnt, axis=1, keepdims=True)

    to_key = lambda x: (lambda b: b ^ ((b >> 31) & 0x7FFFFFFF))(pltpu.bitcast(x, I32))
    to_val = lambda kx: pltpu.bitcast(kx ^ ((kx >> 31) & 0x7FFFFFFF), F32)

    def search_cond(st):
        return jnp.logical_and(st[0] < MAX_SEARCH_STEPS, st[1] > 0.0)

    def unsettled(lo, hi):
        mid = 0.5 * lo + 0.5 * hi
        settled = jnp.logical_or(lo == hi, jnp.logical_or(mid == lo, mid == hi))
        return jnp.max(jnp.where(settled, 0.0, 1.0))

    def search_body(st):
        it, _, lo, hi = st
        mid = 0.5 * lo + 0.5 * hi
        n = count(mid, False)
        lo = jnp.where(n >= k_eff, mid, lo)
        hi = jnp.where(n <= k_eff, mid, hi)
        return it + 1, unsettled(lo, hi), lo, hi

    take_all = n_causal <= k_eff
    above = n_gt0 > k_eff
    below = n_ge0 < k_eff
    lo0 = jnp.where(below, to_val(to_key(row_min) - 1), 0.0)
    hi0 = jnp.where(above, to_val(to_key(row_max) + 1), 0.0)
    lo0 = jnp.where(take_all, -FLT_MAX, lo0)
    hi0 = jnp.where(take_all, -FLT_MAX, hi0)
    st = lax.while_loop(search_cond, search_body, (jnp.int32(0), unsettled(lo0, hi0), lo0, hi0))
    vstar, hi_end = st[2], st[3]
    vsb = jnp.broadcast_to(vstar, (QB, KT))
    at_zero = jnp.logical_not(jnp.logical_or(jnp.logical_or(above, below), take_all))
    tied = jnp.logical_or(vstar != hi_end, jnp.logical_and(at_zero, n_ge0 > k_eff))
    any_tied = jnp.max(jnp.where(tied, 1.0, 0.0)) > 0.0

    @pl.when(any_tied)
    def _():
        need = k_eff - count(vstar, True)
        ustrict = ustrict_ref[...]

        def final_tile(j, carry):
            kk = keys_ref[j]
            tie = kk == vsb
            tie_b = jnp.where(tie, 1.0, 0.0).astype(BF16)
            rank = jnp.dot(tie_b, ustrict, preferred_element_type=F32) + carry
            keep_tie = jnp.where(rank < need, 0.0, NEG)
            bias = jnp.where(kk > vsb, 0.0, jnp.where(tie, keep_tie, NEG))
            bias_ref[j] = bias.astype(BF16)
            return carry + jnp.sum(jnp.where(tie, 1.0, 0.0), axis=1, keepdims=True)

        lax.fori_loop(0, nt, final_tile, jnp.zeros((QB, 1), F32))

    @pl.when(jnp.logical_not(any_tied))
    def _():
        def final_tile(j, carry):
            bias_ref[j] = jnp.where(keys_ref[j] >= vsb, 0.0, NEG).astype(BF16)
            return carry

        lax.fori_loop(0, nt, final_tile, 0)

    def fill_tile(j, carry):
        bias_ref[j] = jnp.full((QB, KT), NEG, BF16)
        return carry

    lax.fori_loop(nt, nt_all, fill_tile, 0)


def _select(qi, wsm, kit, ustrict, batch, seq, k_sel):
    nq = seq // QB
    nt = seq // KT
    return pl.pallas_call(
        functools.partial(_select_kernel, k_sel=k_sel),
        grid=(batch, nq),
        in_specs=[pl.BlockSpec((QB, IDX_HEADS * IDX_DIM), lambda b, q: (b * nq + q, 0)),
                  pl.BlockSpec((QB, LANES), lambda b, q: (b * nq + q, 0)),
                  pl.BlockSpec((None, nt, IDX_DIM, KT), lambda b, q: (b, 0, 0, 0)),
                  pl.BlockSpec((KT, KT), lambda b, q: (0, 0))],
        out_specs=pl.BlockSpec((None, None, nt, QB, KT), lambda b, q: (b, q, 0, 0, 0)),
        out_shape=jax.ShapeDtypeStruct((batch, nq, nt, QB, KT), BF16),
        scratch_shapes=[pltpu.VMEM((nt, QB, KT), F32),
                        pltpu.VMEM((IDX_HEADS * QB, IDX_DIM), BF16)],
        compiler_params=_cparams(("arbitrary", "arbitrary")),
        name="select",
    )(qi, wsm, kit, ustrict)


def _attn_kernel(q_ref, k_ref, v_ref, bias_ref, bound_ref, o_ref, m_ref, l_ref, acc_ref, accw_ref):
    qblk = pl.program_id(1)
    nt = (qblk * QA + QA + KT - 1) // KT
    scale = ATT_HEAD ** -0.5
    q4 = [jnp.concatenate([q_ref[:, (cc * ATT_GROUP + g) * LANES:(cc * ATT_GROUP + g + 1) * LANES]
                           for g in range(ATT_GROUP)], axis=0) for cc in range(ATT_KV_HEADS)]

    def bias_rows(j):
        b = jnp.concatenate([bias_ref[r, j] for r in range(QA // QB)], axis=0).astype(F32)
        return jnp.concatenate([b] * ATT_GROUP, axis=0)

    def write_out(cc, o):
        for g in range(ATT_GROUP):
            h = cc * ATT_GROUP + g
            o_ref[:, h * LANES:(h + 1) * LANES] = o[g * QA:(g + 1) * QA, :].astype(o_ref.dtype)

    safe = bound_ref[0] <= SAFE_LOGIT_BOUND

    @pl.when(safe)
    def _():
        accw_ref[...] = jnp.zeros_like(accw_ref)
        ones = jnp.ones((KT, LANES), BF16)

        def kv_tile(j, carry):
            r0 = pl.multiple_of(j * KT, KT)
            bias4 = bias_rows(j)
            for cc in range(ATT_KV_HEADS):
                kc = k_ref[pl.ds(r0, KT), cc * LANES:(cc + 1) * LANES]
                vc = v_ref[pl.ds(r0, KT), cc * LANES:(cc + 1) * LANES]
                s = lax.dot_general(q4[cc], kc, (((1,), (1,)), ((), ())), preferred_element_type=F32)
                p = jnp.exp2(s * (scale * LOG2E) + bias4)
                accw_ref[cc] += jnp.dot(p.astype(BF16), jnp.concatenate([vc, ones], axis=1),
                                        preferred_element_type=F32)
            return carry

        lax.fori_loop(0, nt, kv_tile, 0)
        for cc in range(ATT_KV_HEADS):
            aw = accw_ref[cc]
            write_out(cc, aw[:, :LANES] / aw[:, LANES:])

    @pl.when(jnp.logical_not(safe))
    def _():
        m_ref[...] = jnp.full_like(m_ref, NEG)
        l_ref[...] = jnp.zeros_like(l_ref)
        acc_ref[...] = jnp.zeros_like(acc_ref)

        def kv_tile(j, carry):
            r0 = pl.multiple_of(j * KT, KT)
            bias4 = bias_rows(j)
            for cc in range(ATT_KV_HEADS):
                kc = k_ref[pl.ds(r0, KT), cc * LANES:(cc + 1) * LANES]
                vc = v_ref[pl.ds(r0, KT), cc * LANES:(cc + 1) * LANES]
                s = lax.dot_general(q4[cc], kc, (((1,), (1,)), ((), ())), preferred_element_type=F32)
                s = s * scale + bias4
                m_prev = m_ref[cc]
                m_new = jnp.maximum(m_prev, jnp.max(s, axis=1, keepdims=True))
                alpha = jnp.exp(m_prev - m_new)
                p = jnp.exp(s - m_new)
                l_ref[cc] = alpha * l_ref[cc] + jnp.sum(p, axis=1, keepdims=True)
                acc_ref[cc] = alpha * acc_ref[cc] + jnp.dot(p.astype(BF16), vc, preferred_element_type=F32)
                m_ref[cc] = m_new
            return carry

        lax.fori_loop(0, nt, kv_tile, 0)
        for cc in range(ATT_KV_HEADS):
            write_out(cc, acc_ref[cc] / l_ref[cc])


def _attention(q, k, v, bias, logit_bound, batch, seq):
    t = q.shape[0]
    nq = seq // QA
    nt = seq // KT
    rows = ATT_GROUP * QA
    return pl.pallas_call(
        _attn_kernel,
        grid=(batch, nq),
        in_specs=[pl.BlockSpec((QA, D_MODEL), lambda b, i: (b * nq + i, 0)),
                  pl.BlockSpec((seq, ATT_KV_HEADS * LANES), lambda b, i: (b, 0)),
                  pl.BlockSpec((seq, ATT_KV_HEADS * LANES), lambda b, i: (b, 0)),
                  pl.BlockSpec((None, QA // QB, nt, QB, KT), lambda b, i: (b, i, 0, 0, 0)),
                  pl.BlockSpec(memory_space=pltpu.SMEM)],
        out_specs=pl.BlockSpec((QA, D_MODEL), lambda b, i: (b * nq + i, 0)),
        out_shape=jax.ShapeDtypeStruct((t, D_MODEL), BF16),
        scratch_shapes=[pltpu.VMEM((ATT_KV_HEADS, rows, 1), F32),
                        pltpu.VMEM((ATT_KV_HEADS, rows, 1), F32),
                        pltpu.VMEM((ATT_KV_HEADS, rows, LANES), F32),
                        pltpu.VMEM((ATT_KV_HEADS, rows, 2 * LANES), F32)],
        compiler_params=_cparams(("arbitrary", "arbitrary")),
        name="attention",
    )(q, k, v, bias, logit_bound)


def _merge_kernel(ga_ref, gb_ref, og_ref, oa_ref, x_ref, w_ref, o_ref):
    merged = (jax.nn.sigmoid(ga_ref[...].astype(F32)) * og_ref[...].astype(F32)
              + jax.nn.sigmoid(gb_ref[...].astype(F32)) * oa_ref[...].astype(F32))
    o_ref[...] = x_ref[...] + jnp.dot(merged.astype(BF16), w_ref[...], preferred_element_type=F32)


def _merge(proj, o_gdn, o_att, x2, w_out):
    t = x2.shape[0]
    tm = TM_MERGE
    blk = lambda c: pl.BlockSpec((tm, D_MODEL), lambda i: (i, c))
    return pl.pallas_call(
        _merge_kernel,
        grid=(t // tm,),
        in_specs=[blk(COL_GA * LANES // D_MODEL), blk(COL_GB * LANES // D_MODEL), blk(0), blk(0), blk(0),
                  pl.BlockSpec((D_MODEL, D_MODEL), lambda i: (0, 0))],
        out_specs=blk(0),
        out_shape=jax.ShapeDtypeStruct((t, D_MODEL), F32),
        compiler_params=_cparams(("arbitrary",)),
        name="merge",
    )(proj, proj, o_gdn, o_att, x2, w_out)


def _mlp_kernel(x_ref, g_ref, wu_ref, wd_ref, o_ref, h_ref, acc_ref):
    f = pl.program_id(1)

    @pl.when(f == 0)
    def _():
        x = x_ref[...]
        ms = jnp.mean(x * x, axis=-1, keepdims=True)
        h_ref[...] = (x * lax.rsqrt(ms + EPS) * g_ref[...]).astype(BF16)
        acc_ref[...] = x

    up = jnp.dot(h_ref[...], wu_ref[...], preferred_element_type=F32)
    act = jnp.square(jnp.maximum(up, 0.0))
    acc_ref[...] += jnp.dot(act.astype(BF16), wd_ref[...], preferred_element_type=F32)

    @pl.when(f == pl.num_programs(1) - 1)
    def _():
        o_ref[...] = acc_ref[...]


def _mlp(x1, g_row, w_up, w_down):
    t = x1.shape[0]
    tm = min(TM_MLP, t)
    return pl.pallas_call(
        _mlp_kernel,
        grid=(t // tm, D_FF // TF_MLP),
        in_specs=[pl.BlockSpec((tm, D_MODEL), lambda i, f: (i, 0)),
                  pl.BlockSpec((1, D_MODEL), lambda i, f: (0, 0)),
                  pl.BlockSpec((D_MODEL, TF_MLP), lambda i, f: (0, f)),
                  pl.BlockSpec((TF_MLP, D_MODEL), lambda i, f: (f, 0))],
        out_specs=pl.BlockSpec((tm, D_MODEL), lambda i, f: (i, 0)),
        out_shape=jax.ShapeDtypeStruct((t, D_MODEL), F32),
        scratch_shapes=[pltpu.VMEM((tm, D_MODEL), BF16), pltpu.VMEM((tm, D_MODEL), F32)],
        compiler_params=_cparams(("arbitrary", "arbitrary")),
        name="mlp",
    )(x1, g_row, w_up, w_down)


def _pack_w_in(w_in):
    pts = np.cumsum(np.array(IN_SPLITS))[:-1].tolist()
    (g_qkv, g_z, g_a, g_b, a_q, a_k, a_v, i_q, i_k, i_w, gate_a, gate_b) = jnp.split(w_in, pts, axis=-1)
    small = jnp.concatenate([g_a, g_b, i_k, i_w], axis=-1)
    small = jnp.pad(small, ((0, 0), (0, LANES - small.shape[-1])))
    main = jnp.concatenate([g_qkv, g_z, a_q, a_k, a_v, i_q, gate_a, gate_b], axis=-1)
    return main.astype(BF16), small.astype(BF16)


def _lane_row(vec, offset=0):
    return jnp.zeros((1, LANES), F32).at[0, offset:offset + vec.shape[0]].set(vec.astype(F32))


def _rope_table(seq):
    tab = np.zeros((seq, LANES), np.float64)
    pos = np.arange(seq, dtype=np.float64)[:, None]
    for rot_dim, base in ((ATT_HEAD // ROPE_FRACTION, 0), (IDX_DIM // ROPE_FRACTION, TAB_COS_I)):
        half = rot_dim // 2
        ang = pos * ROPE_THETA ** (-np.arange(0, rot_dim, 2, dtype=np.float64) / rot_dim)[None, :]
        tab[:, base:base + half] = tab[:, base + half:base + 2 * half] = np.cos(ang)
        tab[:, base + 2 * half:base + 3 * half] = -np.sin(ang)
        tab[:, base + 3 * half:base + 4 * half] = np.sin(ang)
    return jnp.asarray(tab, dtype=F32)


def _layer(x2, batch, seq, norm_mix_g, w_in, conv_w, a_log, dt_bias, gdn_norm_g, q_norm_g, k_norm_g,
           w_out, norm_mlp_g, w_mlp_up, w_mlp_down):
    k_sel = min(TOPK_MAX, seq // 4)
    w_main, w_small = _pack_w_in(w_in)
    g_mix = norm_mix_g.reshape(1, D_MODEL)
    proj = _inproj(x2, g_mix, w_main, TN_IN, BF16, "inproj")
    small = _inproj(x2, g_mix, w_small, LANES, F32, "inproj_small")

    o_gdn = _gdn(proj, small, conv_w, _lane_row(a_log, SM_GA), _lane_row(dt_bias, SM_GA),
                 gdn_norm_g.reshape(1, LANES), batch, seq)

    q_att, k_att, v_att, q_idx, k_idx_t, w_idx = _prep(
        proj, small, _rope_table(seq), q_norm_g.reshape(1, LANES), k_norm_g.reshape(1, LANES), batch, seq)

    ii = np.arange(KT)
    ustrict = jnp.asarray((ii[:, None] < ii[None, :]).astype(np.float32), dtype=BF16)
    bias = _select(q_idx, w_idx, k_idx_t, ustrict, batch, seq, k_sel)
    logit_bound = (1.01 * ATT_HEAD * ATT_HEAD ** -0.5) * jnp.max(jnp.abs(q_norm_g)) * jnp.max(jnp.abs(k_norm_g))
    o_att = _attention(q_att, k_att, v_att, bias, logit_bound.reshape(1).astype(F32), batch, seq)

    x1 = _merge(proj, o_gdn, o_att, x2, w_out.astype(BF16))
    return _mlp(x1, norm_mlp_g.reshape(1, D_MODEL), w_mlp_up.astype(BF16), w_mlp_down.astype(BF16))


def kernel(x, norm_mix_g, w_in, conv_w, a_log, dt_bias, gdn_norm_g, q_norm_g, k_norm_g, w_out,
           norm_mlp_g, w_mlp_up, w_mlp_down):
    batch, seq, _ = x.shape
    x2 = x.reshape(batch * seq, D_MODEL)
    for layer in range(norm_mix_g.shape[0]):
        x2 = _layer(x2, batch, seq, norm_mix_g[layer], w_in[layer], conv_w[layer], a_log[layer],
                    dt_bias[layer], gdn_norm_g[layer], q_norm_g[layer], k_norm_g[layer], w_out[layer],
                    norm_mlp_g[layer], w_mlp_up[layer], w_mlp_down[layer])
    return x2.reshape(batch, seq, D_MODEL)
```

```python
import functools
import math

import numpy as np
import jax
import jax.numpy as jnp
from jax import lax
from jax.experimental import pallas as pl
from jax.experimental.pallas import tpu as pltpu

F32 = jnp.float32
BF16 = jnp.bfloat16
I32 = jnp.int32

D_MODEL = 1024
GDN_HEADS = 8
GDN_HEAD = 128
GDN_CONV = 4
GDN_CHUNK = 64
GDN_CONV_DIM = 3 * GDN_HEADS * GDN_HEAD
ATT_HEADS = 8
ATT_HEAD = 128
ATT_KV_HEADS = 2
ATT_GROUP = ATT_HEADS // ATT_KV_HEADS
IDX_HEADS = 8
IDX_DIM = 64
TOPK_MAX = 256
ROPE_THETA = 500000.0
ROPE_FRACTION = 4
D_FF = 4 * D_MODEL
EPS = 1e-6

IN_SPLITS = (GDN_CONV_DIM, D_MODEL, GDN_HEADS, GDN_HEADS, D_MODEL, 256, 256,
             IDX_HEADS * IDX_DIM, IDX_DIM, IDX_HEADS, D_MODEL, D_MODEL)

LANES = 128
assert 2 * GDN_CHUNK == LANES
COL_QKV = 0
COL_Z = 24
COL_AQ = 32
COL_AKV = 40
COL_IQ = 44
COL_GA = 48
COL_GB = 56
SM_GA = 0
SM_GB = 8
SM_IK = 16
SM_IW = 80
TAB_SIN_A = 2 * (ATT_HEAD // ROPE_FRACTION // 2)
TAB_COS_I = 4 * (ATT_HEAD // ROPE_FRACTION // 2)
TAB_SIN_I = TAB_COS_I + 2 * (IDX_DIM // ROPE_FRACTION // 2)

NEG = -1e30
FLT_MAX = 3.4028234663852886e38
MAX_SEARCH_STEPS = 254 + 24 + 8
LOG2E = 1.4426950408889634
SAFE_LOGIT_BOUND = 60.0

TM_IN = 512
TN_IN = 16 * LANES
GDN_BLK = 512
GDN_CHUNKS_PER_STEP = 8
TM_PREP = 512
QB = 128
QA = 256
KT = 512
TM_MERGE = 512
TM_MLP = 1024
TF_MLP = 512
VMEM_LIMIT = 56 * 2 ** 20


def _cparams(sem):
    return pltpu.CompilerParams(dimension_semantics=sem, vmem_limit_bytes=VMEM_LIMIT)


def _inproj_kernel(x_ref, g_ref, w_ref, o_ref):
    x = x_ref[...]
    ms = jnp.mean(x * x, axis=-1, keepdims=True)
    h = x * lax.rsqrt(ms + EPS) * g_ref[...]
    o_ref[...] = jnp.dot(h.astype(BF16), w_ref[...], preferred_element_type=F32).astype(o_ref.dtype)


def _inproj(x2, g_row, w_p, tn, out_dtype, name):
    t = x2.shape[0]
    n_cols = w_p.shape[1]
    return pl.pallas_call(
        _inproj_kernel,
        grid=(n_cols // tn, t // TM_IN),
        in_specs=[pl.BlockSpec((TM_IN, D_MODEL), lambda n, m: (m, 0)),
                  pl.BlockSpec((1, D_MODEL), lambda n, m: (0, 0)),
                  pl.BlockSpec((D_MODEL, tn), lambda n, m: (0, n))],
        out_specs=pl.BlockSpec((TM_IN, tn), lambda n, m: (m, n)),
        out_shape=jax.ShapeDtypeStruct((t, n_cols), out_dtype),
        compiler_params=_cparams(("arbitrary", "arbitrary")),
        name=name,
    )(x2, g_row, w_p)


def _gdn_kernel(qkv_ref, z_ref, sm_ref, cw_ref, alog_ref, dtb_ref, gn_ref, o_ref,
                halo_ref, state_ref, qn_ref, kn_ref, vv_ref, gc_ref, bt_ref):
    blk = qkv_ref.shape[0]
    c = GDN_CHUNK

    @pl.when(pl.program_id(1) == 0)
    def _():
        halo_ref[...] = jnp.zeros_like(halo_ref)
        state_ref[...] = jnp.zeros_like(state_ref)

    row8 = lax.broadcasted_iota(I32, (8, LANES), 0)
    for j in range(GDN_CONV_DIM // LANES):
        lo = j * LANES
        u = qkv_ref[:, lo:lo + LANES].astype(F32)
        hal = halo_ref[:, lo:lo + LANES]
        w = cw_ref[:, lo:lo + LANES]
        y = u * w[GDN_CONV - 1:GDN_CONV, :]
        for sft in range(1, GDN_CONV):
            ur = pltpu.roll(u, sft, axis=0)
            hr = pltpu.roll(hal, sft, axis=0)
            top = jnp.where(row8 < sft, hr, ur[0:8, :])
            ush = jnp.concatenate([top, ur[8:, :]], axis=0)
            y = y + ush * w[GDN_CONV - 1 - sft:GDN_CONV - sft, :]
        halo_ref[:, lo:lo + LANES] = u[blk - 8:, :]
        a = y * jax.nn.sigmoid(y)
        head = j % GDN_HEADS
        if j < 2 * GDN_HEADS:
            a = a * lax.rsqrt(jnp.sum(a * a, axis=-1, keepdims=True) + EPS)
        if j < GDN_HEADS:
            qn_ref[:, head * LANES:(head + 1) * LANES] = a * (GDN_HEAD ** -0.5)
        elif j < 2 * GDN_HEADS:
            kn_ref[:, head * LANES:(head + 1) * LANES] = a
        else:
            vv_ref[:, head * LANES:(head + 1) * LANES] = a

    sm = sm_ref[...]
    xg = sm + dtb_ref[...]
    softplus = jnp.maximum(xg, 0.0) + jnp.log1p(jnp.exp(-jnp.abs(xg)))
    gc = -jnp.exp(alog_ref[...]) * softplus
    row_in_chunk = lax.broadcasted_iota(I32, (blk, LANES), 0) & (c - 1)
    d = 1
    while d < c:
        gc = gc + jnp.where(row_in_chunk >= d, pltpu.roll(gc, d, axis=0), 0.0)
        d *= 2
    gc_ref[...] = gc
    bt_ref[...] = jax.nn.sigmoid(sm)

    ii = lax.broadcasted_iota(I32, (c, 2 * c), 0)
    lane_cc = lax.broadcasted_iota(I32, (c, 2 * c), 1)
    jj = lane_cc & (c - 1)
    incl = ii >= jj
    strict = ii > jj
    left = lane_cc < c
    left_row = lax.broadcasted_iota(I32, (1, 2 * c), 1) < c
    gn = gn_ref[...]

    def chunk_body(ci, carry):
        nch = GDN_CHUNKS_PER_STEP
        r0 = [pl.multiple_of((ci * nch + s) * c, c) for s in range(nch)]
        items = [(s, h) for s in range(nch) for h in range(GDN_HEADS)]
        n_items = len(items)
        pairs = range(n_items // 2)
        hs = [slice(h * LANES, (h + 1) * LANES) for h in range(GDN_HEADS)]
        half = lambda i: slice((i % 2) * LANES, (i % 2 + 1) * LANES)
        mm = lambda a, b: jnp.dot(a.astype(BF16), b.astype(BF16), preferred_element_type=F32)

        def bdiag(xa, xb):
            return jnp.concatenate([jnp.concatenate([xa, jnp.zeros_like(xb)], axis=1),
                                    jnp.concatenate([jnp.zeros_like(xa), xb], axis=1)], axis=0)

        def bdiag2(x2):
            return jnp.concatenate([jnp.where(left, x2, 0.0), jnp.where(left, 0.0, x2)], axis=0)

        def side(xa, xb, axis):
            return jnp.concatenate([xa, xb], axis=axis)

        gcc = [gc_ref[pl.ds(r0[s], c), :] for s in range(nch)]
        btc = [bt_ref[pl.ds(r0[s], c), :] for s in range(nch)]
        gct2 = [jnp.concatenate([g, g], axis=0).T for g in gcc]
        gcol = [gcc[s][:, SM_GA + h:SM_GA + h + 1] for s, h in items]
        glast = [gcc[s][c - 1:c, SM_GA + h:SM_GA + h + 1] for s, h in items]
        bcol = [btc[s][:, SM_GB + h:SM_GB + h + 1] for s, h in items]
        q = [qn_ref[pl.ds(r0[s], c), hs[h]] for s, h in items]
        k = [kn_ref[pl.ds(r0[s], c), hs[h]] for s, h in items]
        v = [vv_ref[pl.ds(r0[s], c), hs[h]] for s, h in items]
        kb = [k[i] * bcol[i] for i in range(n_items)]
        eg = [jnp.exp(gcol[i]) for i in range(n_items)]
        rhs = [jnp.concatenate([v[i] * bcol[i], kb[i] * eg[i]], axis=1) for i in range(n_items)]
        ab = [lax.dot_general(
            side(side(kb[2 * p], q[2 * p], 0), side(kb[2 * p + 1], q[2 * p + 1], 0), 1).astype(BF16),
            bdiag(k[2 * p], k[2 * p + 1]).astype(BF16),
            (((1,), (1,)), ((), ())), preferred_element_type=F32) for p in pairs]
        decay = []
        for p in pairs:
            s, h = items[2 * p]
            ha, hb = SM_GA + h, SM_GA + h + 1
            diff = (jnp.where(left, gcol[2 * p], gcol[2 * p + 1])
                    - jnp.where(left_row, gct2[s][ha:ha + 1, :], gct2[s][hb:hb + 1, :]))
            decay.append(jnp.where(incl, jnp.exp(jnp.where(incl, diff, 0.0)), 0.0))
        qk = [jnp.where(incl, ab[p][c:, :] * decay[p], 0.0) for p in pairs]
        pw = [-jnp.where(strict, ab[p][0:c, :] * decay[p], 0.0) for p in pairs]
        r = list(pw)
        for _ in range(int(math.log2(c)) - 1):
            pw = [mm(pw[p], bdiag2(pw[p])) for p in pairs]
            rp = [mm(r[p], bdiag2(pw[p])) for p in pairs]
            r = [r[p] + pw[p] + rp[p] for p in pairs]
        rr = [mm(r[p], bdiag(rhs[2 * p], rhs[2 * p + 1])) for p in pairs]
        sol = [rhs[i] + rr[i // 2][:, (i % 2) * 2 * LANES:(i % 2 + 1) * 2 * LANES] for i in range(n_items)]
        q_dec = [q[i] * eg[i] for i in range(n_items)]
        k_dec_t = [(k[i] * jnp.exp(glast[i] - gcol[i])).T for i in range(n_items)]
        st = [state_ref[h] for h in range(GDN_HEADS)]
        for s in range(nch):
            base = s * GDN_HEADS
            hp = range(GDN_HEADS // 2)
            ws = [mm(side(side(sol[base + 2 * p][:, LANES:], q_dec[base + 2 * p], 0),
                          side(sol[base + 2 * p + 1][:, LANES:], q_dec[base + 2 * p + 1], 0), 1),
                     bdiag(st[2 * p], st[2 * p + 1])) for p in hp]
            v_new = [sol[base + h][:, :LANES] - ws[h // 2][0:c, half(h)] for h in range(GDN_HEADS)]
            qv = [mm(qk[base // 2 + p], bdiag(v_new[2 * p], v_new[2 * p + 1])) for p in hp]
            o_n = [ws[h // 2][c:, half(h)] + qv[h // 2][:, half(h)] for h in range(GDN_HEADS)]
            st = [st[h] * jnp.exp(glast[base + h]) + mm(k_dec_t[base + h], v_new[h]) for h in range(GDN_HEADS)]
            for h in range(GDN_HEADS):
                on = o_n[h] * lax.rsqrt(jnp.mean(o_n[h] * o_n[h], axis=-1, keepdims=True) + EPS) * gn
                zz = z_ref[pl.ds(r0[s], c), hs[h]].astype(F32)
                o_ref[pl.ds(r0[s], c), hs[h]] = (on * (zz * jax.nn.sigmoid(zz))).astype(o_ref.dtype)
        for h in range(GDN_HEADS):
            state_ref[h] = st[h]
        return carry

    lax.fori_loop(0, blk // (c * GDN_CHUNKS_PER_STEP), chunk_body, 0)


def _gdn(proj, small, conv_w, alog_row, dtb_row, gn_row, batch, seq):
    t = proj.shape[0]
    blk = min(GDN_BLK, seq)
    ns = seq // blk
    row = lambda b, s: b * ns + s
    return pl.pallas_call(
        _gdn_kernel,
        grid=(batch, ns),
        in_specs=[pl.BlockSpec((blk, GDN_CONV_DIM), lambda b, s: (row(b, s), COL_QKV * LANES // GDN_CONV_DIM)),
                  pl.BlockSpec((blk, D_MODEL), lambda b, s: (row(b, s), COL_Z * LANES // D_MODEL)),
                  pl.BlockSpec((blk, LANES), lambda b, s: (row(b, s), 0)),
                  pl.BlockSpec((GDN_CONV, GDN_CONV_DIM), lambda b, s: (0, 0)),
                  pl.BlockSpec((1, LANES), lambda b, s: (0, 0)),
                  pl.BlockSpec((1, LANES), lambda b, s: (0, 0)),
                  pl.BlockSpec((1, LANES), lambda b, s: (0, 0))],
        out_specs=pl.BlockSpec((blk, D_MODEL), lambda b, s: (row(b, s), 0)),
        out_shape=jax.ShapeDtypeStruct((t, D_MODEL), BF16),
        scratch_shapes=[pltpu.VMEM((8, GDN_CONV_DIM), F32),
                        pltpu.VMEM((GDN_HEADS, GDN_HEAD, GDN_HEAD), F32),
                        pltpu.VMEM((blk, D_MODEL), F32),
                        pltpu.VMEM((blk, D_MODEL), F32),
                        pltpu.VMEM((blk, D_MODEL), F32),
                        pltpu.VMEM((blk, LANES), F32),
                        pltpu.VMEM((blk, LANES), F32)],
        compiler_params=_cparams(("arbitrary", "arbitrary")),
        name="gdn",
    )(proj, proj, small, conv_w, alog_row, dtb_row, gn_row)


def _rope(y, cos, sin_lo, sin_hi, half):
    return y * cos + pltpu.roll(y, LANES - half, axis=1) * sin_lo + pltpu.roll(y, half, axis=1) * sin_hi


def _prep_kernel(aq_ref, akv_ref, iq_ref, sm_ref, tab_ref, gq_ref, gk_ref,
                 q_out, k_out, v_out, qi_out, kit_out, w_out):
    half_a = ATT_HEAD // ROPE_FRACTION // 2
    half_i = IDX_DIM // ROPE_FRACTION // 2
    t = tab_ref[...]
    lane = lax.broadcasted_iota(I32, t.shape, 1)
    from_lane = lambda src: pltpu.roll(t, (-src) % LANES, axis=1)
    between = lambda lo, hi: (lane >= lo) & (lane < hi)
    ca = jnp.where(lane < 2 * half_a, t, 1.0)
    sa = from_lane(TAB_SIN_A)
    sa1 = jnp.where(lane < half_a, sa, 0.0)
    sa2 = jnp.where(between(half_a, 2 * half_a), sa, 0.0)
    lm = lane & (IDX_DIM - 1)
    low = lane < IDX_DIM
    ci = jnp.where(lm < 2 * half_i, jnp.where(low, from_lane(TAB_COS_I), from_lane(TAB_COS_I - IDX_DIM)), 1.0)
    si = jnp.where(low, from_lane(TAB_SIN_I), from_lane(TAB_SIN_I - IDX_DIM))
    si1 = jnp.where(lm < half_i, si, 0.0)
    si2 = jnp.where((lm >= half_i) & (lm < 2 * half_i), si, 0.0)
    ck = jnp.where(between(SM_IK, SM_IK + 2 * half_i), from_lane(TAB_COS_I - SM_IK), 1.0)
    sk = from_lane(TAB_SIN_I - SM_IK)
    sk1 = jnp.where(between(SM_IK, SM_IK + half_i), sk, 0.0)
    sk2 = jnp.where(between(SM_IK + half_i, SM_IK + 2 * half_i), sk, 0.0)
    gq = gq_ref[...]
    gk = gk_ref[...]
    for h in range(ATT_HEADS):
        hs = slice(h * LANES, (h + 1) * LANES)
        xh = aq_ref[:, hs].astype(F32)
        y = xh * lax.rsqrt(jnp.mean(xh * xh, axis=-1, keepdims=True) + EPS) * gq
        q_out[:, hs] = _rope(y, ca, sa1, sa2, half_a).astype(BF16)
    for cc in range(ATT_KV_HEADS):
        hs = slice(cc * LANES, (cc + 1) * LANES)
        xk = akv_ref[:, hs].astype(F32)
        y = xk * lax.rsqrt(jnp.mean(xk * xk, axis=-1, keepdims=True) + EPS) * gk
        k_out[:, hs] = _rope(y, ca, sa1, sa2, half_a).astype(BF16)
        v_out[:, hs] = akv_ref[:, ATT_KV_HEADS * LANES + cc * LANES:ATT_KV_HEADS * LANES + (cc + 1) * LANES].astype(BF16)
    for j in range(IDX_HEADS * IDX_DIM // LANES):
        hs = slice(j * LANES, (j + 1) * LANES)
        qi_out[:, hs] = _rope(iq_ref[:, hs].astype(F32), ci, si1, si2, half_i).astype(BF16)
    sm = sm_ref[...]
    smr = _rope(sm, ck, sk1, sk2, half_i)
    kit_out[...] = smr.T[SM_IK:SM_IK + IDX_DIM, :].astype(BF16)
    w_out[...] = sm * (IDX_HEADS ** -0.5 * IDX_DIM ** -0.5)


def _prep(proj, small, table, gq_row, gk_row, batch, seq):
    t = proj.shape[0]
    tm = min(TM_PREP, seq)
    nsb = seq // tm
    tab_spec = pl.BlockSpec((tm, LANES), lambda i: (i % nsb, 0))
    row_spec = pl.BlockSpec((1, LANES), lambda i: (0, 0))
    return pl.pallas_call(
        _prep_kernel,
        grid=(t // tm,),
        in_specs=[pl.BlockSpec((tm, D_MODEL), lambda i: (i, COL_AQ * LANES // D_MODEL)),
                  pl.BlockSpec((tm, 4 * LANES), lambda i: (i, COL_AKV // 4)),
                  pl.BlockSpec((tm, 4 * LANES), lambda i: (i, COL_IQ // 4)),
                  pl.BlockSpec((tm, LANES), lambda i: (i, 0))]
                 + [tab_spec, row_spec, row_spec],
        out_specs=[pl.BlockSpec((tm, D_MODEL), lambda i: (i, 0)),
                   pl.BlockSpec((tm, ATT_KV_HEADS * LANES), lambda i: (i, 0)),
                   pl.BlockSpec((tm, ATT_KV_HEADS * LANES), lambda i: (i, 0)),
                   pl.BlockSpec((tm, IDX_HEADS * IDX_DIM), lambda i: (i, 0)),
                   pl.BlockSpec((None, None, IDX_DIM, tm), lambda i: (i // nsb, i % nsb, 0, 0)),
                   pl.BlockSpec((tm, LANES), lambda i: (i, 0))],
        out_shape=[jax.ShapeDtypeStruct((t, D_MODEL), BF16),
                   jax.ShapeDtypeStruct((t, ATT_KV_HEADS * LANES), BF16),
                   jax.ShapeDtypeStruct((t, ATT_KV_HEADS * LANES), BF16),
                   jax.ShapeDtypeStruct((t, IDX_HEADS * IDX_DIM), BF16),
                   jax.ShapeDtypeStruct((batch, nsb, IDX_DIM, tm), BF16),
                   jax.ShapeDtypeStruct((t, LANES), F32)],
        compiler_params=_cparams(("arbitrary",)),
        name="prep",
    )(proj, proj, proj, small, table, gq_row, gk_row)


def _select_kernel(qi_ref, w_ref, kit_ref, ustrict_ref, bias_ref, keys_ref, qh_ref, *, k_sel):
    qblk = pl.program_id(1)
    nt_all = keys_ref.shape[0]
    n_groups = KT // LANES
    nt = (qblk * QB + QB + KT - 1) // KT

    for h in range(IDX_HEADS):
        qh_ref[h * QB:(h + 1) * QB, :] = qi_ref[:, h * IDX_DIM:(h + 1) * IDX_DIM]
    wb = [jnp.broadcast_to(w_ref[:, SM_IW + h:SM_IW + h + 1], (QB, LANES)) for h in range(IDX_HEADS)]
    row_g = qblk * QB + lax.broadcasted_iota(I32, (QB, LANES), 0)
    lane = lax.broadcasted_iota(I32, (QB, LANES), 1)

    def score_tile(j, carry):
        mx, mn, ge0, gt0 = carry
        lg = jnp.dot(qh_ref[...], kit_ref[j], preferred_element_type=F32)
        acc = [jnp.zeros((QB, LANES), F32) for _ in range(n_groups)]
        for h in range(IDX_HEADS):
            for g in range(n_groups):
                acc[g] = acc[g] + jnp.maximum(lg[h * QB:(h + 1) * QB, g * LANES:(g + 1) * LANES], 0.0) * wb[h]
        for g in range(n_groups):
            causal = (j * KT + g * LANES + lane) <= row_g
            sc = jnp.where(causal, acc[g], -jnp.inf)
            keys_ref[j, :, g * LANES:(g + 1) * LANES] = sc
            mx = jnp.maximum(mx, sc)
            mn = jnp.minimum(mn, jnp.where(causal, acc[g], jnp.inf))
            ge0 = ge0 + jnp.where(sc >= 0.0, 1.0, 0.0)
            gt0 = gt0 + jnp.where(sc > 0.0, 1.0, 0.0)
        return mx, mn, ge0, gt0

    zeros = jnp.zeros((QB, LANES), F32)
    mx, mn, ge0, gt0 = lax.fori_loop(0, nt, score_tile, (jnp.full((QB, LANES), -jnp.inf, F32),
                                                         jnp.full((QB, LANES), jnp.inf, F32), zeros, zeros))
    row_max = jnp.max(mx, axis=1, keepdims=True)
    row_min = jnp.min(mn, axis=1, keepdims=True)
    n_ge0 = jnp.sum(ge0, axis=1, keepdims=True)
    n_gt0 = jnp.sum(gt0, axis=1, keepdims=True)

    n_causal = (row_g[:, 0:1] + 1).astype(F32)
    k_eff = jnp.minimum(float(k_sel), n_causal)

    def count(cand, strict):
        cb = jnp.broadcast_to(cand, (QB, LANES))

        def tile(j, cnt):
            kk = keys_ref[j]
            for g in range(n_groups):
                kg = kk[:, g * LANES:(g + 1) * LANES]
                cnt = cnt + jnp.where(kg > cb if strict else kg >= cb, 1.0, 0.0)
            return cnt

        cnt = lax.fori_loop(0, nt, tile, jnp.zeros((QB, LANES), F32))
        return jnp.sum(cnt, axis=1, keepdims=True)

    to_key = lambda x: (lambda b: b ^ ((b >> 31) & 0x7FFFFFFF))(pltpu.bitcast(x, I32))
    to_val = lambda kx: pltpu.bitcast(kx ^ ((kx >> 31) & 0x7FFFFFFF), F32)

    def search_cond(st):
        return jnp.logical_and(st[0] < MAX_SEARCH_STEPS, st[1] > 0.0)

    def unsettled(lo, hi):
        mid = 0.5 * lo + 0.5 * hi
        settled = jnp.logical_or(lo == hi, jnp.logical_or(mid == lo, mid == hi))
        return jnp.max(jnp.where(settled, 0.0, 1.0))

    def search_body(st):
        it, _, lo, hi = st
        mid = 0.5 * lo + 0.5 * hi
        n = count(mid, False)
        lo = jnp.where(n >= k_eff, mid, lo)
        hi = jnp.where(n <= k_eff, mid, hi)
        return it + 1, unsettled(lo, hi), lo, hi

    take_all = n_causal <= k_eff
    above = n_gt0 > k_eff
    below = n_ge0 < k_eff
    lo0 = jnp.where(below, to_val(to_key(row_min) - 1), 0.0)
    hi0 = jnp.where(above, to_val(to_key(row_max) + 1), 0.0)
    lo0 = jnp.where(take_all, -FLT_MAX, lo0)
    hi0 = jnp.where(take_all, -FLT_MAX, hi0)
    st = lax.while_loop(search_cond, search_body, (jnp.int32(0), unsettled(lo0, hi0), lo0, hi0))
    vstar, hi_end = st[2], st[3]
    vsb = jnp.broadcast_to(vstar, (QB, KT))
    at_zero = jnp.logical_not(jnp.logical_or(jnp.logical_or(above, below), take_all))
    tied = jnp.logical_or(vstar != hi_end, jnp.logical_and(at_zero, n_ge0 > k_eff))
    any_tied = jnp.max(jnp.where(tied, 1.0, 0.0)) > 0.0

    @pl.when(any_tied)
    def _():
        need = k_eff - count(vstar, True)
        ustrict = ustrict_ref[...]

        def final_tile(j, carry):
            kk = keys_ref[j]
            tie = kk == vsb
            tie_b = jnp.where(tie, 1.0, 0.0).astype(BF16)
            rank = jnp.dot(tie_b, ustrict, preferred_element_type=F32) + carry
            keep_tie = jnp.where(rank < need, 0.0, NEG)
            bias = jnp.where(kk > vsb, 0.0, jnp.where(tie, keep_tie, NEG))
            bias_ref[j] = bias.astype(BF16)
            return carry + jnp.sum(jnp.where(tie, 1.0, 0.0), axis=1, keepdims=True)

        lax.fori_loop(0, nt, final_tile, jnp.zeros((QB, 1), F32))

    @pl.when(jnp.logical_not(any_tied))
    def _():
        def final_tile(j, carry):
            bias_ref[j] = jnp.where(keys_ref[j] >= vsb, 0.0, NEG).astype(BF16)
            return carry

        lax.fori_loop(0, nt, final_tile, 0)

    def fill_tile(j, carry):
        bias_ref[j] = jnp.full((QB, KT), NEG, BF16)
        return carry

    lax.fori_loop(nt, nt_all, fill_tile, 0)


def _select(qi, wsm, kit, ustrict, batch, seq, k_sel):
    nq = seq // QB
    nt = seq // KT
    return pl.pallas_call(
        functools.partial(_select_kernel, k_sel=k_sel),
        grid=(batch, nq),
        in_specs=[pl.BlockSpec((QB, IDX_HEADS * IDX_DIM), lambda b, q: (b * nq + q, 0)),
                  pl.BlockSpec((QB, LANES), lambda b, q: (b * nq + q, 0)),
                  pl.BlockSpec((None, nt, IDX_DIM, KT), lambda b, q: (b, 0, 0, 0)),
                  pl.BlockSpec((KT, KT), lambda b, q: (0, 0))],
        out_specs=pl.BlockSpec((None, None, nt, QB, KT), lambda b, q: (b, q, 0, 0, 0)),
        out_shape=jax.ShapeDtypeStruct((batch, nq, nt, QB, KT), BF16),
        scratch_shapes=[pltpu.VMEM((nt, QB, KT), F32),
                        pltpu.VMEM((IDX_HEADS * QB, IDX_DIM), BF16)],
        compiler_params=_cparams(("arbitrary", "arbitrary")),
        name="select",
    )(qi, wsm, kit, ustrict)


def _attn_kernel(q_ref, k_ref, v_ref, bias_ref, bound_ref, o_ref, m_ref, l_ref, acc_ref, accw_ref):
    qblk = pl.program_id(1)
    nt = (qblk * QA + QA + KT - 1) // KT
    scale = ATT_HEAD ** -0.5
    q4 = [jnp.concatenate([q_ref[:, (cc * ATT_GROUP + g) * LANES:(cc * ATT_GROUP + g + 1) * LANES]
                           for g in range(ATT_GROUP)], axis=0) for cc in range(ATT_KV_HEADS)]

    def bias_rows(j):
        b = jnp.concatenate([bias_ref[r, j] for r in range(QA // QB)], axis=0).astype(F32)
        return jnp.concatenate([b] * ATT_GROUP, axis=0)

    def write_out(cc, o):
        for g in range(ATT_GROUP):
            h = cc * ATT_GROUP + g
            o_ref[:, h * LANES:(h + 1) * LANES] = o[g * QA:(g + 1) * QA, :].astype(o_ref.dtype)

    safe = bound_ref[0] <= SAFE_LOGIT_BOUND

    @pl.when(safe)
    def _():
        accw_ref[...] = jnp.zeros_like(accw_ref)
        ones = jnp.ones((KT, LANES), BF16)

        def kv_tile(j, carry):
            r0 = pl.multiple_of(j * KT, KT)
            bias4 = bias_rows(j)
            for cc in range(ATT_KV_HEADS):
                kc = k_ref[pl.ds(r0, KT), cc * LANES:(cc + 1) * LANES]
                vc = v_ref[pl.ds(r0, KT), cc * LANES:(cc + 1) * LANES]
                s = lax.dot_general(q4[cc], kc, (((1,), (1,)), ((), ())), preferred_element_type=F32)
                p = jnp.exp2(s * (scale * LOG2E) + bias4)
                accw_ref[cc] += jnp.dot(p.astype(BF16), jnp.concatenate([vc, ones], axis=1),
                                        preferred_element_type=F32)
            return carry

        lax.fori_loop(0, nt, kv_tile, 0)
        for cc in range(ATT_KV_HEADS):
            aw = accw_ref[cc]
            write_out(cc, aw[:, :LANES] / aw[:, LANES:])

    @pl.when(jnp.logical_not(safe))
    def _():
        m_ref[...] = jnp.full_like(m_ref, NEG)
        l_ref[...] = jnp.zeros_like(l_ref)
        acc_ref[...] = jnp.zeros_like(acc_ref)

        def kv_tile(j, carry):
            r0 = pl.multiple_of(j * KT, KT)
            bias4 = bias_rows(j)
            for cc in range(ATT_KV_HEADS):
                kc = k_ref[pl.ds(r0, KT), cc * LANES:(cc + 1) * LANES]
                vc = v_ref[pl.ds(r0, KT), cc * LANES:(cc + 1) * LANES]
                s = lax.dot_general(q4[cc], kc, (((1,), (1,)), ((), ())), preferred_element_type=F32)
                s = s * scale + bias4
                m_prev = m_ref[cc]
                m_new = jnp.maximum(m_prev, jnp.max(s, axis=1, keepdims=True))
                alpha = jnp.exp(m_prev - m_new)
                p = jnp.exp(s - m_new)
                l_ref[cc] = alpha * l_ref[cc] + jnp.sum(p, axis=1, keepdims=True)
                acc_ref[cc] = alpha * acc_ref[cc] + jnp.dot(p.astype(BF16), vc, preferred_element_type=F32)
                m_ref[cc] = m_new
            return carry

        lax.fori_loop(0, nt, kv_tile, 0)
        for cc in range(ATT_KV_HEADS):
            write_out(cc, acc_ref[cc] / l_ref[cc])


def _attention(q, k, v, bias, logit_bound, batch, seq):
    t = q.shape[0]
    nq = seq // QA
    nt = seq // KT
    rows = ATT_GROUP * QA
    return pl.pallas_call(
        _attn_kernel,
        grid=(batch, nq),
        in_specs=[pl.BlockSpec((QA, D_MODEL), lambda b, i: (b * nq + i, 0)),
                  pl.BlockSpec((seq, ATT_KV_HEADS * LANES), lambda b, i: (b, 0)),
                  pl.BlockSpec((seq, ATT_KV_HEADS * LANES), lambda b, i: (b, 0)),
                  pl.BlockSpec((None, QA // QB, nt, QB, KT), lambda b, i: (b, i, 0, 0, 0)),
                  pl.BlockSpec(memory_space=pltpu.SMEM)],
        out_specs=pl.BlockSpec((QA, D_MODEL), lambda b, i: (b * nq + i, 0)),
        out_shape=jax.ShapeDtypeStruct((t, D_MODEL), BF16),
        scratch_shapes=[pltpu.VMEM((ATT_KV_HEADS, rows, 1), F32),
                        pltpu.VMEM((ATT_KV_HEADS, rows, 1), F32),
                        pltpu.VMEM((ATT_KV_HEADS, rows, LANES), F32),
                        pltpu.VMEM((ATT_KV_HEADS, rows, 2 * LANES), F32)],
        compiler_params=_cparams(("arbitrary", "arbitrary")),
        name="attention",
    )(q, k, v, bias, logit_bound)


def _merge_kernel(ga_ref, gb_ref, og_ref, oa_ref, x_ref, w_ref, o_ref):
    merged = (jax.nn.sigmoid(ga_ref[...].astype(F32)) * og_ref[...].astype(F32)
              + jax.nn.sigmoid(gb_ref[...].astype(F32)) * oa_ref[...].astype(F32))
    o_ref[...] = x_ref[...] + jnp.dot(merged.astype(BF16), w_ref[...], preferred_element_type=F32)


def _merge(proj, o_gdn, o_att, x2, w_out):
    t = x2.shape[0]
    tm = TM_MERGE
    blk = lambda c: pl.BlockSpec((tm, D_MODEL), lambda i: (i, c))
    return pl.pallas_call(
        _merge_kernel,
        grid=(t // tm,),
        in_specs=[blk(COL_GA * LANES // D_MODEL), blk(COL_GB * LANES // D_MODEL), blk(0), blk(0), blk(0),
                  pl.BlockSpec((D_MODEL, D_MODEL), lambda i: (0, 0))],
        out_specs=blk(0),
        out_shape=jax.ShapeDtypeStruct((t, D_MODEL), F32),
        compiler_params=_cparams(("arbitrary",)),
        name="merge",
    )(proj, proj, o_gdn, o_att, x2, w_out)


def _mlp_kernel(x_ref, g_ref, wu_ref, wd_ref, o_ref, h_ref, acc_ref):
    f = pl.program_id(1)

    @pl.when(f == 0)
    def _():
        x = x_ref[...]
        ms = jnp.mean(x * x, axis=-1, keepdims=True)
        h_ref[...] = (x * lax.rsqrt(ms + EPS) * g_ref[...]).astype(BF16)
        acc_ref[...] = x

    up = jnp.dot(h_ref[...], wu_ref[...], preferred_element_type=F32)
    act = jnp.square(jnp.maximum(up, 0.0))
    acc_ref[...] += jnp.dot(act.astype(BF16), wd_ref[...], preferred_element_type=F32)

    @pl.when(f == pl.num_programs(1) - 1)
    def _():
        o_ref[...] = acc_ref[...]


def _mlp(x1, g_row, w_up, w_down):
    t = x1.shape[0]
    tm = min(TM_MLP, t)
    return pl.pallas_call(
        _mlp_kernel,
        grid=(t // tm, D_FF // TF_MLP),
        in_specs=[pl.BlockSpec((tm, D_MODEL), lambda i, f: (i, 0)),
                  pl.BlockSpec((1, D_MODEL), lambda i, f: (0, 0)),
                  pl.BlockSpec((D_MODEL, TF_MLP), lambda i, f: (0, f)),
                  pl.BlockSpec((TF_MLP, D_MODEL), lambda i, f: (f, 0))],
        out_specs=pl.BlockSpec((tm, D_MODEL), lambda i, f: (i, 0)),
        out_shape=jax.ShapeDtypeStruct((t, D_MODEL), F32),
        scratch_shapes=[pltpu.VMEM((tm, D_MODEL), BF16), pltpu.VMEM((tm, D_MODEL), F32)],
        compiler_params=_cparams(("arbitrary", "arbitrary")),
        name="mlp",
    )(x1, g_row, w_up, w_down)


def _pack_w_in(w_in):
    pts = np.cumsum(np.array(IN_SPLITS))[:-1].tolist()
    (g_qkv, g_z, g_a, g_b, a_q, a_k, a_v, i_q, i_k, i_w, gate_a, gate_b) = jnp.split(w_in, pts, axis=-1)
    small = jnp.concatenate([g_a, g_b, i_k, i_w], axis=-1)
    small = jnp.pad(small, ((0, 0), (0, LANES - small.shape[-1])))
    main = jnp.concatenate([g_qkv, g_z, a_q, a_k, a_v, i_q, gate_a, gate_b], axis=-1)
    return main.astype(BF16), small.astype(BF16)


def _lane_row(vec, offset=0):
    return jnp.zeros((1, LANES), F32).at[0, offset:offset + vec.shape[0]].set(vec.astype(F32))


def _rope_table(seq):
    tab = np.zeros((seq, LANES), np.float64)
    pos = np.arange(seq, dtype=np.float64)[:, None]
    for rot_dim, base in ((ATT_HEAD // ROPE_FRACTION, 0), (IDX_DIM // ROPE_FRACTION, TAB_COS_I)):
        half = rot_dim // 2
        ang = pos * ROPE_THETA ** (-np.arange(0, rot_dim, 2, dtype=np.float64) / rot_dim)[None, :]
        tab[:, base:base + half] = tab[:, base + half:base + 2 * half] = np.cos(ang)
        tab[:, base + 2 * half:base + 3 * half] = -np.sin(ang)
        tab[:, base + 3 * half:base + 4 * half] = np.sin(ang)
    return jnp.asarray(tab, dtype=F32)


def _layer(x2, batch, seq, norm_mix_g, w_in, conv_w, a_log, dt_bias, gdn_norm_g, q_norm_g, k_norm_g,
           w_out, norm_mlp_g, w_mlp_up, w_mlp_down):
    k_sel = min(TOPK_MAX, seq // 4)
    w_main, w_small = _pack_w_in(w_in)
    g_mix = norm_mix_g.reshape(1, D_MODEL)
    proj = _inproj(x2, g_mix, w_main, TN_IN, BF16, "inproj")
    small = _inproj(x2, g_mix, w_small, LANES, F32, "inproj_small")

    o_gdn = _gdn(proj, small, conv_w, _lane_row(a_log, SM_GA), _lane_row(dt_bias, SM_GA),
                 gdn_norm_g.reshape(1, LANES), batch, seq)

    q_att, k_att, v_att, q_idx, k_idx_t, w_idx = _prep(
        proj, small, _rope_table(seq), q_norm_g.reshape(1, LANES), k_norm_g.reshape(1, LANES), batch, seq)

    ii = np.arange(KT)
    ustrict = jnp.asarray((ii[:, None] < ii[None, :]).astype(np.float32), dtype=BF16)
    bias = _select(q_idx, w_idx, k_idx_t, ustrict, batch, seq, k_sel)
    logit_bound = (1.01 * ATT_HEAD * ATT_HEAD ** -0.5) * jnp.max(jnp.abs(q_norm_g)) * jnp.max(jnp.abs(k_norm_g))
    o_att = _attention(q_att, k_att, v_att, bias, logit_bound.reshape(1).astype(F32), batch, seq)

    x1 = _merge(proj, o_gdn, o_att, x2, w_out.astype(BF16))
    return _mlp(x1, norm_mlp_g.reshape(1, D_MODEL), w_mlp_up.astype(BF16), w_mlp_down.astype(BF16))


def kernel(x, norm_mix_g, w_in, conv_w, a_log, dt_bias, gdn_norm_g, q_norm_g, k_norm_g, w_out,
           norm_mlp_g, w_mlp_up, w_mlp_down):
    batch, seq, _ = x.shape
    x2 = x.reshape(batch * seq, D_MODEL)
    for layer in range(norm_mix_g.shape[0]):
        x2 = _layer(x2, batch, seq, norm_mix_g[layer], w_in[layer], conv_w[layer], a_log[layer],
                    dt_bias[layer], gdn_norm_g[layer], q_norm_g[layer], k_norm_g[layer], w_out[layer],
                    norm_mlp_g[layer], w_mlp_up[layer], w_mlp_down[layer])
    return x2.reshape(batch, seq, D_MODEL)
```

```python
import functools
import math

import numpy as np
import jax
import jax.numpy as jnp
from jax import lax
from jax.experimental import pallas as pl
from jax.experimental.pallas import tpu as pltpu

F32 = jnp.float32
BF16 = jnp.bfloat16
I32 = jnp.int32

D_MODEL = 1024
GDN_HEADS = 8
GDN_HEAD = 128
GDN_CONV = 4
GDN_CHUNK = 64
GDN_CONV_DIM = 3 * GDN_HEADS * GDN_HEAD
ATT_HEADS = 8
ATT_HEAD = 128
ATT_KV_HEADS = 2
ATT_GROUP = ATT_HEADS // ATT_KV_HEADS
IDX_HEADS = 8
IDX_DIM = 64
TOPK_MAX = 256
ROPE_THETA = 500000.0
ROPE_FRACTION = 4
D_FF = 4 * D_MODEL
EPS = 1e-6

IN_SPLITS = (GDN_CONV_DIM, D_MODEL, GDN_HEADS, GDN_HEADS, D_MODEL, 256, 256,
             IDX_HEADS * IDX_DIM, IDX_DIM, IDX_HEADS, D_MODEL, D_MODEL)

LANES = 128
assert 2 * GDN_CHUNK == LANES
COL_QKV = 0
COL_Z = 24
COL_AQ = 32
COL_AKV = 40
COL_IQ = 44
COL_GA = 48
COL_GB = 56
SM_GA = 0
SM_GB = 8
SM_IK = 16
SM_IW = 80
TAB_SIN_A = 2 * (ATT_HEAD // ROPE_FRACTION // 2)
TAB_COS_I = 4 * (ATT_HEAD // ROPE_FRACTION // 2)
TAB_SIN_I = TAB_COS_I + 2 * (IDX_DIM // ROPE_FRACTION // 2)

NEG = -1e30
FLT_MAX = 3.4028234663852886e38
MAX_SEARCH_STEPS = 254 + 24 + 8
LOG2E = 1.4426950408889634
SAFE_LOGIT_BOUND = 60.0

TM_IN = 512
TN_IN = 16 * LANES
GDN_BLK = 512
GDN_CHUNKS_PER_STEP = 8
TM_PREP = 512
QB = 128
QA = 256
KT = 512
TM_MERGE = 512
TM_MLP = 1024
TF_MLP = 512
VMEM_LIMIT = 56 * 2 ** 20


def _cparams(sem):
    return pltpu.CompilerParams(dimension_semantics=sem, vmem_limit_bytes=VMEM_LIMIT)


def _loop_groups(n, body, init, group=2):
    shift = group.bit_length() - 1

    def many(i, carry):
        for s in range(group):
            carry = body(group * i + s, carry)
        return carry

    n_groups = lax.shift_right_logical(n, shift)
    carry = lax.fori_loop(0, n_groups, many, init)
    return lax.fori_loop(lax.shift_left(n_groups, shift), n, body, carry)


def _inproj_kernel(x_ref, g_ref, w_ref, o_ref):
    x = x_ref[...]
    ms = jnp.mean(x * x, axis=-1, keepdims=True)
    h = x * lax.rsqrt(ms + EPS) * g_ref[...]
    o_ref[...] = jnp.dot(h.astype(BF16), w_ref[...], preferred_element_type=F32).astype(o_ref.dtype)


def _inproj(x2, g_row, w_p, tn, out_dtype, name):
    t = x2.shape[0]
    n_cols = w_p.shape[1]
    return pl.pallas_call(
        _inproj_kernel,
        grid=(n_cols // tn, t // TM_IN),
        in_specs=[pl.BlockSpec((TM_IN, D_MODEL), lambda n, m: (m, 0)),
                  pl.BlockSpec((1, D_MODEL), lambda n, m: (0, 0)),
                  pl.BlockSpec((D_MODEL, tn), lambda n, m: (0, n))],
        out_specs=pl.BlockSpec((TM_IN, tn), lambda n, m: (m, n)),
        out_shape=jax.ShapeDtypeStruct((t, n_cols), out_dtype),
        compiler_params=_cparams(("arbitrary", "arbitrary")),
        name=name,
    )(x2, g_row, w_p)


def _gdn_kernel(qkv_ref, z_ref, sm_ref, cw_ref, alog_ref, dtb_ref, gn_ref, o_ref,
                halo_ref, state_ref, qn_ref, kn_ref, vv_ref, gc_ref, bt_ref):
    blk = qkv_ref.shape[0]
    c = GDN_CHUNK

    @pl.when(pl.program_id(1) == 0)
    def _():
        halo_ref[...] = jnp.zeros_like(halo_ref)
        state_ref[...] = jnp.zeros_like(state_ref)

    row8 = lax.broadcasted_iota(I32, (8, LANES), 0)
    for j in range(GDN_CONV_DIM // LANES):
        lo = j * LANES
        u = qkv_ref[:, lo:lo + LANES].astype(F32)
        hal = halo_ref[:, lo:lo + LANES]
        w = cw_ref[:, lo:lo + LANES]
        y = u * w[GDN_CONV - 1:GDN_CONV, :]
        for sft in range(1, GDN_CONV):
            ur = pltpu.roll(u, sft, axis=0)
            hr = pltpu.roll(hal, sft, axis=0)
            top = jnp.where(row8 < sft, hr, ur[0:8, :])
            ush = jnp.concatenate([top, ur[8:, :]], axis=0)
            y = y + ush * w[GDN_CONV - 1 - sft:GDN_CONV - sft, :]
        halo_ref[:, lo:lo + LANES] = u[blk - 8:, :]
        a = y * jax.nn.sigmoid(y)
        head = j % GDN_HEADS
        if j < 2 * GDN_HEADS:
            a = a * lax.rsqrt(jnp.sum(a * a, axis=-1, keepdims=True) + EPS)
        if j < GDN_HEADS:
            qn_ref[:, head * LANES:(head + 1) * LANES] = a * (GDN_HEAD ** -0.5)
        elif j < 2 * GDN_HEADS:
            kn_ref[:, head * LANES:(head + 1) * LANES] = a
        else:
            vv_ref[:, head * LANES:(head + 1) * LANES] = a

    sm = sm_ref[...]
    xg = sm + dtb_ref[...]
    softplus = jnp.maximum(xg, 0.0) + jnp.log1p(jnp.exp(-jnp.abs(xg)))
    gc = -jnp.exp(alog_ref[...]) * softplus
    row_in_chunk = lax.broadcasted_iota(I32, (blk, LANES), 0) & (c - 1)
    d = 1
    while d < c:
        gc = gc + jnp.where(row_in_chunk >= d, pltpu.roll(gc, d, axis=0), 0.0)
        d *= 2
    gc_ref[...] = gc
    bt_ref[...] = jax.nn.sigmoid(sm)

    ii = lax.broadcasted_iota(I32, (c, 2 * c), 0)
    lane_cc = lax.broadcasted_iota(I32, (c, 2 * c), 1)
    jj = lane_cc & (c - 1)
    incl = ii >= jj
    strict = ii > jj
    left = lane_cc < c
    left_row = lax.broadcasted_iota(I32, (1, 2 * c), 1) < c
    gn = gn_ref[...]

    def chunk_body(ci, carry):
        nch = GDN_CHUNKS_PER_STEP
        r0 = [pl.multiple_of((ci * nch + s) * c, c) for s in range(nch)]
        items = [(s, h) for s in range(nch) for h in range(GDN_HEADS)]
        n_items = len(items)
        pairs = range(n_items // 2)
        hs = [slice(h * LANES, (h + 1) * LANES) for h in range(GDN_HEADS)]
        half = lambda i: slice((i % 2) * LANES, (i % 2 + 1) * LANES)
        mm = lambda a, b: jnp.dot(a.astype(BF16), b.astype(BF16), preferred_element_type=F32)

        def bdiag(xa, xb):
            return jnp.concatenate([jnp.concatenate([xa, jnp.zeros_like(xb)], axis=1),
                                    jnp.concatenate([jnp.zeros_like(xa), xb], axis=1)], axis=0)

        def bdiag2(x2):
            return jnp.concatenate([jnp.where(left, x2, 0.0), jnp.where(left, 0.0, x2)], axis=0)

        def side(xa, xb, axis):
            return jnp.concatenate([xa, xb], axis=axis)

        gcc = [gc_ref[pl.ds(r0[s], c), :] for s in range(nch)]
        btc = [bt_ref[pl.ds(r0[s], c), :] for s in range(nch)]
        gct2 = [jnp.concatenate([g, g], axis=0).T for g in gcc]
        gcol = [gcc[s][:, SM_GA + h:SM_GA + h + 1] for s, h in items]
        glast = [gcc[s][c - 1:c, SM_GA + h:SM_GA + h + 1] for s, h in items]
        bcol = [btc[s][:, SM_GB + h:SM_GB + h + 1] for s, h in items]
        q = [qn_ref[pl.ds(r0[s], c), hs[h]] for s, h in items]
        k = [kn_ref[pl.ds(r0[s], c), hs[h]] for s, h in items]
        v = [vv_ref[pl.ds(r0[s], c), hs[h]] for s, h in items]
        kb = [k[i] * bcol[i] for i in range(n_items)]
        eg = [jnp.exp(gcol[i]) for i in range(n_items)]
        rhs = [jnp.concatenate([v[i] * bcol[i], kb[i] * eg[i]], axis=1) for i in range(n_items)]
        ab = [lax.dot_general(
            side(side(kb[2 * p], q[2 * p], 0), side(kb[2 * p + 1], q[2 * p + 1], 0), 1).astype(BF16),
            bdiag(k[2 * p], k[2 * p + 1]).astype(BF16),
            (((1,), (1,)), ((), ())), preferred_element_type=F32) for p in pairs]
        decay = []
        for p in pairs:
            s, h = items[2 * p]
            ha, hb = SM_GA + h, SM_GA + h + 1
            diff = (jnp.where(left, gcol[2 * p], gcol[2 * p + 1])
                    - jnp.where(left_row, gct2[s][ha:ha + 1, :], gct2[s][hb:hb + 1, :]))
            decay.append(jnp.where(incl, jnp.exp(jnp.where(incl, diff, 0.0)), 0.0))
        qk = [jnp.where(incl, ab[p][c:, :] * decay[p], 0.0) for p in pairs]
        pw = [-jnp.where(strict, ab[p][0:c, :] * decay[p], 0.0) for p in pairs]
        r = list(pw)
        for _ in range(int(math.log2(c)) - 1):
            pw = [mm(pw[p], bdiag2(pw[p])) for p in pairs]
            rp = [mm(r[p], bdiag2(pw[p])) for p in pairs]
            r = [r[p] + pw[p] + rp[p] for p in pairs]
        rr = [mm(r[p], bdiag(rhs[2 * p], rhs[2 * p + 1])) for p in pairs]
        sol = [rhs[i] + rr[i // 2][:, (i % 2) * 2 * LANES:(i % 2 + 1) * 2 * LANES] for i in range(n_items)]
        q_dec = [q[i] * eg[i] for i in range(n_items)]
        k_dec_t = [(k[i] * jnp.exp(glast[i] - gcol[i])).T for i in range(n_items)]
        st = [state_ref[h] for h in range(GDN_HEADS)]
        for s in range(nch):
            base = s * GDN_HEADS
            hp = range(GDN_HEADS // 2)
            ws = [mm(side(side(sol[base + 2 * p][:, LANES:], q_dec[base + 2 * p], 0),
                          side(sol[base + 2 * p + 1][:, LANES:], q_dec[base + 2 * p + 1], 0), 1),
                     bdiag(st[2 * p], st[2 * p + 1])) for p in hp]
            v_new = [sol[base + h][:, :LANES] - ws[h // 2][0:c, half(h)] for h in range(GDN_HEADS)]
            qv = [mm(qk[base // 2 + p], bdiag(v_new[2 * p], v_new[2 * p + 1])) for p in hp]
            o_n = [ws[h // 2][c:, half(h)] + qv[h // 2][:, half(h)] for h in range(GDN_HEADS)]
            st = [st[h] * jnp.exp(glast[base + h]) + mm(k_dec_t[base + h], v_new[h]) for h in range(GDN_HEADS)]
            for h in range(GDN_HEADS):
                on = o_n[h] * lax.rsqrt(jnp.mean(o_n[h] * o_n[h], axis=-1, keepdims=True) + EPS) * gn
                zz = z_ref[pl.ds(r0[s], c), hs[h]].astype(F32)
                o_ref[pl.ds(r0[s], c), hs[h]] = (on * (zz * jax.nn.sigmoid(zz))).astype(o_ref.dtype)
        for h in range(GDN_HEADS):
            state_ref[h] = st[h]
        return carry

    lax.fori_loop(0, blk // (c * GDN_CHUNKS_PER_STEP), chunk_body, 0)


def _gdn(proj, small, conv_w, alog_row, dtb_row, gn_row, batch, seq):
    t = proj.shape[0]
    blk = min(GDN_BLK, seq)
    ns = seq // blk
    row = lambda b, s: b * ns + s
    return pl.pallas_call(
        _gdn_kernel,
        grid=(batch, ns),
        in_specs=[pl.BlockSpec((blk, GDN_CONV_DIM), lambda b, s: (row(b, s), COL_QKV * LANES // GDN_CONV_DIM)),
                  pl.BlockSpec((blk, D_MODEL), lambda b, s: (row(b, s), COL_Z * LANES // D_MODEL)),
                  pl.BlockSpec((blk, LANES), lambda b, s: (row(b, s), 0)),
                  pl.BlockSpec((GDN_CONV, GDN_CONV_DIM), lambda b, s: (0, 0)),
                  pl.BlockSpec((1, LANES), lambda b, s: (0, 0)),
                  pl.BlockSpec((1, LANES), lambda b, s: (0, 0)),
                  pl.BlockSpec((1, LANES), lambda b, s: (0, 0))],
        out_specs=pl.BlockSpec((blk, D_MODEL), lambda b, s: (row(b, s), 0)),
        out_shape=jax.ShapeDtypeStruct((t, D_MODEL), BF16),
        scratch_shapes=[pltpu.VMEM((8, GDN_CONV_DIM), F32),
                        pltpu.VMEM((GDN_HEADS, GDN_HEAD, GDN_HEAD), F32),
                        pltpu.VMEM((blk, D_MODEL), F32),
                        pltpu.VMEM((blk, D_MODEL), F32),
                        pltpu.VMEM((blk, D_MODEL), F32),
                        pltpu.VMEM((blk, LANES), F32),
                        pltpu.VMEM((blk, LANES), F32)],
        compiler_params=_cparams(("arbitrary", "arbitrary")),
        name="gdn",
    )(proj, proj, small, conv_w, alog_row, dtb_row, gn_row)


def _rope(y, cos, sin_lo, sin_hi, half):
    return y * cos + pltpu.roll(y, LANES - half, axis=1) * sin_lo + pltpu.roll(y, half, axis=1) * sin_hi


def _prep_kernel(aq_ref, akv_ref, iq_ref, sm_ref, tab_ref, gq_ref, gk_ref,
                 q_out, k_out, v_out, qi_out, kit_out, w_out):
    half_a = ATT_HEAD // ROPE_FRACTION // 2
    half_i = IDX_DIM // ROPE_FRACTION // 2
    t = tab_ref[...]
    lane = lax.broadcasted_iota(I32, t.shape, 1)
    from_lane = lambda src: pltpu.roll(t, (-src) % LANES, axis=1)
    between = lambda lo, hi: (lane >= lo) & (lane < hi)
    ca = jnp.where(lane < 2 * half_a, t, 1.0)
    sa = from_lane(TAB_SIN_A)
    sa1 = jnp.where(lane < half_a, sa, 0.0)
    sa2 = jnp.where(between(half_a, 2 * half_a), sa, 0.0)
    lm = lane & (IDX_DIM - 1)
    low = lane < IDX_DIM
    ci = jnp.where(lm < 2 * half_i, jnp.where(low, from_lane(TAB_COS_I), from_lane(TAB_COS_I - IDX_DIM)), 1.0)
    si = jnp.where(low, from_lane(TAB_SIN_I), from_lane(TAB_SIN_I - IDX_DIM))
    si1 = jnp.where(lm < half_i, si, 0.0)
    si2 = jnp.where((lm >= half_i) & (lm < 2 * half_i), si, 0.0)
    ck = jnp.where(between(SM_IK, SM_IK + 2 * half_i), from_lane(TAB_COS_I - SM_IK), 1.0)
    sk = from_lane(TAB_SIN_I - SM_IK)
    sk1 = jnp.where(between(SM_IK, SM_IK + half_i), sk, 0.0)
    sk2 = jnp.where(between(SM_IK + half_i, SM_IK + 2 * half_i), sk, 0.0)
    gq = gq_ref[...]
    gk = gk_ref[...]
    for h in range(ATT_HEADS):
        hs = slice(h * LANES, (h + 1) * LANES)
        xh = aq_ref[:, hs].astype(F32)
        y = xh * lax.rsqrt(jnp.mean(xh * xh, axis=-1, keepdims=True) + EPS) * gq
        q_out[:, hs] = _rope(y, ca, sa1, sa2, half_a).astype(BF16)
    for cc in range(ATT_KV_HEADS):
        hs = slice(cc * LANES, (cc + 1) * LANES)
        xk = akv_ref[:, hs].astype(F32)
        y = xk * lax.rsqrt(jnp.mean(xk * xk, axis=-1, keepdims=True) + EPS) * gk
        k_out[:, hs] = _rope(y, ca, sa1, sa2, half_a).astype(BF16)
        v_out[:, hs] = akv_ref[:, ATT_KV_HEADS * LANES + cc * LANES:ATT_KV_HEADS * LANES + (cc + 1) * LANES].astype(BF16)
    for j in range(IDX_HEADS * IDX_DIM // LANES):
        hs = slice(j * LANES, (j + 1) * LANES)
        qi_out[:, hs] = _rope(iq_ref[:, hs].astype(F32), ci, si1, si2, half_i).astype(BF16)
    sm = sm_ref[...]
    smr = _rope(sm, ck, sk1, sk2, half_i)
    kit_out[...] = smr.T[SM_IK:SM_IK + IDX_DIM, :].astype(BF16)
    w_out[...] = sm * (IDX_HEADS ** -0.5 * IDX_DIM ** -0.5)


def _prep(proj, small, table, gq_row, gk_row, batch, seq):
    t = proj.shape[0]
    tm = min(TM_PREP, seq)
    nsb = seq // tm
    tab_spec = pl.BlockSpec((tm, LANES), lambda i: (i % nsb, 0))
    row_spec = pl.BlockSpec((1, LANES), lambda i: (0, 0))
    return pl.pallas_call(
        _prep_kernel,
        grid=(t // tm,),
        in_specs=[pl.BlockSpec((tm, D_MODEL), lambda i: (i, COL_AQ * LANES // D_MODEL)),
                  pl.BlockSpec((tm, 4 * LANES), lambda i: (i, COL_AKV // 4)),
                  pl.BlockSpec((tm, 4 * LANES), lambda i: (i, COL_IQ // 4)),
                  pl.BlockSpec((tm, LANES), lambda i: (i, 0))]
                 + [tab_spec, row_spec, row_spec],
        out_specs=[pl.BlockSpec((tm, D_MODEL), lambda i: (i, 0)),
                   pl.BlockSpec((tm, ATT_KV_HEADS * LANES), lambda i: (i, 0)),
                   pl.BlockSpec((tm, ATT_KV_HEADS * LANES), lambda i: (i, 0)),
                   pl.BlockSpec((tm, IDX_HEADS * IDX_DIM), lambda i: (i, 0)),
                   pl.BlockSpec((None, None, IDX_DIM, tm), lambda i: (i // nsb, i % nsb, 0, 0)),
                   pl.BlockSpec((tm, LANES), lambda i: (i, 0))],
        out_shape=[jax.ShapeDtypeStruct((t, D_MODEL), BF16),
                   jax.ShapeDtypeStruct((t, ATT_KV_HEADS * LANES), BF16),
                   jax.ShapeDtypeStruct((t, ATT_KV_HEADS * LANES), BF16),
                   jax.ShapeDtypeStruct((t, IDX_HEADS * IDX_DIM), BF16),
                   jax.ShapeDtypeStruct((batch, nsb, IDX_DIM, tm), BF16),
                   jax.ShapeDtypeStruct((t, LANES), F32)],
        compiler_params=_cparams(("arbitrary",)),
        name="prep",
    )(proj, proj, proj, small, table, gq_row, gk_row)


def _select_kernel(qi_ref, w_ref, kit_ref, ustrict_ref, bias_ref, keys_ref, qh_ref, *, k_sel):
    qblk = pl.program_id(1)
    nt_all = keys_ref.shape[0]
    n_groups = KT // LANES
    nt = (qblk * QB + QB + KT - 1) // KT

    for h in range(IDX_HEADS):
        qh_ref[h * QB:(h + 1) * QB, :] = qi_ref[:, h * IDX_DIM:(h + 1) * IDX_DIM]
    wb = [jnp.broadcast_to(w_ref[:, SM_IW + h:SM_IW + h + 1], (QB, LANES)) for h in range(IDX_HEADS)]
    row_g = qblk * QB + lax.broadcasted_iota(I32, (QB, LANES), 0)
    lane = lax.broadcasted_iota(I32, (QB, LANES), 1)

    def score_tile(j, carry):
        mx, mn, ge0, gt0 = carry
        lg = jnp.dot(qh_ref[...], kit_ref[j], preferred_element_type=F32)
        acc = [jnp.zeros((QB, LANES), F32) for _ in range(n_groups)]
        for h in range(IDX_HEADS):
            for g in range(n_groups):
                acc[g] = acc[g] + jnp.maximum(lg[h * QB:(h + 1) * QB, g * LANES:(g + 1) * LANES], 0.0) * wb[h]
        for g in range(n_groups):
            causal = (j * KT + g * LANES + lane) <= row_g
            sc = jnp.where(causal, acc[g], -jnp.inf)
            keys_ref[j, :, g * LANES:(g + 1) * LANES] = sc
            mx = jnp.maximum(mx, sc)
            mn = jnp.minimum(mn, jnp.where(causal, acc[g], jnp.inf))
            ge0 = ge0 + jnp.where(sc >= 0.0, 1.0, 0.0)
            gt0 = gt0 + jnp.where(sc > 0.0, 1.0, 0.0)
        return mx, mn, ge0, gt0

    zeros = jnp.zeros((QB, LANES), F32)
    mx, mn, ge0, gt0 = _loop_groups(nt, score_tile, (jnp.full((QB, LANES), -jnp.inf, F32),
                                                     jnp.full((QB, LANES), jnp.inf, F32), zeros, zeros), 4)
    row_max = jnp.max(mx, axis=1, keepdims=True)
    row_min = jnp.min(mn, axis=1, keepdims=True)
    n_ge0 = jnp.sum(ge0, axis=1, keepdims=True)
    n_gt0 = jnp.sum(gt0, axis=1, keepdims=True)

    n_causal = (row_g[:, 0:1] + 1).astype(F32)
    k_eff = jnp.minimum(float(k_sel), n_causal)

    def count(cand, strict):
        cb = jnp.broadcast_to(cand, (QB, LANES))

        def tile(j, cnt):
            kk = keys_ref[j]
            for g in range(n_groups):
                kg = kk[:, g * LANES:(g + 1) * LANES]
                cnt = cnt + jnp.where(kg > cb if strict else kg >= cb, 1.0, 0.0)
            return cnt

        cnt = lax.fori_loop(0, nt, tile, jnp.zeros((QB, LANES), F32))
        return jnp.sum(cnt, axis=1, keepdims=True)

    to_key = lambda x: (lambda b: b ^ ((b >> 31) & 0x7FFFFFFF))(pltpu.bitcast(x, I32))
    to_val = lambda kx: pltpu.bitcast(kx ^ ((kx >> 31) & 0x7FFFFFFF), F32)

    def search_cond(st):
        return jnp.logical_and(st[0] < MAX_SEARCH_STEPS, st[1] > 0.0)

    def unsettled(lo, hi):
        mid = 0.5 * lo + 0.5 * hi
        settled = jnp.logical_or(lo == hi, jnp.logical_or(mid == lo, mid == hi))
        return jnp.max(jnp.where(settled, 0.0, 1.0))

    def search_body(st):
        it, _, lo, hi = st
        mid = 0.5 * lo + 0.5 * hi
        n = count(mid, False)
        lo = jnp.where(n >= k_eff, mid, lo)
        hi = jnp.where(n <= k_eff, mid, hi)
        return it + 1, unsettled(lo, hi), lo, hi

    take_all = n_causal <= k_eff
    above = n_gt0 > k_eff
    below = n_ge0 < k_eff
    lo0 = jnp.where(below, to_val(to_key(row_min) - 1), 0.0)
    hi0 = jnp.where(above, to_val(to_key(row_max) + 1), 0.0)
    lo0 = jnp.where(take_all, -FLT_MAX, lo0)
    hi0 = jnp.where(take_all, -FLT_MAX, hi0)
    st = lax.while_loop(search_cond, search_body, (jnp.int32(0), unsettled(lo0, hi0), lo0, hi0))
    vstar, hi_end = st[2], st[3]
    vsb = jnp.broadcast_to(vstar, (QB, KT))
    at_zero = jnp.logical_not(jnp.logical_or(jnp.logical_or(above, below), take_all))
    tied = jnp.logical_or(vstar != hi_end, jnp.logical_and(at_zero, n_ge0 > k_eff))
    any_tied = jnp.max(jnp.where(tied, 1.0, 0.0)) > 0.0

    @pl.when(any_tied)
    def _():
        need = k_eff - count(vstar, True)
        ustrict = ustrict_ref[...]

        def final_tile(j, carry):
            kk = keys_ref[j]
            tie = kk == vsb
            tie_b = jnp.where(tie, 1.0, 0.0).astype(BF16)
            rank = jnp.dot(tie_b, ustrict, preferred_element_type=F32) + carry
            keep_tie = jnp.where(rank < need, 0.0, NEG)
            bias = jnp.where(kk > vsb, 0.0, jnp.where(tie, keep_tie, NEG))
            bias_ref[j] = bias.astype(BF16)
            return carry + jnp.sum(jnp.where(tie, 1.0, 0.0), axis=1, keepdims=True)

        _loop_groups(nt, final_tile, jnp.zeros((QB, 1), F32))

    @pl.when(jnp.logical_not(any_tied))
    def _():
        def final_tile(j, carry):
            bias_ref[j] = jnp.where(keys_ref[j] >= vsb, 0.0, NEG).astype(BF16)
            return carry

        lax.fori_loop(0, nt, final_tile, 0)

    def fill_tile(j, carry):
        bias_ref[j] = jnp.full((QB, KT), NEG, BF16)
        return carry

    lax.fori_loop(nt, nt_all, fill_tile, 0)


def _select(qi, wsm, kit, ustrict, batch, seq, k_sel):
    nq = seq // QB
    nt = seq // KT
    return pl.pallas_call(
        functools.partial(_select_kernel, k_sel=k_sel),
        grid=(batch, nq),
        in_specs=[pl.BlockSpec((QB, IDX_HEADS * IDX_DIM), lambda b, q: (b * nq + q, 0)),
                  pl.BlockSpec((QB, LANES), lambda b, q: (b * nq + q, 0)),
                  pl.BlockSpec((None, nt, IDX_DIM, KT), lambda b, q: (b, 0, 0, 0)),
                  pl.BlockSpec((KT, KT), lambda b, q: (0, 0))],
        out_specs=pl.BlockSpec((None, None, nt, QB, KT), lambda b, q: (b, q, 0, 0, 0)),
        out_shape=jax.ShapeDtypeStruct((batch, nq, nt, QB, KT), BF16),
        scratch_shapes=[pltpu.VMEM((nt, QB, KT), F32),
                        pltpu.VMEM((IDX_HEADS * QB, IDX_DIM), BF16)],
        compiler_params=_cparams(("arbitrary", "arbitrary")),
        name="select",
    )(qi, wsm, kit, ustrict)


def _attn_kernel(q_ref, k_ref, v_ref, bias_ref, bound_ref, o_ref, m_ref, l_ref, acc_ref, accw_ref):
    qblk = pl.program_id(1)
    nt = (qblk * QA + QA + KT - 1) // KT
    scale = ATT_HEAD ** -0.5
    q4 = [jnp.concatenate([q_ref[:, (cc * ATT_GROUP + g) * LANES:(cc * ATT_GROUP + g + 1) * LANES]
                           for g in range(ATT_GROUP)], axis=0) for cc in range(ATT_KV_HEADS)]

    def bias_rows(j):
        b = jnp.concatenate([bias_ref[r, j] for r in range(QA // QB)], axis=0).astype(F32)
        return jnp.concatenate([b] * ATT_GROUP, axis=0)

    def write_out(cc, o):
        for g in range(ATT_GROUP):
            h = cc * ATT_GROUP + g
            o_ref[:, h * LANES:(h + 1) * LANES] = o[g * QA:(g + 1) * QA, :].astype(o_ref.dtype)

    safe = bound_ref[0] <= SAFE_LOGIT_BOUND

    @pl.when(safe)
    def _():
        accw_ref[...] = jnp.zeros_like(accw_ref)
        ones = jnp.ones((KT, LANES), BF16)

        def kv_tile(j, carry):
            r0 = pl.multiple_of(j * KT, KT)
            bias4 = bias_rows(j)
            for cc in range(ATT_KV_HEADS):
                kc = k_ref[pl.ds(r0, KT), cc * LANES:(cc + 1) * LANES]
                vc = v_ref[pl.ds(r0, KT), cc * LANES:(cc + 1) * LANES]
                s = lax.dot_general(q4[cc], kc, (((1,), (1,)), ((), ())), preferred_element_type=F32)
                p = jnp.exp2(s * (scale * LOG2E) + bias4)
                accw_ref[cc] += jnp.dot(p.astype(BF16), jnp.concatenate([vc, ones], axis=1),
                                        preferred_element_type=F32)
            return carry

        _loop_groups(nt, kv_tile, 0)
        for cc in range(ATT_KV_HEADS):
            aw = accw_ref[cc]
            write_out(cc, aw[:, :LANES] / aw[:, LANES:])

    @pl.when(jnp.logical_not(safe))
    def _():
        m_ref[...] = jnp.full_like(m_ref, NEG)
        l_ref[...] = jnp.zeros_like(l_ref)
        acc_ref[...] = jnp.zeros_like(acc_ref)

        def kv_tile(j, carry):
            r0 = pl.multiple_of(j * KT, KT)
            bias4 = bias_rows(j)
            for cc in range(ATT_KV_HEADS):
                kc = k_ref[pl.ds(r0, KT), cc * LANES:(cc + 1) * LANES]
                vc = v_ref[pl.ds(r0, KT), cc * LANES:(cc + 1) * LANES]
                s = lax.dot_general(q4[cc], kc, (((1,), (1,)), ((), ())), preferred_element_type=F32)
                s = s * scale + bias4
                m_prev = m_ref[cc]
                m_new = jnp.maximum(m_prev, jnp.max(s, axis=1, keepdims=True))
                alpha = jnp.exp(m_prev - m_new)
                p = jnp.exp(s - m_new)
                l_ref[cc] = alpha * l_ref[cc] + jnp.sum(p, axis=1, keepdims=True)
                acc_ref[cc] = alpha * acc_ref[cc] + jnp.dot(p.astype(BF16), vc, preferred_element_type=F32)
                m_ref[cc] = m_new
            return carry

        lax.fori_loop(0, nt, kv_tile, 0)
        for cc in range(ATT_KV_HEADS):
            write_out(cc, acc_ref[cc] / l_ref[cc])


def _attention(q, k, v, bias, logit_bound, batch, seq):
    t = q.shape[0]
    nq = seq // QA
    nt = seq // KT
    rows = ATT_GROUP * QA
    return pl.pallas_call(
        _attn_kernel,
        grid=(batch, nq),
        in_specs=[pl.BlockSpec((QA, D_MODEL), lambda b, i: (b * nq + i, 0)),
                  pl.BlockSpec((seq, ATT_KV_HEADS * LANES), lambda b, i: (b, 0)),
                  pl.BlockSpec((seq, ATT_KV_HEADS * LANES), lambda b, i: (b, 0)),
                  pl.BlockSpec((None, QA // QB, nt, QB, KT), lambda b, i: (b, i, 0, 0, 0)),
                  pl.BlockSpec(memory_space=pltpu.SMEM)],
        out_specs=pl.BlockSpec((QA, D_MODEL), lambda b, i: (b * nq + i, 0)),
        out_shape=jax.ShapeDtypeStruct((t, D_MODEL), BF16),
        scratch_shapes=[pltpu.VMEM((ATT_KV_HEADS, rows, 1), F32),
                        pltpu.VMEM((ATT_KV_HEADS, rows, 1), F32),
                        pltpu.VMEM((ATT_KV_HEADS, rows, LANES), F32),
                        pltpu.VMEM((ATT_KV_HEADS, rows, 2 * LANES), F32)],
        compiler_params=_cparams(("arbitrary", "arbitrary")),
        name="attention",
    )(q, k, v, bias, logit_bound)


def _merge_kernel(ga_ref, gb_ref, og_ref, oa_ref, x_ref, w_ref, o_ref):
    merged = (jax.nn.sigmoid(ga_ref[...].astype(F32)) * og_ref[...].astype(F32)
              + jax.nn.sigmoid(gb_ref[...].astype(F32)) * oa_ref[...].astype(F32))
    o_ref[...] = x_ref[...] + jnp.dot(merged.astype(BF16), w_ref[...], preferred_element_type=F32)


def _merge(proj, o_gdn, o_att, x2, w_out):
    t = x2.shape[0]
    tm = TM_MERGE
    blk = lambda c: pl.BlockSpec((tm, D_MODEL), lambda i: (i, c))
    return pl.pallas_call(
        _merge_kernel,
        grid=(t // tm,),
        in_specs=[blk(COL_GA * LANES // D_MODEL), blk(COL_GB * LANES // D_MODEL), blk(0), blk(0), blk(0),
                  pl.BlockSpec((D_MODEL, D_MODEL), lambda i: (0, 0))],
        out_specs=blk(0),
        out_shape=jax.ShapeDtypeStruct((t, D_MODEL), F32),
        compiler_params=_cparams(("arbitrary",)),
        name="merge",
    )(proj, proj, o_gdn, o_att, x2, w_out)


def _mlp_kernel(x_ref, g_ref, wu_ref, wd_ref, o_ref, h_ref, acc_ref):
    f = pl.program_id(1)

    @pl.when(f == 0)
    def _():
        x = x_ref[...]
        ms = jnp.mean(x * x, axis=-1, keepdims=True)
        h_ref[...] = (x * lax.rsqrt(ms + EPS) * g_ref[...]).astype(BF16)
        acc_ref[...] = x

    up = jnp.dot(h_ref[...], wu_ref[...], preferred_element_type=F32)
    act = jnp.square(jnp.maximum(up, 0.0))
    acc_ref[...] += jnp.dot(act.astype(BF16), wd_ref[...], preferred_element_type=F32)

    @pl.when(f == pl.num_programs(1) - 1)
    def _():
        o_ref[...] = acc_ref[...]


def _mlp(x1, g_row, w_up, w_down):
    t = x1.shape[0]
    tm = min(TM_MLP, t)
    return pl.pallas_call(
        _mlp_kernel,
        grid=(t // tm, D_FF // TF_MLP),
        in_specs=[pl.BlockSpec((tm, D_MODEL), lambda i, f: (i, 0)),
                  pl.BlockSpec((1, D_MODEL), lambda i, f: (0, 0)),
                  pl.BlockSpec((D_MODEL, TF_MLP), lambda i, f: (0, f)),
                  pl.BlockSpec((TF_MLP, D_MODEL), lambda i, f: (f, 0))],
        out_specs=pl.BlockSpec((tm, D_MODEL), lambda i, f: (i, 0)),
        out_shape=jax.ShapeDtypeStruct((t, D_MODEL), F32),
        scratch_shapes=[pltpu.VMEM((tm, D_MODEL), BF16), pltpu.VMEM((tm, D_MODEL), F32)],
        compiler_params=_cparams(("arbitrary", "arbitrary")),
        name="mlp",
    )(x1, g_row, w_up, w_down)


def _pack_w_in(w_in):
    pts = np.cumsum(np.array(IN_SPLITS))[:-1].tolist()
    (g_qkv, g_z, g_a, g_b, a_q, a_k, a_v, i_q, i_k, i_w, gate_a, gate_b) = jnp.split(w_in, pts, axis=-1)
    small = jnp.concatenate([g_a, g_b, i_k, i_w], axis=-1)
    small = jnp.pad(small, ((0, 0), (0, LANES - small.shape[-1])))
    main = jnp.concatenate([g_qkv, g_z, a_q, a_k, a_v, i_q, gate_a, gate_b], axis=-1)
    return main.astype(BF16), small.astype(BF16)


def _lane_row(vec, offset=0):
    return jnp.zeros((1, LANES), F32).at[0, offset:offset + vec.shape[0]].set(vec.astype(F32))


def _rope_table(seq):
    tab = np.zeros((seq, LANES), np.float64)
    pos = np.arange(seq, dtype=np.float64)[:, None]
    for rot_dim, base in ((ATT_HEAD // ROPE_FRACTION, 0), (IDX_DIM // ROPE_FRACTION, TAB_COS_I)):
        half = rot_dim // 2
        ang = pos * ROPE_THETA ** (-np.arange(0, rot_dim, 2, dtype=np.float64) / rot_dim)[None, :]
        tab[:, base:base + half] = tab[:, base + half:base + 2 * half] = np.cos(ang)
        tab[:, base + 2 * half:base + 3 * half] = -np.sin(ang)
        tab[:, base + 3 * half:base + 4 * half] = np.sin(ang)
    return jnp.asarray(tab, dtype=F32)


def _layer(x2, batch, seq, norm_mix_g, w_in, conv_w, a_log, dt_bias, gdn_norm_g, q_norm_g, k_norm_g,
           w_out, norm_mlp_g, w_mlp_up, w_mlp_down):
    k_sel = min(TOPK_MAX, seq // 4)
    w_main, w_small = _pack_w_in(w_in)
    g_mix = norm_mix_g.reshape(1, D_MODEL)
    proj = _inproj(x2, g_mix, w_main, TN_IN, BF16, "inproj")
    small = _inproj(x2, g_mix, w_small, LANES, F32, "inproj_small")

    o_gdn = _gdn(proj, small, conv_w, _lane_row(a_log, SM_GA), _lane_row(dt_bias, SM_GA),
                 gdn_norm_g.reshape(1, LANES), batch, seq)

    q_att, k_att, v_att, q_idx, k_idx_t, w_idx = _prep(
        proj, small, _rope_table(seq), q_norm_g.reshape(1, LANES), k_norm_g.reshape(1, LANES), batch, seq)

    ii = np.arange(KT)
    ustrict = jnp.asarray((ii[:, None] < ii[None, :]).astype(np.float32), dtype=BF16)
    bias = _select(q_idx, w_idx, k_idx_t, ustrict, batch, seq, k_sel)
    logit_bound = (1.01 * ATT_HEAD * ATT_HEAD ** -0.5) * jnp.max(jnp.abs(q_norm_g)) * jnp.max(jnp.abs(k_norm_g))
    o_att = _attention(q_att, k_att, v_att, bias, logit_bound.reshape(1).astype(F32), batch, seq)

    x1 = _merge(proj, o_gdn, o_att, x2, w_out.astype(BF16))
    return _mlp(x1, norm_mlp_g.reshape(1, D_MODEL), w_mlp_up.astype(BF16), w_mlp_down.astype(BF16))


def kernel(x, norm_mix_g, w_in, conv_w, a_log, dt_bias, gdn_norm_g, q_norm_g, k_norm_g, w_out,
           norm_mlp_g, w_mlp_up, w_mlp_down):
    batch, seq, _ = x.shape
    x2 = x.reshape(batch * seq, D_MODEL)
    for layer in range(norm_mix_g.shape[0]):
        x2 = _layer(x2, batch, seq, norm_mix_g[layer], w_in[layer], conv_w[layer], a_log[layer],
                    dt_bias[layer], gdn_norm_g[layer], q_norm_g[layer], k_norm_g[layer], w_out[layer],
                    norm_mlp_g[layer], w_mlp_up[layer], w_mlp_down[layer])
    return x2.reshape(batch, seq, D_MODEL)
```

```python
import functools
import math

import numpy as np
import jax
import jax.numpy as jnp
from jax import lax
from jax.experimental import pallas as pl
from jax.experimental.pallas import tpu as pltpu

F32 = jnp.float32
BF16 = jnp.bfloat16
I32 = jnp.int32

D_MODEL = 1024
GDN_HEADS = 8
GDN_HEAD = 128
GDN_CONV = 4
GDN_CHUNK = 64
GDN_CONV_DIM = 3 * GDN_HEADS * GDN_HEAD
ATT_HEADS = 8
ATT_HEAD = 128
ATT_KV_HEADS = 2
ATT_GROUP = ATT_HEADS // ATT_KV_HEADS
IDX_HEADS = 8
IDX_DIM = 64
TOPK_MAX = 256
ROPE_THETA = 500000.0
ROPE_FRACTION = 4
D_FF = 4 * D_MODEL
EPS = 1e-6

IN_SPLITS = (GDN_CONV_DIM, D_MODEL, GDN_HEADS, GDN_HEADS, D_MODEL, 256, 256,
             IDX_HEADS * IDX_DIM, IDX_DIM, IDX_HEADS, D_MODEL, D_MODEL)

LANES = 128
assert 2 * GDN_CHUNK == LANES
COL_QKV = 0
COL_Z = 24
COL_AQ = 32
COL_AKV = 40
COL_IQ = 44
COL_GA = 48
COL_GB = 56
SM_GA = 0
SM_GB = 8
SM_IK = 16
SM_IW = 80
TAB_SIN_A = 2 * (ATT_HEAD // ROPE_FRACTION // 2)
TAB_COS_I = 4 * (ATT_HEAD // ROPE_FRACTION // 2)
TAB_SIN_I = TAB_COS_I + 2 * (IDX_DIM // ROPE_FRACTION // 2)

NEG = -1e30
FLT_MAX = 3.4028234663852886e38
MAX_SEARCH_STEPS = 254 + 24 + 8
LOG2E = 1.4426950408889634
SAFE_LOGIT_BOUND = 60.0

TM_IN = 1024
TN_IN = 16 * LANES
GDN_BLK = 512
GDN_CHUNKS_PER_STEP = 8
TM_PREP = 512
QB = 128
SEL_BLOCKS = 1
QA = 256
KT = 512
assert KT % QB == 0
TM_MLP = 512
TF_MLP = 1024
VMEM_LIMIT = 56 * 2 ** 20


def _cparams(sem):
    return pltpu.CompilerParams(dimension_semantics=sem, vmem_limit_bytes=VMEM_LIMIT)


def _loop_groups(n, body, init, group=2):
    shift = group.bit_length() - 1

    def many(i, carry):
        for s in range(group):
            carry = body(group * i + s, carry)
        return carry

    n_groups = lax.shift_right_logical(n, shift)
    carry = lax.fori_loop(0, n_groups, many, init)
    return lax.fori_loop(lax.shift_left(n_groups, shift), n, body, carry)


def _inproj_kernel(x_ref, g_ref, w_ref, o_ref):
    x = x_ref[...]
    ms = jnp.mean(x * x, axis=-1, keepdims=True)
    h = x * lax.rsqrt(ms + EPS) * g_ref[...]
    o_ref[...] = jnp.dot(h.astype(BF16), w_ref[...], preferred_element_type=F32).astype(o_ref.dtype)


def _inproj(x2, g_row, w_p, tn, out_dtype, name):
    t = x2.shape[0]
    n_cols = w_p.shape[1]
    return pl.pallas_call(
        _inproj_kernel,
        grid=(n_cols // tn, t // TM_IN),
        in_specs=[pl.BlockSpec((TM_IN, D_MODEL), lambda n, m: (m, 0)),
                  pl.BlockSpec((1, D_MODEL), lambda n, m: (0, 0)),
                  pl.BlockSpec((D_MODEL, tn), lambda n, m: (0, n))],
        out_specs=pl.BlockSpec((TM_IN, tn), lambda n, m: (m, n)),
        out_shape=jax.ShapeDtypeStruct((t, n_cols), out_dtype),
        compiler_params=_cparams(("arbitrary", "arbitrary")),
        name=name,
    )(x2, g_row, w_p)


def _gdn_kernel(qkv_ref, z_ref, sm_ref, cw_ref, alog_ref, dtb_ref, gn_ref, o_ref,
                halo_ref, state_ref, qn_ref, kn_ref, vv_ref, gc_ref, bt_ref):
    blk = qkv_ref.shape[0]
    c = GDN_CHUNK

    @pl.when(pl.program_id(1) == 0)
    def _():
        halo_ref[...] = jnp.zeros_like(halo_ref)
        state_ref[...] = jnp.zeros_like(state_ref)

    row8 = lax.broadcasted_iota(I32, (8, LANES), 0)
    for j in range(GDN_CONV_DIM // LANES):
        lo = j * LANES
        u = qkv_ref[:, lo:lo + LANES].astype(F32)
        hal = halo_ref[:, lo:lo + LANES]
        w = cw_ref[:, lo:lo + LANES]
        y = u * w[GDN_CONV - 1:GDN_CONV, :]
        for sft in range(1, GDN_CONV):
            ur = pltpu.roll(u, sft, axis=0)
            hr = pltpu.roll(hal, sft, axis=0)
            top = jnp.where(row8 < sft, hr, ur[0:8, :])
            ush = jnp.concatenate([top, ur[8:, :]], axis=0)
            y = y + ush * w[GDN_CONV - 1 - sft:GDN_CONV - sft, :]
        halo_ref[:, lo:lo + LANES] = u[blk - 8:, :]
        a = y * jax.nn.sigmoid(y)
        head = j % GDN_HEADS
        if j < 2 * GDN_HEADS:
            a = a * lax.rsqrt(jnp.sum(a * a, axis=-1, keepdims=True) + EPS)
        if j < GDN_HEADS:
            qn_ref[:, head * LANES:(head + 1) * LANES] = a * (GDN_HEAD ** -0.5)
        elif j < 2 * GDN_HEADS:
            kn_ref[:, head * LANES:(head + 1) * LANES] = a
        else:
            vv_ref[:, head * LANES:(head + 1) * LANES] = a

    sm = sm_ref[...]
    xg = sm + dtb_ref[...]
    softplus = jnp.maximum(xg, 0.0) + jnp.log1p(jnp.exp(-jnp.abs(xg)))
    gc = -jnp.exp(alog_ref[...]) * softplus
    row_in_chunk = lax.broadcasted_iota(I32, (blk, LANES), 0) & (c - 1)
    d = 1
    while d < c:
        gc = gc + jnp.where(row_in_chunk >= d, pltpu.roll(gc, d, axis=0), 0.0)
        d *= 2
    gc_ref[...] = gc
    bt_ref[...] = jax.nn.sigmoid(sm)

    ii = lax.broadcasted_iota(I32, (c, 2 * c), 0)
    lane_cc = lax.broadcasted_iota(I32, (c, 2 * c), 1)
    jj = lane_cc & (c - 1)
    incl = ii >= jj
    strict = ii > jj
    left = lane_cc < c
    left_row = lax.broadcasted_iota(I32, (1, 2 * c), 1) < c
    gn = gn_ref[...]

    def chunk_body(ci, carry):
        nch = GDN_CHUNKS_PER_STEP
        r0 = [pl.multiple_of((ci * nch + s) * c, c) for s in range(nch)]
        items = [(s, h) for s in range(nch) for h in range(GDN_HEADS)]
        n_items = len(items)
        pairs = range(n_items // 2)
        hs = [slice(h * LANES, (h + 1) * LANES) for h in range(GDN_HEADS)]
        half = lambda i: slice((i % 2) * LANES, (i % 2 + 1) * LANES)
        mm = lambda a, b: jnp.dot(a.astype(BF16), b.astype(BF16), preferred_element_type=F32)

        def bdiag(xa, xb):
            return jnp.concatenate([jnp.concatenate([xa, jnp.zeros_like(xb)], axis=1),
                                    jnp.concatenate([jnp.zeros_like(xa), xb], axis=1)], axis=0)

        def bdiag2(x2):
            return jnp.concatenate([jnp.where(left, x2, 0.0), jnp.where(left, 0.0, x2)], axis=0)

        def side(xa, xb, axis):
            return jnp.concatenate([xa, xb], axis=axis)

        gcc = [gc_ref[pl.ds(r0[s], c), :] for s in range(nch)]
        btc = [bt_ref[pl.ds(r0[s], c), :] for s in range(nch)]
        gct2 = [jnp.concatenate([g, g], axis=0).T for g in gcc]
        gcol = [gcc[s][:, SM_GA + h:SM_GA + h + 1] for s, h in items]
        glast = [gcc[s][c - 1:c, SM_GA + h:SM_GA + h + 1] for s, h in items]
        bcol = [btc[s][:, SM_GB + h:SM_GB + h + 1] for s, h in items]
        q = [qn_ref[pl.ds(r0[s], c), hs[h]] for s, h in items]
        k = [kn_ref[pl.ds(r0[s], c), hs[h]] for s, h in items]
        v = [vv_ref[pl.ds(r0[s], c), hs[h]] for s, h in items]
        kb = [k[i] * bcol[i] for i in range(n_items)]
        eg = [jnp.exp(gcol[i]) for i in range(n_items)]
        rhs = [jnp.concatenate([v[i] * bcol[i], kb[i] * eg[i]], axis=1) for i in range(n_items)]
        ab = [lax.dot_general(
            side(side(kb[2 * p], q[2 * p], 0), side(kb[2 * p + 1], q[2 * p + 1], 0), 1).astype(BF16),
            bdiag(k[2 * p], k[2 * p + 1]).astype(BF16),
            (((1,), (1,)), ((), ())), preferred_element_type=F32) for p in pairs]
        decay = []
        for p in pairs:
            s, h = items[2 * p]
            ha, hb = SM_GA + h, SM_GA + h + 1
            diff = (jnp.where(left, gcol[2 * p], gcol[2 * p + 1])
                    - jnp.where(left_row, gct2[s][ha:ha + 1, :], gct2[s][hb:hb + 1, :]))
            decay.append(jnp.where(incl, jnp.exp(jnp.where(incl, diff, 0.0)), 0.0))
        qk = [jnp.where(incl, ab[p][c:, :] * decay[p], 0.0) for p in pairs]
        pw = [-jnp.where(strict, ab[p][0:c, :] * decay[p], 0.0) for p in pairs]
        r = list(pw)
        for _ in range(int(math.log2(c)) - 1):
            pw = [mm(pw[p], bdiag2(pw[p])) for p in pairs]
            rp = [mm(r[p], bdiag2(pw[p])) for p in pairs]
            r = [r[p] + pw[p] + rp[p] for p in pairs]
        rr = [mm(r[p], bdiag(rhs[2 * p], rhs[2 * p + 1])) for p in pairs]
        sol = [rhs[i] + rr[i // 2][:, (i % 2) * 2 * LANES:(i % 2 + 1) * 2 * LANES] for i in range(n_items)]
        q_dec = [q[i] * eg[i] for i in range(n_items)]
        k_dec_t = [(k[i] * jnp.exp(glast[i] - gcol[i])).T for i in range(n_items)]
        st = [state_ref[h] for h in range(GDN_HEADS)]
        for s in range(nch):
            base = s * GDN_HEADS
            hp = range(GDN_HEADS // 2)
            ws = [mm(side(side(sol[base + 2 * p][:, LANES:], q_dec[base + 2 * p], 0),
                          side(sol[base + 2 * p + 1][:, LANES:], q_dec[base + 2 * p + 1], 0), 1),
                     bdiag(st[2 * p], st[2 * p + 1])) for p in hp]
            v_new = [sol[base + h][:, :LANES] - ws[h // 2][0:c, half(h)] for h in range(GDN_HEADS)]
            qv = [mm(qk[base // 2 + p], bdiag(v_new[2 * p], v_new[2 * p + 1])) for p in hp]
            o_n = [ws[h // 2][c:, half(h)] + qv[h // 2][:, half(h)] for h in range(GDN_HEADS)]
            st = [st[h] * jnp.exp(glast[base + h]) + mm(k_dec_t[base + h], v_new[h]) for h in range(GDN_HEADS)]
            for h in range(GDN_HEADS):
                on = o_n[h] * lax.rsqrt(jnp.mean(o_n[h] * o_n[h], axis=-1, keepdims=True) + EPS) * gn
                zz = z_ref[pl.ds(r0[s], c), hs[h]].astype(F32)
                o_ref[pl.ds(r0[s], c), hs[h]] = (on * (zz * jax.nn.sigmoid(zz))).astype(o_ref.dtype)
        for h in range(GDN_HEADS):
            state_ref[h] = st[h]
        return carry

    lax.fori_loop(0, blk // (c * GDN_CHUNKS_PER_STEP), chunk_body, 0)


def _gdn(proj, small, conv_w, alog_row, dtb_row, gn_row, batch, seq):
    t = proj.shape[0]
    blk = min(GDN_BLK, seq)
    ns = seq // blk
    row = lambda b, s: b * ns + s
    return pl.pallas_call(
        _gdn_kernel,
        grid=(batch, ns),
        in_specs=[pl.BlockSpec((blk, GDN_CONV_DIM), lambda b, s: (row(b, s), COL_QKV * LANES // GDN_CONV_DIM)),
                  pl.BlockSpec((blk, D_MODEL), lambda b, s: (row(b, s), COL_Z * LANES // D_MODEL)),
                  pl.BlockSpec((blk, LANES), lambda b, s: (row(b, s), 0)),
                  pl.BlockSpec((GDN_CONV, GDN_CONV_DIM), lambda b, s: (0, 0)),
                  pl.BlockSpec((1, LANES), lambda b, s: (0, 0)),
                  pl.BlockSpec((1, LANES), lambda b, s: (0, 0)),
                  pl.BlockSpec((1, LANES), lambda b, s: (0, 0))],
        out_specs=pl.BlockSpec((blk, D_MODEL), lambda b, s: (row(b, s), 0)),
        out_shape=jax.ShapeDtypeStruct((t, D_MODEL), BF16),
        scratch_shapes=[pltpu.VMEM((8, GDN_CONV_DIM), F32),
                        pltpu.VMEM((GDN_HEADS, GDN_HEAD, GDN_HEAD), F32),
                        pltpu.VMEM((blk, D_MODEL), F32),
                        pltpu.VMEM((blk, D_MODEL), F32),
                        pltpu.VMEM((blk, D_MODEL), F32),
                        pltpu.VMEM((blk, LANES), F32),
                        pltpu.VMEM((blk, LANES), F32)],
        compiler_params=_cparams(("arbitrary", "arbitrary")),
        name="gdn",
    )(proj, proj, small, conv_w, alog_row, dtb_row, gn_row)


def _rope(y, cos, sin_lo, sin_hi, half):
    return y * cos + pltpu.roll(y, LANES - half, axis=1) * sin_lo + pltpu.roll(y, half, axis=1) * sin_hi


def _prep_kernel(aq_ref, akv_ref, iq_ref, sm_ref, tab_ref, gq_ref, gk_ref,
                 q_out, k_out, v_out, qi_out, kit_out, w_out):
    half_a = ATT_HEAD // ROPE_FRACTION // 2
    half_i = IDX_DIM // ROPE_FRACTION // 2
    t = tab_ref[...]
    lane = lax.broadcasted_iota(I32, t.shape, 1)
    from_lane = lambda src: pltpu.roll(t, (-src) % LANES, axis=1)
    between = lambda lo, hi: (lane >= lo) & (lane < hi)
    ca = jnp.where(lane < 2 * half_a, t, 1.0)
    sa = from_lane(TAB_SIN_A)
    sa1 = jnp.where(lane < half_a, sa, 0.0)
    sa2 = jnp.where(between(half_a, 2 * half_a), sa, 0.0)
    lm = lane & (IDX_DIM - 1)
    low = lane < IDX_DIM
    ci = jnp.where(lm < 2 * half_i, jnp.where(low, from_lane(TAB_COS_I), from_lane(TAB_COS_I - IDX_DIM)), 1.0)
    si = jnp.where(low, from_lane(TAB_SIN_I), from_lane(TAB_SIN_I - IDX_DIM))
    si1 = jnp.where(lm < half_i, si, 0.0)
    si2 = jnp.where((lm >= half_i) & (lm < 2 * half_i), si, 0.0)
    ck = jnp.where(between(SM_IK, SM_IK + 2 * half_i), from_lane(TAB_COS_I - SM_IK), 1.0)
    sk = from_lane(TAB_SIN_I - SM_IK)
    sk1 = jnp.where(between(SM_IK, SM_IK + half_i), sk, 0.0)
    sk2 = jnp.where(between(SM_IK + half_i, SM_IK + 2 * half_i), sk, 0.0)
    gq = gq_ref[...]
    gk = gk_ref[...]
    for h in range(ATT_HEADS):
        hs = slice(h * LANES, (h + 1) * LANES)
        xh = aq_ref[:, hs].astype(F32)
        y = xh * lax.rsqrt(jnp.mean(xh * xh, axis=-1, keepdims=True) + EPS) * gq
        q_out[:, hs] = _rope(y, ca, sa1, sa2, half_a).astype(BF16)
    for cc in range(ATT_KV_HEADS):
        hs = slice(cc * LANES, (cc + 1) * LANES)
        xk = akv_ref[:, hs].astype(F32)
        y = xk * lax.rsqrt(jnp.mean(xk * xk, axis=-1, keepdims=True) + EPS) * gk
        k_out[:, hs] = _rope(y, ca, sa1, sa2, half_a).astype(BF16)
        v_out[:, hs] = akv_ref[:, ATT_KV_HEADS * LANES + cc * LANES:ATT_KV_HEADS * LANES + (cc + 1) * LANES].astype(BF16)
    for j in range(IDX_HEADS * IDX_DIM // LANES):
        hs = slice(j * LANES, (j + 1) * LANES)
        qi_out[:, hs] = _rope(iq_ref[:, hs].astype(F32), ci, si1, si2, half_i).astype(BF16)
    sm = sm_ref[...]
    smr = _rope(sm, ck, sk1, sk2, half_i)
    kit_out[...] = smr.T[SM_IK:SM_IK + IDX_DIM, :].astype(BF16)
    w_out[...] = sm * (IDX_HEADS ** -0.5 * IDX_DIM ** -0.5)


def _prep(proj, small, table, gq_row, gk_row, batch, seq):
    t = proj.shape[0]
    tm = min(TM_PREP, seq)
    nsb = seq // tm
    tab_spec = pl.BlockSpec((tm, LANES), lambda i: (i % nsb, 0))
    row_spec = pl.BlockSpec((1, LANES), lambda i: (0, 0))
    return pl.pallas_call(
        _prep_kernel,
        grid=(t // tm,),
        in_specs=[pl.BlockSpec((tm, D_MODEL), lambda i: (i, COL_AQ * LANES // D_MODEL)),
                  pl.BlockSpec((tm, 4 * LANES), lambda i: (i, COL_AKV // 4)),
                  pl.BlockSpec((tm, 4 * LANES), lambda i: (i, COL_IQ // 4)),
                  pl.BlockSpec((tm, LANES), lambda i: (i, 0))]
                 + [tab_spec, row_spec, row_spec],
        out_specs=[pl.BlockSpec((tm, D_MODEL), lambda i: (i, 0)),
                   pl.BlockSpec((tm, ATT_KV_HEADS * LANES), lambda i: (i, 0)),
                   pl.BlockSpec((tm, ATT_KV_HEADS * LANES), lambda i: (i, 0)),
                   pl.BlockSpec((tm, IDX_HEADS * IDX_DIM), lambda i: (i, 0)),
                   pl.BlockSpec((None, None, IDX_DIM, tm), lambda i: (i // nsb, i % nsb, 0, 0)),
                   pl.BlockSpec((tm, LANES), lambda i: (i, 0))],
        out_shape=[jax.ShapeDtypeStruct((t, D_MODEL), BF16),
                   jax.ShapeDtypeStruct((t, ATT_KV_HEADS * LANES), BF16),
                   jax.ShapeDtypeStruct((t, ATT_KV_HEADS * LANES), BF16),
                   jax.ShapeDtypeStruct((t, IDX_HEADS * IDX_DIM), BF16),
                   jax.ShapeDtypeStruct((batch, nsb, IDX_DIM, tm), BF16),
                   jax.ShapeDtypeStruct((t, LANES), F32)],
        compiler_params=_cparams(("arbitrary",)),
        name="prep",
    )(proj, proj, proj, small, table, gq_row, gk_row)


def _select_kernel(qi_ref, w_ref, kit_ref, ustrict_ref, bias_ref, keys_ref, qh_ref, *, k_sel):
    nt_all = keys_ref.shape[1]
    n_groups = KT // LANES
    lane = lax.broadcasted_iota(I32, (QB, LANES), 1)
    to_key = lambda x: (lambda b: b ^ ((b >> 31) & 0x7FFFFFFF))(pltpu.bitcast(x, I32))
    to_val = lambda kx: pltpu.bitcast(kx ^ ((kx >> 31) & 0x7FFFFFFF), F32)

    def count(sb, nt, cand, strict):
        cb = jnp.broadcast_to(cand, (QB, LANES))

        def tile(j, cnt):
            kk = keys_ref[sb, j]
            for g in range(n_groups):
                kg = kk[:, g * LANES:(g + 1) * LANES]
                cnt = cnt + jnp.where(kg > cb if strict else kg >= cb, 1.0, 0.0)
            return cnt

        cnt = lax.fori_loop(0, nt, tile, jnp.zeros((QB, LANES), F32))
        return jnp.sum(cnt, axis=1, keepdims=True)

    def score_block(sb):
        qblk = pl.program_id(1) * SEL_BLOCKS + sb
        nt = (qblk * QB + QB + KT - 1) // KT
        rows = slice(sb * QB, (sb + 1) * QB)
        for h in range(IDX_HEADS):
            qh_ref[h * QB:(h + 1) * QB, :] = qi_ref[rows, h * IDX_DIM:(h + 1) * IDX_DIM]
        wb = [jnp.broadcast_to(w_ref[rows, SM_IW + h:SM_IW + h + 1], (QB, LANES)) for h in range(IDX_HEADS)]
        row_g = qblk * QB + lax.broadcasted_iota(I32, (QB, LANES), 0)

        def score_tile(j, carry, diagonal):
            mx, mn, ge0, gt0 = carry
            lg = jnp.dot(qh_ref[...], kit_ref[j], preferred_element_type=F32)
            acc = [jnp.zeros((QB, LANES), F32) for _ in range(n_groups)]
            for h in range(IDX_HEADS):
                for g in range(n_groups):
                    acc[g] = acc[g] + jnp.maximum(lg[h * QB:(h + 1) * QB, g * LANES:(g + 1) * LANES], 0.0) * wb[h]
            for g in range(n_groups):
                if diagonal:
                    causal = (j * KT + g * LANES + lane) <= row_g
                    sc = jnp.where(causal, acc[g], -jnp.inf)
                    mn = jnp.minimum(mn, jnp.where(causal, acc[g], jnp.inf))
                else:
                    sc = acc[g]
                    mn = jnp.minimum(mn, sc)
                keys_ref[sb, j, :, g * LANES:(g + 1) * LANES] = sc
                mx = jnp.maximum(mx, sc)
                ge0 = ge0 + jnp.where(sc >= 0.0, 1.0, 0.0)
                gt0 = gt0 + jnp.where(sc > 0.0, 1.0, 0.0)
            return mx, mn, ge0, gt0

        zeros = jnp.zeros((QB, LANES), F32)
        carry = _loop_groups(nt - 1, functools.partial(score_tile, diagonal=False),
                             (jnp.full((QB, LANES), -jnp.inf, F32), jnp.full((QB, LANES), jnp.inf, F32),
                              zeros, zeros), 4)
        mx, mn, ge0, gt0 = score_tile(nt - 1, carry, diagonal=True)
        row_max = jnp.max(mx, axis=1, keepdims=True)
        row_min = jnp.min(mn, axis=1, keepdims=True)
        n_ge0 = jnp.sum(ge0, axis=1, keepdims=True)
        n_gt0 = jnp.sum(gt0, axis=1, keepdims=True)
        n_causal = (row_g[:, 0:1] + 1).astype(F32)
        k_eff = jnp.minimum(float(k_sel), n_causal)
        take_all = n_causal <= k_eff
        above = n_gt0 > k_eff
        below = n_ge0 < k_eff
        lo0 = jnp.where(below, to_val(to_key(row_min) - 1), 0.0)
        hi0 = jnp.where(above, to_val(to_key(row_max) + 1), 0.0)
        lo0 = jnp.where(take_all, -FLT_MAX, lo0)
        hi0 = jnp.where(take_all, -FLT_MAX, hi0)
        at_zero = jnp.logical_not(jnp.logical_or(jnp.logical_or(above, below), take_all))
        zero_tied = jnp.logical_and(at_zero, n_ge0 > k_eff)
        return dict(nt=nt, k_eff=k_eff, lo0=lo0, hi0=hi0, zero_tied=zero_tied)

    blocks = [score_block(sb) for sb in range(SEL_BLOCKS)]

    def unsettled(los, his):
        flags = []
        for lo, hi in zip(los, his):
            mid = 0.5 * lo + 0.5 * hi
            settled = jnp.logical_or(lo == hi, jnp.logical_or(mid == lo, mid == hi))
            flags.append(jnp.where(settled, 0.0, 1.0))
        return jnp.max(functools.reduce(jnp.maximum, flags))

    def search_cond(st):
        return jnp.logical_and(st[0] < MAX_SEARCH_STEPS, st[1] > 0.0)

    def search_body(st):
        it, _, los, his = st
        mids = [0.5 * lo + 0.5 * hi for lo, hi in zip(los, his)]
        ns = [count(sb, blocks[sb]["nt"], mids[sb], False) for sb in range(SEL_BLOCKS)]
        los = tuple(jnp.where(ns[sb] >= blocks[sb]["k_eff"], mids[sb], los[sb]) for sb in range(SEL_BLOCKS))
        his = tuple(jnp.where(ns[sb] <= blocks[sb]["k_eff"], mids[sb], his[sb]) for sb in range(SEL_BLOCKS))
        return it + 1, unsettled(los, his), los, his

    los0 = tuple(b["lo0"] for b in blocks)
    his0 = tuple(b["hi0"] for b in blocks)
    _, _, los, his = lax.while_loop(search_cond, search_body, (jnp.int32(0), unsettled(los0, his0), los0, his0))

    def finish_block(sb):
        nt, k_eff = blocks[sb]["nt"], blocks[sb]["k_eff"]
        vstar = los[sb]
        vsb = jnp.broadcast_to(vstar, (QB, KT))
        tied = jnp.logical_or(vstar != his[sb], blocks[sb]["zero_tied"])
        any_tied = jnp.max(jnp.where(tied, 1.0, 0.0)) > 0.0

        @pl.when(any_tied)
        def _():
            need = k_eff - count(sb, nt, vstar, True)
            ustrict = ustrict_ref[...]

            def final_tile(j, carry):
                kk = keys_ref[sb, j]
                tie = kk == vsb
                tie_b = jnp.where(tie, 1.0, 0.0).astype(BF16)
                rank = jnp.dot(tie_b, ustrict, preferred_element_type=F32) + carry
                keep_tie = jnp.where(rank < need, 0.0, NEG)
                bias = jnp.where(kk > vsb, 0.0, jnp.where(tie, keep_tie, NEG))
                bias_ref[sb, j] = bias.astype(BF16)
                return carry + jnp.sum(jnp.where(tie, 1.0, 0.0), axis=1, keepdims=True)

            _loop_groups(nt, final_tile, jnp.zeros((QB, 1), F32))

        @pl.when(jnp.logical_not(any_tied))
        def _():
            def final_tile(j, carry):
                bias_ref[sb, j] = jnp.where(keys_ref[sb, j] >= vsb, 0.0, NEG).astype(BF16)
                return carry

            lax.fori_loop(0, nt, final_tile, 0)

        def fill_tile(j, carry):
            bias_ref[sb, j] = jnp.full((QB, KT), NEG, BF16)
            return carry

        lax.fori_loop(nt, nt_all, fill_tile, 0)

    for sb in range(SEL_BLOCKS):
        finish_block(sb)


def _select(qi, wsm, kit, ustrict, batch, seq, k_sel):
    nq = seq // (QB * SEL_BLOCKS)
    nt = seq // KT
    rows = QB * SEL_BLOCKS
    return pl.pallas_call(
        functools.partial(_select_kernel, k_sel=k_sel),
        grid=(batch, nq),
        in_specs=[pl.BlockSpec((rows, IDX_HEADS * IDX_DIM), lambda b, q: (b * nq + q, 0)),
                  pl.BlockSpec((rows, LANES), lambda b, q: (b * nq + q, 0)),
                  pl.BlockSpec((None, nt, IDX_DIM, KT), lambda b, q: (b, 0, 0, 0)),
                  pl.BlockSpec((KT, KT), lambda b, q: (0, 0))],
        out_specs=pl.BlockSpec((None, SEL_BLOCKS, nt, QB, KT), lambda b, q: (b, q, 0, 0, 0)),
        out_shape=jax.ShapeDtypeStruct((batch, nq * SEL_BLOCKS, nt, QB, KT), BF16),
        scratch_shapes=[pltpu.VMEM((SEL_BLOCKS, nt, QB, KT), F32),
                        pltpu.VMEM((IDX_HEADS * QB, IDX_DIM), BF16)],
        compiler_params=_cparams(("arbitrary", "arbitrary")),
        name="select",
    )(qi, wsm, kit, ustrict)


def _attn_kernel(q_ref, k_ref, v_ref, bias_ref, bound_ref, o_ref, m_ref, l_ref, acc_ref, accw_ref):
    qblk = pl.program_id(1)
    nt = (qblk * QA + QA + KT - 1) // KT
    scale = ATT_HEAD ** -0.5
    q4 = [jnp.concatenate([q_ref[:, (cc * ATT_GROUP + g) * LANES:(cc * ATT_GROUP + g + 1) * LANES]
                           for g in range(ATT_GROUP)], axis=0) for cc in range(ATT_KV_HEADS)]

    def bias_rows(j):
        b = jnp.concatenate([bias_ref[r, j] for r in range(QA // QB)], axis=0).astype(F32)
        return jnp.concatenate([b] * ATT_GROUP, axis=0)

    def write_out(cc, o):
        for g in range(ATT_GROUP):
            h = cc * ATT_GROUP + g
            o_ref[:, h * LANES:(h + 1) * LANES] = o[g * QA:(g + 1) * QA, :].astype(o_ref.dtype)

    safe = bound_ref[0] <= SAFE_LOGIT_BOUND

    @pl.when(safe)
    def _():
        accw_ref[...] = jnp.zeros_like(accw_ref)
        ones = jnp.ones((KT, LANES), BF16)

        def kv_tile(j, carry):
            r0 = pl.multiple_of(j * KT, KT)
            bias4 = bias_rows(j)
            for cc in range(ATT_KV_HEADS):
                kc = k_ref[pl.ds(r0, KT), cc * LANES:(cc + 1) * LANES]
                vc = v_ref[pl.ds(r0, KT), cc * LANES:(cc + 1) * LANES]
                s = lax.dot_general(q4[cc], kc, (((1,), (1,)), ((), ())), preferred_element_type=F32)
                p = jnp.exp2(s * (scale * LOG2E) + bias4)
                accw_ref[cc] += jnp.dot(p.astype(BF16), jnp.concatenate([vc, ones], axis=1),
                                        preferred_element_type=F32)
            return carry

        _loop_groups(nt, kv_tile, 0)
        for cc in range(ATT_KV_HEADS):
            aw = accw_ref[cc]
            write_out(cc, aw[:, :LANES] / aw[:, LANES:])

    @pl.when(jnp.logical_not(safe))
    def _():
        m_ref[...] = jnp.full_like(m_ref, NEG)
        l_ref[...] = jnp.zeros_like(l_ref)
        acc_ref[...] = jnp.zeros_like(acc_ref)

        def kv_tile(j, carry):
            r0 = pl.multiple_of(j * KT, KT)
            bias4 = bias_rows(j)
            for cc in range(ATT_KV_HEADS):
                kc = k_ref[pl.ds(r0, KT), cc * LANES:(cc + 1) * LANES]
                vc = v_ref[pl.ds(r0, KT), cc * LANES:(cc + 1) * LANES]
                s = lax.dot_general(q4[cc], kc, (((1,), (1,)), ((), ())), preferred_element_type=F32)
                s = s * scale + bias4
                m_prev = m_ref[cc]
                m_new = jnp.maximum(m_prev, jnp.max(s, axis=1, keepdims=True))
                alpha = jnp.exp(m_prev - m_new)
                p = jnp.exp(s - m_new)
                l_ref[cc] = alpha * l_ref[cc] + jnp.sum(p, axis=1, keepdims=True)
                acc_ref[cc] = alpha * acc_ref[cc] + jnp.dot(p.astype(BF16), vc, preferred_element_type=F32)
                m_ref[cc] = m_new
            return carry

        lax.fori_loop(0, nt, kv_tile, 0)
        for cc in range(ATT_KV_HEADS):
            write_out(cc, acc_ref[cc] / l_ref[cc])


def _attention(q, k, v, bias, logit_bound, batch, seq):
    t = q.shape[0]
    nq = seq // QA
    nt = seq // KT
    rows = ATT_GROUP * QA
    return pl.pallas_call(
        _attn_kernel,
        grid=(batch, nq),
        in_specs=[pl.BlockSpec((QA, D_MODEL), lambda b, i: (b * nq + i, 0)),
                  pl.BlockSpec((seq, ATT_KV_HEADS * LANES), lambda b, i: (b, 0)),
                  pl.BlockSpec((seq, ATT_KV_HEADS * LANES), lambda b, i: (b, 0)),
                  pl.BlockSpec((None, QA // QB, nt, QB, KT), lambda b, i: (b, i, 0, 0, 0)),
                  pl.BlockSpec(memory_space=pltpu.SMEM)],
        out_specs=pl.BlockSpec((QA, D_MODEL), lambda b, i: (b * nq + i, 0)),
        out_shape=jax.ShapeDtypeStruct((t, D_MODEL), BF16),
        scratch_shapes=[pltpu.VMEM((ATT_KV_HEADS, rows, 1), F32),
                        pltpu.VMEM((ATT_KV_HEADS, rows, 1), F32),
                        pltpu.VMEM((ATT_KV_HEADS, rows, LANES), F32),
                        pltpu.VMEM((ATT_KV_HEADS, rows, 2 * LANES), F32)],
        compiler_params=_cparams(("arbitrary", "arbitrary")),
        name="attention",
    )(q, k, v, bias, logit_bound)


def _mlp_kernel(ga_ref, gb_ref, og_ref, oa_ref, x_ref, wo_ref, g_ref, wu_ref, wd_ref, o_ref, h_ref, acc_ref):
    f = pl.program_id(1)

    @pl.when(f == 0)
    def _():
        merged = (jax.nn.sigmoid(ga_ref[...].astype(F32)) * og_ref[...].astype(F32)
                  + jax.nn.sigmoid(gb_ref[...].astype(F32)) * oa_ref[...].astype(F32))
        x = x_ref[...] + jnp.dot(merged.astype(BF16), wo_ref[...], preferred_element_type=F32)
        ms = jnp.mean(x * x, axis=-1, keepdims=True)
        h_ref[...] = (x * lax.rsqrt(ms + EPS) * g_ref[...]).astype(BF16)
        acc_ref[...] = x

    up = jnp.dot(h_ref[...], wu_ref[...], preferred_element_type=F32)
    act = jnp.square(jnp.maximum(up, 0.0))
    acc_ref[...] += jnp.dot(act.astype(BF16), wd_ref[...], preferred_element_type=F32)

    @pl.when(f == pl.num_programs(1) - 1)
    def _():
        o_ref[...] = acc_ref[...]


def _merge_mlp(proj, o_gdn, o_att, x2, w_out, g_row, w_up, w_down):
    t = x2.shape[0]
    tm = min(TM_MLP, t)
    blk = lambda c: pl.BlockSpec((tm, D_MODEL), lambda i, f: (i, c))
    return pl.pallas_call(
        _mlp_kernel,
        grid=(t // tm, D_FF // TF_MLP),
        in_specs=[blk(COL_GA * LANES // D_MODEL), blk(COL_GB * LANES // D_MODEL), blk(0), blk(0), blk(0),
                  pl.BlockSpec((D_MODEL, D_MODEL), lambda i, f: (0, 0)),
                  pl.BlockSpec((1, D_MODEL), lambda i, f: (0, 0)),
                  pl.BlockSpec((D_MODEL, TF_MLP), lambda i, f: (0, f)),
                  pl.BlockSpec((TF_MLP, D_MODEL), lambda i, f: (f, 0))],
        out_specs=blk(0),
        out_shape=jax.ShapeDtypeStruct((t, D_MODEL), F32),
        scratch_shapes=[pltpu.VMEM((tm, D_MODEL), BF16), pltpu.VMEM((tm, D_MODEL), F32)],
        compiler_params=_cparams(("arbitrary", "arbitrary")),
        name="merge_mlp",
    )(proj, proj, o_gdn, o_att, x2, w_out, g_row, w_up, w_down)


def _pack_w_in(w_in):
    pts = np.cumsum(np.array(IN_SPLITS))[:-1].tolist()
    (g_qkv, g_z, g_a, g_b, a_q, a_k, a_v, i_q, i_k, i_w, gate_a, gate_b) = jnp.split(w_in, pts, axis=-1)
    small = jnp.concatenate([g_a, g_b, i_k, i_w], axis=-1)
    small = jnp.pad(small, ((0, 0), (0, LANES - small.shape[-1])))
    main = jnp.concatenate([g_qkv, g_z, a_q, a_k, a_v, i_q, gate_a, gate_b], axis=-1)
    return main.astype(BF16), small.astype(BF16)


def _lane_row(vec, offset=0):
    return jnp.zeros((1, LANES), F32).at[0, offset:offset + vec.shape[0]].set(vec.astype(F32))


def _rope_table(seq):
    tab = np.zeros((seq, LANES), np.float64)
    pos = np.arange(seq, dtype=np.float64)[:, None]
    for rot_dim, base in ((ATT_HEAD // ROPE_FRACTION, 0), (IDX_DIM // ROPE_FRACTION, TAB_COS_I)):
        half = rot_dim // 2
        ang = pos * ROPE_THETA ** (-np.arange(0, rot_dim, 2, dtype=np.float64) / rot_dim)[None, :]
        tab[:, base:base + half] = tab[:, base + half:base + 2 * half] = np.cos(ang)
        tab[:, base + 2 * half:base + 3 * half] = -np.sin(ang)
        tab[:, base + 3 * half:base + 4 * half] = np.sin(ang)
    return jnp.asarray(tab, dtype=F32)


def _layer(x2, batch, seq, norm_mix_g, w_in, conv_w, a_log, dt_bias, gdn_norm_g, q_norm_g, k_norm_g,
           w_out, norm_mlp_g, w_mlp_up, w_mlp_down):
    k_sel = min(TOPK_MAX, seq // 4)
    w_main, w_small = _pack_w_in(w_in)
    g_mix = norm_mix_g.reshape(1, D_MODEL)
    proj = _inproj(x2, g_mix, w_main, TN_IN, BF16, "inproj")
    small = _inproj(x2, g_mix, w_small, LANES, F32, "inproj_small")

    o_gdn = _gdn(proj, small, conv_w, _lane_row(a_log, SM_GA), _lane_row(dt_bias, SM_GA),
                 gdn_norm_g.reshape(1, LANES), batch, seq)

    q_att, k_att, v_att, q_idx, k_idx_t, w_idx = _prep(
        proj, small, _rope_table(seq), q_norm_g.reshape(1, LANES), k_norm_g.reshape(1, LANES), batch, seq)

    ii = np.arange(KT)
    ustrict = jnp.asarray((ii[:, None] < ii[None, :]).astype(np.float32), dtype=BF16)
    bias = _select(q_idx, w_idx, k_idx_t, ustrict, batch, seq, k_sel)
    logit_bound = (1.01 * ATT_HEAD * ATT_HEAD ** -0.5) * jnp.max(jnp.abs(q_norm_g)) * jnp.max(jnp.abs(k_norm_g))
    o_att = _attention(q_att, k_att, v_att, bias, logit_bound.reshape(1).astype(F32), batch, seq)

    return _merge_mlp(proj, o_gdn, o_att, x2, w_out.astype(BF16), norm_mlp_g.reshape(1, D_MODEL),
                      w_mlp_up.astype(BF16), w_mlp_down.astype(BF16))


def kernel(x, norm_mix_g, w_in, conv_w, a_log, dt_bias, gdn_norm_g, q_norm_g, k_norm_g, w_out,
           norm_mlp_g, w_mlp_up, w_mlp_down):
    batch, seq, _ = x.shape
    x2 = x.reshape(batch * seq, D_MODEL)
    for layer in range(norm_mix_g.shape[0]):
        x2 = _layer(x2, batch, seq, norm_mix_g[layer], w_in[layer], conv_w[layer], a_log[layer],
                    dt_bias[layer], gdn_norm_g[layer], q_norm_g[layer], k_norm_g[layer], w_out[layer],
                    norm_mlp_g[layer], w_mlp_up[layer], w_mlp_down[layer])
    return x2.reshape(batch, seq, D_MODEL)
```

```python
import functools
import math

import numpy as np
import jax
import jax.numpy as jnp
from jax import lax
from jax.experimental import pallas as pl
from jax.experimental.pallas import tpu as pltpu

F32 = jnp.float32
BF16 = jnp.bfloat16
I32 = jnp.int32

D_MODEL = 1024
GDN_HEADS = 8
GDN_HEAD = 128
GDN_CONV = 4
GDN_CHUNK = 64
GDN_CONV_DIM = 3 * GDN_HEADS * GDN_HEAD
ATT_HEADS = 8
ATT_HEAD = 128
ATT_KV_HEADS = 2
ATT_GROUP = ATT_HEADS // ATT_KV_HEADS
IDX_HEADS = 8
IDX_DIM = 64
TOPK_MAX = 256
ROPE_THETA = 500000.0
ROPE_FRACTION = 4
D_FF = 4 * D_MODEL
EPS = 1e-6

IN_SPLITS = (GDN_CONV_DIM, D_MODEL, GDN_HEADS, GDN_HEADS, D_MODEL, 256, 256,
             IDX_HEADS * IDX_DIM, IDX_DIM, IDX_HEADS, D_MODEL, D_MODEL)

LANES = 128
assert 2 * GDN_CHUNK == LANES
COL_QKV = 0
COL_Z = 24
COL_AQ = 32
COL_AKV = 40
COL_IQ = 44
COL_GA = 48
COL_GB = 56
SM_GA = 0
SM_GB = 8
SM_IK = 16
SM_IW = 80
TAB_SIN_A = 2 * (ATT_HEAD // ROPE_FRACTION // 2)
TAB_COS_I = 4 * (ATT_HEAD // ROPE_FRACTION // 2)
TAB_SIN_I = TAB_COS_I + 2 * (IDX_DIM // ROPE_FRACTION // 2)

NEG = -1e30
FLT_MAX = 3.4028234663852886e38
MAX_SEARCH_STEPS = 254 + 24 + 8
LOG2E = 1.4426950408889634
SAFE_LOGIT_BOUND = 60.0

TM_IN = 1024
TN_IN = 16 * LANES
GDN_BLK = 512
GDN_CHUNKS_PER_STEP = 8
TM_PREP = 512
QB = 128
SEL_BLOCKS = 1
QA = 256
KT = 512
assert KT % QB == 0
TM_MLP = 1024
TF_MLP = 512
VMEM_LIMIT = 56 * 2 ** 20


def _cparams(sem):
    return pltpu.CompilerParams(dimension_semantics=sem, vmem_limit_bytes=VMEM_LIMIT)


def _loop_groups(n, body, init, group=2):
    shift = group.bit_length() - 1

    def many(i, carry):
        for s in range(group):
            carry = body(group * i + s, carry)
        return carry

    n_groups = lax.shift_right_logical(n, shift)
    carry = lax.fori_loop(0, n_groups, many, init)
    return lax.fori_loop(lax.shift_left(n_groups, shift), n, body, carry)


def _inproj_kernel(x_ref, g_ref, w_ref, o_ref):
    x = x_ref[...]
    ms = jnp.mean(x * x, axis=-1, keepdims=True)
    h = x * lax.rsqrt(ms + EPS) * g_ref[...]
    o_ref[...] = jnp.dot(h.astype(BF16), w_ref[...], preferred_element_type=F32).astype(o_ref.dtype)


def _inproj(x2, g_row, w_p, tn, out_dtype, name):
    t = x2.shape[0]
    n_cols = w_p.shape[1]
    return pl.pallas_call(
        _inproj_kernel,
        grid=(n_cols // tn, t // TM_IN),
        in_specs=[pl.BlockSpec((TM_IN, D_MODEL), lambda n, m: (m, 0)),
                  pl.BlockSpec((1, D_MODEL), lambda n, m: (0, 0)),
                  pl.BlockSpec((D_MODEL, tn), lambda n, m: (0, n))],
        out_specs=pl.BlockSpec((TM_IN, tn), lambda n, m: (m, n)),
        out_shape=jax.ShapeDtypeStruct((t, n_cols), out_dtype),
        compiler_params=_cparams(("arbitrary", "arbitrary")),
        name=name,
    )(x2, g_row, w_p)


def _gdn_kernel(qkv_ref, z_ref, sm_ref, cw_ref, alog_ref, dtb_ref, gn_ref, o_ref,
                halo_ref, state_ref, qn_ref, kn_ref, vv_ref, gc_ref, bt_ref):
    blk = qkv_ref.shape[0]
    c = GDN_CHUNK

    @pl.when(pl.program_id(1) == 0)
    def _():
        halo_ref[...] = jnp.zeros_like(halo_ref)
        state_ref[...] = jnp.zeros_like(state_ref)

    row8 = lax.broadcasted_iota(I32, (8, LANES), 0)
    for j in range(GDN_CONV_DIM // LANES):
        lo = j * LANES
        u = qkv_ref[:, lo:lo + LANES].astype(F32)
        hal = halo_ref[:, lo:lo + LANES]
        w = cw_ref[:, lo:lo + LANES]
        y = u * w[GDN_CONV - 1:GDN_CONV, :]
        for sft in range(1, GDN_CONV):
            ur = pltpu.roll(u, sft, axis=0)
            hr = pltpu.roll(hal, sft, axis=0)
            top = jnp.where(row8 < sft, hr, ur[0:8, :])
            ush = jnp.concatenate([top, ur[8:, :]], axis=0)
            y = y + ush * w[GDN_CONV - 1 - sft:GDN_CONV - sft, :]
        halo_ref[:, lo:lo + LANES] = u[blk - 8:, :]
        a = y * jax.nn.sigmoid(y)
        head = j % GDN_HEADS
        if j < 2 * GDN_HEADS:
            a = a * lax.rsqrt(jnp.sum(a * a, axis=-1, keepdims=True) + EPS)
        if j < GDN_HEADS:
            qn_ref[:, head * LANES:(head + 1) * LANES] = a * (GDN_HEAD ** -0.5)
        elif j < 2 * GDN_HEADS:
            kn_ref[:, head * LANES:(head + 1) * LANES] = a
        else:
            vv_ref[:, head * LANES:(head + 1) * LANES] = a

    sm = sm_ref[...]
    xg = sm + dtb_ref[...]
    softplus = jnp.maximum(xg, 0.0) + jnp.log1p(jnp.exp(-jnp.abs(xg)))
    gc = -jnp.exp(alog_ref[...]) * softplus
    row_in_chunk = lax.broadcasted_iota(I32, (blk, LANES), 0) & (c - 1)
    d = 1
    while d < c:
        gc = gc + jnp.where(row_in_chunk >= d, pltpu.roll(gc, d, axis=0), 0.0)
        d *= 2
    gc_ref[...] = gc
    bt_ref[...] = jax.nn.sigmoid(sm)

    ii = lax.broadcasted_iota(I32, (c, 2 * c), 0)
    lane_cc = lax.broadcasted_iota(I32, (c, 2 * c), 1)
    jj = lane_cc & (c - 1)
    incl = ii >= jj
    strict = ii > jj
    left = lane_cc < c
    left_row = lax.broadcasted_iota(I32, (1, 2 * c), 1) < c
    gn = gn_ref[...]

    def chunk_body(ci, carry):
        nch = GDN_CHUNKS_PER_STEP
        r0 = [pl.multiple_of((ci * nch + s) * c, c) for s in range(nch)]
        items = [(s, h) for s in range(nch) for h in range(GDN_HEADS)]
        n_items = len(items)
        pairs = range(n_items // 2)
        hs = [slice(h * LANES, (h + 1) * LANES) for h in range(GDN_HEADS)]
        half = lambda i: slice((i % 2) * LANES, (i % 2 + 1) * LANES)
        mm = lambda a, b: jnp.dot(a.astype(BF16), b.astype(BF16), preferred_element_type=F32)

        def bdiag(xa, xb):
            return jnp.concatenate([jnp.concatenate([xa, jnp.zeros_like(xb)], axis=1),
                                    jnp.concatenate([jnp.zeros_like(xa), xb], axis=1)], axis=0)

        def bdiag2(x2):
            return jnp.concatenate([jnp.where(left, x2, 0.0), jnp.where(left, 0.0, x2)], axis=0)

        def side(xa, xb, axis):
            return jnp.concatenate([xa, xb], axis=axis)

        gcc = [gc_ref[pl.ds(r0[s], c), :] for s in range(nch)]
        btc = [bt_ref[pl.ds(r0[s], c), :] for s in range(nch)]
        gct2 = [jnp.concatenate([g, g], axis=0).T for g in gcc]
        gcol = [gcc[s][:, SM_GA + h:SM_GA + h + 1] for s, h in items]
        glast = [gcc[s][c - 1:c, SM_GA + h:SM_GA + h + 1] for s, h in items]
        bcol = [btc[s][:, SM_GB + h:SM_GB + h + 1] for s, h in items]
        q = [qn_ref[pl.ds(r0[s], c), hs[h]] for s, h in items]
        k = [kn_ref[pl.ds(r0[s], c), hs[h]] for s, h in items]
        v = [vv_ref[pl.ds(r0[s], c), hs[h]] for s, h in items]
        kb = [k[i] * bcol[i] for i in range(n_items)]
        eg = [jnp.exp(gcol[i]) for i in range(n_items)]
        rhs = [jnp.concatenate([v[i] * bcol[i], kb[i] * eg[i]], axis=1) for i in range(n_items)]
        ab = [lax.dot_general(
            side(side(kb[2 * p], q[2 * p], 0), side(kb[2 * p + 1], q[2 * p + 1], 0), 1).astype(BF16),
            bdiag(k[2 * p], k[2 * p + 1]).astype(BF16),
            (((1,), (1,)), ((), ())), preferred_element_type=F32) for p in pairs]
        decay = []
        for p in pairs:
            s, h = items[2 * p]
            ha, hb = SM_GA + h, SM_GA + h + 1
            diff = (jnp.where(left, gcol[2 * p], gcol[2 * p + 1])
                    - jnp.where(left_row, gct2[s][ha:ha + 1, :], gct2[s][hb:hb + 1, :]))
            decay.append(jnp.where(incl, jnp.exp(jnp.where(incl, diff, 0.0)), 0.0))
        qk = [jnp.where(incl, ab[p][c:, :] * decay[p], 0.0) for p in pairs]
        pw = [-jnp.where(strict, ab[p][0:c, :] * decay[p], 0.0) for p in pairs]
        r = list(pw)
        for _ in range(int(math.log2(c)) - 1):
            pw = [mm(pw[p], bdiag2(pw[p])) for p in pairs]
            rp = [mm(r[p], bdiag2(pw[p])) for p in pairs]
            r = [r[p] + pw[p] + rp[p] for p in pairs]
        rr = [mm(r[p], bdiag(rhs[2 * p], rhs[2 * p + 1])) for p in pairs]
        sol = [rhs[i] + rr[i // 2][:, (i % 2) * 2 * LANES:(i % 2 + 1) * 2 * LANES] for i in range(n_items)]
        q_dec = [q[i] * eg[i] for i in range(n_items)]
        k_dec_t = [(k[i] * jnp.exp(glast[i] - gcol[i])).T for i in range(n_items)]
        st = [state_ref[h] for h in range(GDN_HEADS)]
        for s in range(nch):
            base = s * GDN_HEADS
            hp = range(GDN_HEADS // 2)
            ws = [mm(side(side(sol[base + 2 * p][:, LANES:], q_dec[base + 2 * p], 0),
                          side(sol[base + 2 * p + 1][:, LANES:], q_dec[base + 2 * p + 1], 0), 1),
                     bdiag(st[2 * p], st[2 * p + 1])) for p in hp]
            v_new = [sol[base + h][:, :LANES] - ws[h // 2][0:c, half(h)] for h in range(GDN_HEADS)]
            qv = [mm(qk[base // 2 + p], bdiag(v_new[2 * p], v_new[2 * p + 1])) for p in hp]
            o_n = [ws[h // 2][c:, half(h)] + qv[h // 2][:, half(h)] for h in range(GDN_HEADS)]
            st = [st[h] * jnp.exp(glast[base + h]) + mm(k_dec_t[base + h], v_new[h]) for h in range(GDN_HEADS)]
            for h in range(GDN_HEADS):
                on = o_n[h] * lax.rsqrt(jnp.mean(o_n[h] * o_n[h], axis=-1, keepdims=True) + EPS) * gn
                zz = z_ref[pl.ds(r0[s], c), hs[h]].astype(F32)
                o_ref[pl.ds(r0[s], c), hs[h]] = (on * (zz * jax.nn.sigmoid(zz))).astype(o_ref.dtype)
        for h in range(GDN_HEADS):
            state_ref[h] = st[h]
        return carry

    lax.fori_loop(0, blk // (c * GDN_CHUNKS_PER_STEP), chunk_body, 0)


def _gdn(proj, small, conv_w, alog_row, dtb_row, gn_row, batch, seq):
    t = proj.shape[0]
    blk = min(GDN_BLK, seq)
    ns = seq // blk
    row = lambda b, s: b * ns + s
    return pl.pallas_call(
        _gdn_kernel,
        grid=(batch, ns),
        in_specs=[pl.BlockSpec((blk, GDN_CONV_DIM), lambda b, s: (row(b, s), COL_QKV * LANES // GDN_CONV_DIM)),
                  pl.BlockSpec((blk, D_MODEL), lambda b, s: (row(b, s), COL_Z * LANES // D_MODEL)),
                  pl.BlockSpec((blk, LANES), lambda b, s: (row(b, s), 0)),
                  pl.BlockSpec((GDN_CONV, GDN_CONV_DIM), lambda b, s: (0, 0)),
                  pl.BlockSpec((1, LANES), lambda b, s: (0, 0)),
                  pl.BlockSpec((1, LANES), lambda b, s: (0, 0)),
                  pl.BlockSpec((1, LANES), lambda b, s: (0, 0))],
        out_specs=pl.BlockSpec((blk, D_MODEL), lambda b, s: (row(b, s), 0)),
        out_shape=jax.ShapeDtypeStruct((t, D_MODEL), BF16),
        scratch_shapes=[pltpu.VMEM((8, GDN_CONV_DIM), F32),
                        pltpu.VMEM((GDN_HEADS, GDN_HEAD, GDN_HEAD), F32),
                        pltpu.VMEM((blk, D_MODEL), F32),
                        pltpu.VMEM((blk, D_MODEL), F32),
                        pltpu.VMEM((blk, D_MODEL), F32),
                        pltpu.VMEM((blk, LANES), F32),
                        pltpu.VMEM((blk, LANES), F32)],
        compiler_params=_cparams(("arbitrary", "arbitrary")),
        name="gdn",
    )(proj, proj, small, conv_w, alog_row, dtb_row, gn_row)


def _rope(y, cos, sin_lo, sin_hi, half):
    return y * cos + pltpu.roll(y, LANES - half, axis=1) * sin_lo + pltpu.roll(y, half, axis=1) * sin_hi


def _prep_kernel(aq_ref, akv_ref, iq_ref, sm_ref, tab_ref, gq_ref, gk_ref,
                 q_out, k_out, v_out, qi_out, kit_out, w_out):
    half_a = ATT_HEAD // ROPE_FRACTION // 2
    half_i = IDX_DIM // ROPE_FRACTION // 2
    t = tab_ref[...]
    lane = lax.broadcasted_iota(I32, t.shape, 1)
    from_lane = lambda src: pltpu.roll(t, (-src) % LANES, axis=1)
    between = lambda lo, hi: (lane >= lo) & (lane < hi)
    ca = jnp.where(lane < 2 * half_a, t, 1.0)
    sa = from_lane(TAB_SIN_A)
    sa1 = jnp.where(lane < half_a, sa, 0.0)
    sa2 = jnp.where(between(half_a, 2 * half_a), sa, 0.0)
    lm = lane & (IDX_DIM - 1)
    low = lane < IDX_DIM
    ci = jnp.where(lm < 2 * half_i, jnp.where(low, from_lane(TAB_COS_I), from_lane(TAB_COS_I - IDX_DIM)), 1.0)
    si = jnp.where(low, from_lane(TAB_SIN_I), from_lane(TAB_SIN_I - IDX_DIM))
    si1 = jnp.where(lm < half_i, si, 0.0)
    si2 = jnp.where((lm >= half_i) & (lm < 2 * half_i), si, 0.0)
    ck = jnp.where(between(SM_IK, SM_IK + 2 * half_i), from_lane(TAB_COS_I - SM_IK), 1.0)
    sk = from_lane(TAB_SIN_I - SM_IK)
    sk1 = jnp.where(between(SM_IK, SM_IK + half_i), sk, 0.0)
    sk2 = jnp.where(between(SM_IK + half_i, SM_IK + 2 * half_i), sk, 0.0)
    gq = gq_ref[...]
    gk = gk_ref[...]
    for h in range(ATT_HEADS):
        hs = slice(h * LANES, (h + 1) * LANES)
        xh = aq_ref[:, hs].astype(F32)
        y = xh * lax.rsqrt(jnp.mean(xh * xh, axis=-1, keepdims=True) + EPS) * gq
        q_out[:, hs] = _rope(y, ca, sa1, sa2, half_a).astype(BF16)
    for cc in range(ATT_KV_HEADS):
        hs = slice(cc * LANES, (cc + 1) * LANES)
        xk = akv_ref[:, hs].astype(F32)
        y = xk * lax.rsqrt(jnp.mean(xk * xk, axis=-1, keepdims=True) + EPS) * gk
        k_out[:, hs] = _rope(y, ca, sa1, sa2, half_a).astype(BF16)
        v_out[:, hs] = akv_ref[:, ATT_KV_HEADS * LANES + cc * LANES:ATT_KV_HEADS * LANES + (cc + 1) * LANES].astype(BF16)
    for j in range(IDX_HEADS * IDX_DIM // LANES):
        hs = slice(j * LANES, (j + 1) * LANES)
        qi_out[:, hs] = _rope(iq_ref[:, hs].astype(F32), ci, si1, si2, half_i).astype(BF16)
    sm = sm_ref[...]
    smr = _rope(sm, ck, sk1, sk2, half_i)
    kit_out[...] = smr.T[SM_IK:SM_IK + IDX_DIM, :].astype(BF16)
    w_out[...] = sm * (IDX_HEADS ** -0.5 * IDX_DIM ** -0.5)


def _prep(proj, small, table, gq_row, gk_row, batch, seq):
    t = proj.shape[0]
    tm = min(TM_PREP, seq)
    nsb = seq // tm
    tab_spec = pl.BlockSpec((tm, LANES), lambda i: (i % nsb, 0))
    row_spec = pl.BlockSpec((1, LANES), lambda i: (0, 0))
    return pl.pallas_call(
        _prep_kernel,
        grid=(t // tm,),
        in_specs=[pl.BlockSpec((tm, D_MODEL), lambda i: (i, COL_AQ * LANES // D_MODEL)),
                  pl.BlockSpec((tm, 4 * LANES), lambda i: (i, COL_AKV // 4)),
                  pl.BlockSpec((tm, 4 * LANES), lambda i: (i, COL_IQ // 4)),
                  pl.BlockSpec((tm, LANES), lambda i: (i, 0))]
                 + [tab_spec, row_spec, row_spec],
        out_specs=[pl.BlockSpec((tm, D_MODEL), lambda i: (i, 0)),
                   pl.BlockSpec((tm, ATT_KV_HEADS * LANES), lambda i: (i, 0)),
                   pl.BlockSpec((tm, ATT_KV_HEADS * LANES), lambda i: (i, 0)),
                   pl.BlockSpec((tm, IDX_HEADS * IDX_DIM), lambda i: (i, 0)),
                   pl.BlockSpec((None, None, IDX_DIM, tm), lambda i: (i // nsb, i % nsb, 0, 0)),
                   pl.BlockSpec((tm, LANES), lambda i: (i, 0))],
        out_shape=[jax.ShapeDtypeStruct((t, D_MODEL), BF16),
                   jax.ShapeDtypeStruct((t, ATT_KV_HEADS * LANES), BF16),
                   jax.ShapeDtypeStruct((t, ATT_KV_HEADS * LANES), BF16),
                   jax.ShapeDtypeStruct((t, IDX_HEADS * IDX_DIM), BF16),
                   jax.ShapeDtypeStruct((batch, nsb, IDX_DIM, tm), BF16),
                   jax.ShapeDtypeStruct((t, LANES), F32)],
        compiler_params=_cparams(("arbitrary",)),
        name="prep",
    )(proj, proj, proj, small, table, gq_row, gk_row)


def _select_kernel(qi_ref, w_ref, kit_ref, ustrict_ref, bias_ref, keys_ref, qh_ref, *, k_sel):
    nt_all = keys_ref.shape[1]
    n_groups = KT // LANES
    lane = lax.broadcasted_iota(I32, (QB, LANES), 1)
    to_key = lambda x: (lambda b: b ^ ((b >> 31) & 0x7FFFFFFF))(pltpu.bitcast(x, I32))
    to_val = lambda kx: pltpu.bitcast(kx ^ ((kx >> 31) & 0x7FFFFFFF), F32)

    def count(sb, nt, cand, strict):
        cb = jnp.broadcast_to(cand, (QB, LANES))

        def tile(j, cnt):
            kk = keys_ref[sb, j]
            for g in range(n_groups):
                kg = kk[:, g * LANES:(g + 1) * LANES]
                cnt = cnt + jnp.where(kg > cb if strict else kg >= cb, 1.0, 0.0)
            return cnt

        cnt = lax.fori_loop(0, nt, tile, jnp.zeros((QB, LANES), F32))
        return jnp.sum(cnt, axis=1, keepdims=True)

    def score_block(sb):
        qblk = pl.program_id(1) * SEL_BLOCKS + sb
        nt = (qblk * QB + QB + KT - 1) // KT
        rows = slice(sb * QB, (sb + 1) * QB)
        for h in range(IDX_HEADS):
            qh_ref[h * QB:(h + 1) * QB, :] = qi_ref[rows, h * IDX_DIM:(h + 1) * IDX_DIM]
        wb = [jnp.broadcast_to(w_ref[rows, SM_IW + h:SM_IW + h + 1], (QB, LANES)) for h in range(IDX_HEADS)]
        row_g = qblk * QB + lax.broadcasted_iota(I32, (QB, LANES), 0)

        def score_tile(j, carry, diagonal):
            mx, mn, ge0, gt0 = carry
            lg = jnp.dot(qh_ref[...], kit_ref[j], preferred_element_type=F32)
            acc = [jnp.zeros((QB, LANES), F32) for _ in range(n_groups)]
            for h in range(IDX_HEADS):
                for g in range(n_groups):
                    acc[g] = acc[g] + jnp.maximum(lg[h * QB:(h + 1) * QB, g * LANES:(g + 1) * LANES], 0.0) * wb[h]
            for g in range(n_groups):
                if diagonal:
                    causal = (j * KT + g * LANES + lane) <= row_g
                    sc = jnp.where(causal, acc[g], -jnp.inf)
                    mn = jnp.minimum(mn, jnp.where(causal, acc[g], jnp.inf))
                else:
                    sc = acc[g]
                    mn = jnp.minimum(mn, sc)
                keys_ref[sb, j, :, g * LANES:(g + 1) * LANES] = sc
                mx = jnp.maximum(mx, sc)
                ge0 = ge0 + jnp.where(sc >= 0.0, 1.0, 0.0)
                gt0 = gt0 + jnp.where(sc > 0.0, 1.0, 0.0)
            return mx, mn, ge0, gt0

        zeros = jnp.zeros((QB, LANES), F32)
        carry = _loop_groups(nt - 1, functools.partial(score_tile, diagonal=False),
                             (jnp.full((QB, LANES), -jnp.inf, F32), jnp.full((QB, LANES), jnp.inf, F32),
                              zeros, zeros), 4)
        mx, mn, ge0, gt0 = score_tile(nt - 1, carry, diagonal=True)
        row_max = jnp.max(mx, axis=1, keepdims=True)
        row_min = jnp.min(mn, axis=1, keepdims=True)
        n_ge0 = jnp.sum(ge0, axis=1, keepdims=True)
        n_gt0 = jnp.sum(gt0, axis=1, keepdims=True)
        n_causal = (row_g[:, 0:1] + 1).astype(F32)
        k_eff = jnp.minimum(float(k_sel), n_causal)
        take_all = n_causal <= k_eff
        above = n_gt0 > k_eff
        below = n_ge0 < k_eff
        lo0 = jnp.where(below, to_val(to_key(row_min) - 1), 0.0)
        hi0 = jnp.where(above, to_val(to_key(row_max) + 1), 0.0)
        lo0 = jnp.where(take_all, -FLT_MAX, lo0)
        hi0 = jnp.where(take_all, -FLT_MAX, hi0)
        at_zero = jnp.logical_not(jnp.logical_or(jnp.logical_or(above, below), take_all))
        zero_tied = jnp.logical_and(at_zero, n_ge0 > k_eff)
        return dict(nt=nt, k_eff=k_eff, lo0=lo0, hi0=hi0, zero_tied=zero_tied)

    blocks = [score_block(sb) for sb in range(SEL_BLOCKS)]

    def unsettled(los, his):
        flags = []
        for lo, hi in zip(los, his):
            mid = 0.5 * lo + 0.5 * hi
            settled = jnp.logical_or(lo == hi, jnp.logical_or(mid == lo, mid == hi))
            flags.append(jnp.where(settled, 0.0, 1.0))
        return jnp.max(functools.reduce(jnp.maximum, flags))

    def search_cond(st):
        return jnp.logical_and(st[0] < MAX_SEARCH_STEPS, st[1] > 0.0)

    def search_body(st):
        it, _, los, his = st
        mids = [0.5 * lo + 0.5 * hi for lo, hi in zip(los, his)]
        ns = [count(sb, blocks[sb]["nt"], mids[sb], False) for sb in range(SEL_BLOCKS)]
        los = tuple(jnp.where(ns[sb] >= blocks[sb]["k_eff"], mids[sb], los[sb]) for sb in range(SEL_BLOCKS))
        his = tuple(jnp.where(ns[sb] <= blocks[sb]["k_eff"], mids[sb], his[sb]) for sb in range(SEL_BLOCKS))
        return it + 1, unsettled(los, his), los, his

    los0 = tuple(b["lo0"] for b in blocks)
    his0 = tuple(b["hi0"] for b in blocks)
    _, _, los, his = lax.while_loop(search_cond, search_body, (jnp.int32(0), unsettled(los0, his0), los0, his0))

    def finish_block(sb):
        nt, k_eff = blocks[sb]["nt"], blocks[sb]["k_eff"]
        vstar = los[sb]
        vsb = jnp.broadcast_to(vstar, (QB, KT))
        tied = jnp.logical_or(vstar != his[sb], blocks[sb]["zero_tied"])
        any_tied = jnp.max(jnp.where(tied, 1.0, 0.0)) > 0.0

        @pl.when(any_tied)
        def _():
            need = k_eff - count(sb, nt, vstar, True)
            ustrict = ustrict_ref[...]

            def final_tile(j, carry):
                kk = keys_ref[sb, j]
                tie = kk == vsb
                tie_b = jnp.where(tie, 1.0, 0.0).astype(BF16)
                rank = jnp.dot(tie_b, ustrict, preferred_element_type=F32) + carry
                keep_tie = jnp.where(rank < need, 0.0, NEG)
                bias = jnp.where(kk > vsb, 0.0, jnp.where(tie, keep_tie, NEG))
                bias_ref[sb, j] = bias.astype(BF16)
                return carry + jnp.sum(jnp.where(tie, 1.0, 0.0), axis=1, keepdims=True)

            _loop_groups(nt, final_tile, jnp.zeros((QB, 1), F32))

        @pl.when(jnp.logical_not(any_tied))
        def _():
            def final_tile(j, carry):
                bias_ref[sb, j] = jnp.where(keys_ref[sb, j] >= vsb, 0.0, NEG).astype(BF16)
                return carry

            lax.fori_loop(0, nt, final_tile, 0)

        def fill_tile(j, carry):
            bias_ref[sb, j] = jnp.full((QB, KT), NEG, BF16)
            return carry

        lax.fori_loop(nt, nt_all, fill_tile, 0)

    for sb in range(SEL_BLOCKS):
        finish_block(sb)


def _select(qi, wsm, kit, ustrict, batch, seq, k_sel):
    nq = seq // (QB * SEL_BLOCKS)
    nt = seq // KT
    rows = QB * SEL_BLOCKS
    return pl.pallas_call(
        functools.partial(_select_kernel, k_sel=k_sel),
        grid=(batch, nq),
        in_specs=[pl.BlockSpec((rows, IDX_HEADS * IDX_DIM), lambda b, q: (b * nq + q, 0)),
                  pl.BlockSpec((rows, LANES), lambda b, q: (b * nq + q, 0)),
                  pl.BlockSpec((None, nt, IDX_DIM, KT), lambda b, q: (b, 0, 0, 0)),
                  pl.BlockSpec((KT, KT), lambda b, q: (0, 0))],
        out_specs=pl.BlockSpec((None, SEL_BLOCKS, nt, QB, KT), lambda b, q: (b, q, 0, 0, 0)),
        out_shape=jax.ShapeDtypeStruct((batch, nq * SEL_BLOCKS, nt, QB, KT), BF16),
        scratch_shapes=[pltpu.VMEM((SEL_BLOCKS, nt, QB, KT), F32),
                        pltpu.VMEM((IDX_HEADS * QB, IDX_DIM), BF16)],
        compiler_params=_cparams(("arbitrary", "arbitrary")),
        name="select",
    )(qi, wsm, kit, ustrict)


def _attn_kernel(q_ref, k_ref, v_ref, bias_ref, bound_ref, o_ref, m_ref, l_ref, acc_ref, accw_ref):
    qblk = pl.program_id(1)
    nt = (qblk * QA + QA + KT - 1) // KT
    scale = ATT_HEAD ** -0.5
    q4 = [jnp.concatenate([q_ref[:, (cc * ATT_GROUP + g) * LANES:(cc * ATT_GROUP + g + 1) * LANES]
                           for g in range(ATT_GROUP)], axis=0) for cc in range(ATT_KV_HEADS)]

    def bias_rows(j):
        b = jnp.concatenate([bias_ref[r, j] for r in range(QA // QB)], axis=0).astype(F32)
        return jnp.concatenate([b] * ATT_GROUP, axis=0)

    def write_out(cc, o):
        for g in range(ATT_GROUP):
            h = cc * ATT_GROUP + g
            o_ref[:, h * LANES:(h + 1) * LANES] = o[g * QA:(g + 1) * QA, :].astype(o_ref.dtype)

    safe = bound_ref[0] <= SAFE_LOGIT_BOUND

    @pl.when(safe)
    def _():
        accw_ref[...] = jnp.zeros_like(accw_ref)
        ones = jnp.ones((KT, LANES), BF16)

        def kv_tile(j, carry):
            r0 = pl.multiple_of(j * KT, KT)
            bias4 = bias_rows(j)
            for cc in range(ATT_KV_HEADS):
                kc = k_ref[pl.ds(r0, KT), cc * LANES:(cc + 1) * LANES]
                vc = v_ref[pl.ds(r0, KT), cc * LANES:(cc + 1) * LANES]
                s = lax.dot_general(q4[cc], kc, (((1,), (1,)), ((), ())), preferred_element_type=F32)
                p = jnp.exp2(s * (scale * LOG2E) + bias4)
                accw_ref[cc] += jnp.dot(p.astype(BF16), jnp.concatenate([vc, ones], axis=1),
                                        preferred_element_type=F32)
            return carry

        _loop_groups(nt, kv_tile, 0)
        for cc in range(ATT_KV_HEADS):
            aw = accw_ref[cc]
            write_out(cc, aw[:, :LANES] / aw[:, LANES:])

    @pl.when(jnp.logical_not(safe))
    def _():
        m_ref[...] = jnp.full_like(m_ref, NEG)
        l_ref[...] = jnp.zeros_like(l_ref)
        acc_ref[...] = jnp.zeros_like(acc_ref)

        def kv_tile(j, carry):
            r0 = pl.multiple_of(j * KT, KT)
            bias4 = bias_rows(j)
            for cc in range(ATT_KV_HEADS):
                kc = k_ref[pl.ds(r0, KT), cc * LANES:(cc + 1) * LANES]
                vc = v_ref[pl.ds(r0, KT), cc * LANES:(cc + 1) * LANES]
                s = lax.dot_general(q4[cc], kc, (((1,), (1,)), ((), ())), preferred_element_type=F32)
                s = s * scale + bias4
                m_prev = m_ref[cc]
                m_new = jnp.maximum(m_prev, jnp.max(s, axis=1, keepdims=True))
                alpha = jnp.exp(m_prev - m_new)
                p = jnp.exp(s - m_new)
                l_ref[cc] = alpha * l_ref[cc] + jnp.sum(p, axis=1, keepdims=True)
                acc_ref[cc] = alpha * acc_ref[cc] + jnp.dot(p.astype(BF16), vc, preferred_element_type=F32)
                m_ref[cc] = m_new
            return carry

        lax.fori_loop(0, nt, kv_tile, 0)
        for cc in range(ATT_KV_HEADS):
            write_out(cc, acc_ref[cc] / l_ref[cc])


def _attention(q, k, v, bias, logit_bound, batch, seq):
    t = q.shape[0]
    nq = seq // QA
    nt = seq // KT
    rows = ATT_GROUP * QA
    return pl.pallas_call(
        _attn_kernel,
        grid=(batch, nq),
        in_specs=[pl.BlockSpec((QA, D_MODEL), lambda b, i: (b * nq + i, 0)),
                  pl.BlockSpec((seq, ATT_KV_HEADS * LANES), lambda b, i: (b, 0)),
                  pl.BlockSpec((seq, ATT_KV_HEADS * LANES), lambda b, i: (b, 0)),
                  pl.BlockSpec((None, QA // QB, nt, QB, KT), lambda b, i: (b, i, 0, 0, 0)),
                  pl.BlockSpec(memory_space=pltpu.SMEM)],
        out_specs=pl.BlockSpec((QA, D_MODEL), lambda b, i: (b * nq + i, 0)),
        out_shape=jax.ShapeDtypeStruct((t, D_MODEL), BF16),
        scratch_shapes=[pltpu.VMEM((ATT_KV_HEADS, rows, 1), F32),
                        pltpu.VMEM((ATT_KV_HEADS, rows, 1), F32),
                        pltpu.VMEM((ATT_KV_HEADS, rows, LANES), F32),
                        pltpu.VMEM((ATT_KV_HEADS, rows, 2 * LANES), F32)],
        compiler_params=_cparams(("arbitrary", "arbitrary")),
        name="attention",
    )(q, k, v, bias, logit_bound)


def _mlp_kernel(ga_ref, gb_ref, og_ref, oa_ref, x_ref, wo_ref, g_ref, wu_ref, wd_ref, o_ref, h_ref, acc_ref):
    f = pl.program_id(1)

    @pl.when(f == 0)
    def _():
        merged = (jax.nn.sigmoid(ga_ref[...].astype(F32)) * og_ref[...].astype(F32)
                  + jax.nn.sigmoid(gb_ref[...].astype(F32)) * oa_ref[...].astype(F32))
        x = x_ref[...] + jnp.dot(merged.astype(BF16), wo_ref[...], preferred_element_type=F32)
        ms = jnp.mean(x * x, axis=-1, keepdims=True)
        h_ref[...] = (x * lax.rsqrt(ms + EPS) * g_ref[...]).astype(BF16)
        acc_ref[...] = x

    up = jnp.dot(h_ref[...], wu_ref[...], preferred_element_type=F32)
    act = jnp.square(jnp.maximum(up, 0.0))
    acc_ref[...] += jnp.dot(act.astype(BF16), wd_ref[...], preferred_element_type=F32)

    @pl.when(f == pl.num_programs(1) - 1)
    def _():
        o_ref[...] = acc_ref[...]


def _merge_mlp(proj, o_gdn, o_att, x2, w_out, g_row, w_up, w_down):
    t = x2.shape[0]
    tm = min(TM_MLP, t)
    blk = lambda c: pl.BlockSpec((tm, D_MODEL), lambda i, f: (i, c))
    return pl.pallas_call(
        _mlp_kernel,
        grid=(t // tm, D_FF // TF_MLP),
        in_specs=[blk(COL_GA * LANES // D_MODEL), blk(COL_GB * LANES // D_MODEL), blk(0), blk(0), blk(0),
                  pl.BlockSpec((D_MODEL, D_MODEL), lambda i, f: (0, 0)),
                  pl.BlockSpec((1, D_MODEL), lambda i, f: (0, 0)),
                  pl.BlockSpec((D_MODEL, TF_MLP), lambda i, f: (0, f)),
                  pl.BlockSpec((TF_MLP, D_MODEL), lambda i, f: (f, 0))],
        out_specs=blk(0),
        out_shape=jax.ShapeDtypeStruct((t, D_MODEL), F32),
        scratch_shapes=[pltpu.VMEM((tm, D_MODEL), BF16), pltpu.VMEM((tm, D_MODEL), F32)],
        compiler_params=_cparams(("arbitrary", "arbitrary")),
        name="merge_mlp",
    )(proj, proj, o_gdn, o_att, x2, w_out, g_row, w_up, w_down)


def _pack_w_in(w_in):
    pts = np.cumsum(np.array(IN_SPLITS))[:-1].tolist()
    (g_qkv, g_z, g_a, g_b, a_q, a_k, a_v, i_q, i_k, i_w, gate_a, gate_b) = jnp.split(w_in, pts, axis=-1)
    small = jnp.concatenate([g_a, g_b, i_k, i_w], axis=-1)
    small = jnp.pad(small, ((0, 0), (0, LANES - small.shape[-1])))
    main = jnp.concatenate([g_qkv, g_z, a_q, a_k, a_v, i_q, gate_a, gate_b], axis=-1)
    return main.astype(BF16), small.astype(BF16)


def _lane_row(vec, offset=0):
    return jnp.zeros((1, LANES), F32).at[0, offset:offset + vec.shape[0]].set(vec.astype(F32))


def _rope_table(seq):
    tab = np.zeros((seq, LANES), np.float64)
    pos = np.arange(seq, dtype=np.float64)[:, None]
    for rot_dim, base in ((ATT_HEAD // ROPE_FRACTION, 0), (IDX_DIM // ROPE_FRACTION, TAB_COS_I)):
        half = rot_dim // 2
        ang = pos * ROPE_THETA ** (-np.arange(0, rot_dim, 2, dtype=np.float64) / rot_dim)[None, :]
        tab[:, base:base + half] = tab[:, base + half:base + 2 * half] = np.cos(ang)
        tab[:, base + 2 * half:base + 3 * half] = -np.sin(ang)
        tab[:, base + 3 * half:base + 4 * half] = np.sin(ang)
    return jnp.asarray(tab, dtype=F32)


def _layer(x2, batch, seq, norm_mix_g, w_in, conv_w, a_log, dt_bias, gdn_norm_g, q_norm_g, k_norm_g,
           w_out, norm_mlp_g, w_mlp_up, w_mlp_down):
    k_sel = min(TOPK_MAX, seq // 4)
    w_main, w_small = _pack_w_in(w_in)
    g_mix = norm_mix_g.reshape(1, D_MODEL)
    proj = _inproj(x2, g_mix, w_main, TN_IN, BF16, "inproj")
    small = _inproj(x2, g_mix, w_small, LANES, F32, "inproj_small")

    o_gdn = _gdn(proj, small, conv_w, _lane_row(a_log, SM_GA), _lane_row(dt_bias, SM_GA),
                 gdn_norm_g.reshape(1, LANES), batch, seq)

    q_att, k_att, v_att, q_idx, k_idx_t, w_idx = _prep(
        proj, small, _rope_table(seq), q_norm_g.reshape(1, LANES), k_norm_g.reshape(1, LANES), batch, seq)

    ii = np.arange(KT)
    ustrict = jnp.asarray((ii[:, None] < ii[None, :]).astype(np.float32), dtype=BF16)
    bias = _select(q_idx, w_idx, k_idx_t, ustrict, batch, seq, k_sel)
    logit_bound = (1.01 * ATT_HEAD * ATT_HEAD ** -0.5) * jnp.max(jnp.abs(q_norm_g)) * jnp.max(jnp.abs(k_norm_g))
    o_att = _attention(q_att, k_att, v_att, bias, logit_bound.reshape(1).astype(F32), batch, seq)

    return _merge_mlp(proj, o_gdn, o_att, x2, w_out.astype(BF16), norm_mlp_g.reshape(1, D_MODEL),
                      w_mlp_up.astype(BF16), w_mlp_down.astype(BF16))


def kernel(x, norm_mix_g, w_in, conv_w, a_log, dt_bias, gdn_norm_g, q_norm_g, k_norm_g, w_out,
           norm_mlp_g, w_mlp_up, w_mlp_down):
    batch, seq, _ = x.shape
    x2 = x.reshape(batch * seq, D_MODEL)
    for layer in range(norm_mix_g.shape[0]):
        x2 = _layer(x2, batch, seq, norm_mix_g[layer], w_in[layer], conv_w[layer], a_log[layer],
                    dt_bias[layer], gdn_norm_g[layer], q_norm_g[layer], k_norm_g[layer], w_out[layer],
                    norm_mlp_g[layer], w_mlp_up[layer], w_mlp_down[layer])
    return x2.reshape(batch, seq, D_MODEL)
```

```python
import functools
import math

import numpy as np
import jax
import jax.numpy as jnp
from jax import lax
from jax.experimental import pallas as pl
from jax.experimental.pallas import tpu as pltpu

F32 = jnp.float32
BF16 = jnp.bfloat16
I32 = jnp.int32

D_MODEL = 1024
GDN_HEADS = 8
GDN_HEAD = 128
GDN_CONV = 4
GDN_CHUNK = 64
GDN_CONV_DIM = 3 * GDN_HEADS * GDN_HEAD
ATT_HEADS = 8
ATT_HEAD = 128
ATT_KV_HEADS = 2
ATT_GROUP = ATT_HEADS // ATT_KV_HEADS
IDX_HEADS = 8
IDX_DIM = 64
TOPK_MAX = 256
ROPE_THETA = 500000.0
ROPE_FRACTION = 4
D_FF = 4 * D_MODEL
EPS = 1e-6

IN_SPLITS = (GDN_CONV_DIM, D_MODEL, GDN_HEADS, GDN_HEADS, D_MODEL, 256, 256,
             IDX_HEADS * IDX_DIM, IDX_DIM, IDX_HEADS, D_MODEL, D_MODEL)

LANES = 128
assert 2 * GDN_CHUNK == LANES
COL_QKV = 0
COL_Z = 24
COL_AQ = 32
COL_AKV = 40
COL_IQ = 44
COL_GA = 48
COL_GB = 56
SM_GA = 0
SM_GB = 8
SM_IK = 16
SM_IW = 80
TAB_SIN_A = 2 * (ATT_HEAD // ROPE_FRACTION // 2)
TAB_COS_I = 4 * (ATT_HEAD // ROPE_FRACTION // 2)
TAB_SIN_I = TAB_COS_I + 2 * (IDX_DIM // ROPE_FRACTION // 2)

NEG = -1e30
FLT_MAX = 3.4028234663852886e38
MAX_SEARCH_STEPS = 254 + 24 + 8
LOG2E = 1.4426950408889634
SAFE_LOGIT_BOUND = 60.0

TM_IN = 1024
TN_IN = 16 * LANES
GDN_BLK = 512
GDN_CHUNKS_PER_STEP = 8
TM_PREP = 512
QB = 128
SEL_BLOCKS = 1
QA = 256
KT = 512
assert KT % QB == 0
TM_MERGE = 512
TM_MLP = 1024
TF_MLP = 512
VMEM_LIMIT = 56 * 2 ** 20


def _cparams(sem):
    return pltpu.CompilerParams(dimension_semantics=sem, vmem_limit_bytes=VMEM_LIMIT)


def _loop_groups(n, body, init, group=2):
    shift = group.bit_length() - 1

    def many(i, carry):
        for s in range(group):
            carry = body(group * i + s, carry)
        return carry

    n_groups = lax.shift_right_logical(n, shift)
    carry = lax.fori_loop(0, n_groups, many, init)
    return lax.fori_loop(lax.shift_left(n_groups, shift), n, body, carry)


def _inproj_kernel(x_ref, g_ref, w_ref, o_ref):
    x = x_ref[...]
    ms = jnp.mean(x * x, axis=-1, keepdims=True)
    h = x * lax.rsqrt(ms + EPS) * g_ref[...]
    o_ref[...] = jnp.dot(h.astype(BF16), w_ref[...], preferred_element_type=F32).astype(o_ref.dtype)


def _inproj(x2, g_row, w_p, tn, out_dtype, name):
    t = x2.shape[0]
    n_cols = w_p.shape[1]
    return pl.pallas_call(
        _inproj_kernel,
        grid=(n_cols // tn, t // TM_IN),
        in_specs=[pl.BlockSpec((TM_IN, D_MODEL), lambda n, m: (m, 0)),
                  pl.BlockSpec((1, D_MODEL), lambda n, m: (0, 0)),
                  pl.BlockSpec((D_MODEL, tn), lambda n, m: (0, n))],
        out_specs=pl.BlockSpec((TM_IN, tn), lambda n, m: (m, n)),
        out_shape=jax.ShapeDtypeStruct((t, n_cols), out_dtype),
        compiler_params=_cparams(("arbitrary", "arbitrary")),
        name=name,
    )(x2, g_row, w_p)


def _gdn_kernel(qkv_ref, z_ref, sm_ref, cw_ref, alog_ref, dtb_ref, gn_ref, o_ref,
                halo_ref, state_ref, qn_ref, kn_ref, vv_ref, gc_ref, bt_ref):
    blk = qkv_ref.shape[0]
    c = GDN_CHUNK

    @pl.when(pl.program_id(1) == 0)
    def _():
        halo_ref[...] = jnp.zeros_like(halo_ref)
        state_ref[...] = jnp.zeros_like(state_ref)

    row8 = lax.broadcasted_iota(I32, (8, LANES), 0)
    for j in range(GDN_CONV_DIM // LANES):
        lo = j * LANES
        u = qkv_ref[:, lo:lo + LANES].astype(F32)
        hal = halo_ref[:, lo:lo + LANES]
        w = cw_ref[:, lo:lo + LANES]
        y = u * w[GDN_CONV - 1:GDN_CONV, :]
        for sft in range(1, GDN_CONV):
            ur = pltpu.roll(u, sft, axis=0)
            hr = pltpu.roll(hal, sft, axis=0)
            top = jnp.where(row8 < sft, hr, ur[0:8, :])
            ush = jnp.concatenate([top, ur[8:, :]], axis=0)
            y = y + ush * w[GDN_CONV - 1 - sft:GDN_CONV - sft, :]
        halo_ref[:, lo:lo + LANES] = u[blk - 8:, :]
        a = y * jax.nn.sigmoid(y)
        head = j % GDN_HEADS
        if j < 2 * GDN_HEADS:
            a = a * lax.rsqrt(jnp.sum(a * a, axis=-1, keepdims=True) + EPS)
        if j < GDN_HEADS:
            qn_ref[:, head * LANES:(head + 1) * LANES] = a * (GDN_HEAD ** -0.5)
        elif j < 2 * GDN_HEADS:
            kn_ref[:, head * LANES:(head + 1) * LANES] = a
        else:
            vv_ref[:, head * LANES:(head + 1) * LANES] = a

    sm = sm_ref[...]
    xg = sm + dtb_ref[...]
    softplus = jnp.maximum(xg, 0.0) + jnp.log1p(jnp.exp(-jnp.abs(xg)))
    gc = -jnp.exp(alog_ref[...]) * softplus
    row_in_chunk = lax.broadcasted_iota(I32, (blk, LANES), 0) & (c - 1)
    d = 1
    while d < c:
        gc = gc + jnp.where(row_in_chunk >= d, pltpu.roll(gc, d, axis=0), 0.0)
        d *= 2
    gc_ref[...] = gc
    bt_ref[...] = jax.nn.sigmoid(sm)

    ii = lax.broadcasted_iota(I32, (c, 2 * c), 0)
    lane_cc = lax.broadcasted_iota(I32, (c, 2 * c), 1)
    jj = lane_cc & (c - 1)
    incl = ii >= jj
    strict = ii > jj
    left = lane_cc < c
    left_row = lax.broadcasted_iota(I32, (1, 2 * c), 1) < c
    gn = gn_ref[...]

    def chunk_body(ci, carry):
        nch = GDN_CHUNKS_PER_STEP
        r0 = [pl.multiple_of((ci * nch + s) * c, c) for s in range(nch)]
        items = [(s, h) for s in range(nch) for h in range(GDN_HEADS)]
        n_items = len(items)
        pairs = range(n_items // 2)
        hs = [slice(h * LANES, (h + 1) * LANES) for h in range(GDN_HEADS)]
        half = lambda i: slice((i % 2) * LANES, (i % 2 + 1) * LANES)
        mm = lambda a, b: jnp.dot(a.astype(BF16), b.astype(BF16), preferred_element_type=F32)

        def bdiag(xa, xb):
            return jnp.concatenate([jnp.concatenate([xa, jnp.zeros_like(xb)], axis=1),
                                    jnp.concatenate([jnp.zeros_like(xa), xb], axis=1)], axis=0)

        def bdiag2(x2):
            return jnp.concatenate([jnp.where(left, x2, 0.0), jnp.where(left, 0.0, x2)], axis=0)

        def side(xa, xb, axis):
            return jnp.concatenate([xa, xb], axis=axis)

        gcc = [gc_ref[pl.ds(r0[s], c), :] for s in range(nch)]
        btc = [bt_ref[pl.ds(r0[s], c), :] for s in range(nch)]
        gct2 = [jnp.concatenate([g, g], axis=0).T for g in gcc]
        gcol = [gcc[s][:, SM_GA + h:SM_GA + h + 1] for s, h in items]
        glast = [gcc[s][c - 1:c, SM_GA + h:SM_GA + h + 1] for s, h in items]
        bcol = [btc[s][:, SM_GB + h:SM_GB + h + 1] for s, h in items]
        q = [qn_ref[pl.ds(r0[s], c), hs[h]] for s, h in items]
        k = [kn_ref[pl.ds(r0[s], c), hs[h]] for s, h in items]
        v = [vv_ref[pl.ds(r0[s], c), hs[h]] for s, h in items]
        kb = [k[i] * bcol[i] for i in range(n_items)]
        eg = [jnp.exp(gcol[i]) for i in range(n_items)]
        rhs = [jnp.concatenate([v[i] * bcol[i], kb[i] * eg[i]], axis=1) for i in range(n_items)]
        ab = [lax.dot_general(
            side(side(kb[2 * p], q[2 * p], 0), side(kb[2 * p + 1], q[2 * p + 1], 0), 1).astype(BF16),
            bdiag(k[2 * p], k[2 * p + 1]).astype(BF16),
            (((1,), (1,)), ((), ())), preferred_element_type=F32) for p in pairs]
        decay = []
        for p in pairs:
            s, h = items[2 * p]
            ha, hb = SM_GA + h, SM_GA + h + 1
            diff = (jnp.where(left, gcol[2 * p], gcol[2 * p + 1])
                    - jnp.where(left_row, gct2[s][ha:ha + 1, :], gct2[s][hb:hb + 1, :]))
            decay.append(jnp.where(incl, jnp.exp(jnp.where(incl, diff, 0.0)), 0.0))
        qk = [jnp.where(incl, ab[p][c:, :] * decay[p], 0.0) for p in pairs]
        pw = [-jnp.where(strict, ab[p][0:c, :] * decay[p], 0.0) for p in pairs]
        r = list(pw)
        for _ in range(int(math.log2(c)) - 1):
            pw = [mm(pw[p], bdiag2(pw[p])) for p in pairs]
            rp = [mm(r[p], bdiag2(pw[p])) for p in pairs]
            r = [r[p] + pw[p] + rp[p] for p in pairs]
        rr = [mm(r[p], bdiag(rhs[2 * p], rhs[2 * p + 1])) for p in pairs]
        sol = [rhs[i] + rr[i // 2][:, (i % 2) * 2 * LANES:(i % 2 + 1) * 2 * LANES] for i in range(n_items)]
        q_dec = [q[i] * eg[i] for i in range(n_items)]
        k_dec_t = [(k[i] * jnp.exp(glast[i] - gcol[i])).T for i in range(n_items)]
        st = [state_ref[h] for h in range(GDN_HEADS)]
        for s in range(nch):
            base = s * GDN_HEADS
            hp = range(GDN_HEADS // 2)
            ws = [mm(side(side(sol[base + 2 * p][:, LANES:], q_dec[base + 2 * p], 0),
                          side(sol[base + 2 * p + 1][:, LANES:], q_dec[base + 2 * p + 1], 0), 1),
                     bdiag(st[2 * p], st[2 * p + 1])) for p in hp]
            v_new = [sol[base + h][:, :LANES] - ws[h // 2][0:c, half(h)] for h in range(GDN_HEADS)]
            qv = [mm(qk[base // 2 + p], bdiag(v_new[2 * p], v_new[2 * p + 1])) for p in hp]
            o_n = [ws[h // 2][c:, half(h)] + qv[h // 2][:, half(h)] for h in range(GDN_HEADS)]
            st = [st[h] * jnp.exp(glast[base + h]) + mm(k_dec_t[base + h], v_new[h]) for h in range(GDN_HEADS)]
            for h in range(GDN_HEADS):
                on = o_n[h] * lax.rsqrt(jnp.mean(o_n[h] * o_n[h], axis=-1, keepdims=True) + EPS) * gn
                zz = z_ref[pl.ds(r0[s], c), hs[h]].astype(F32)
                o_ref[pl.ds(r0[s], c), hs[h]] = (on * (zz * jax.nn.sigmoid(zz))).astype(o_ref.dtype)
        for h in range(GDN_HEADS):
            state_ref[h] = st[h]
        return carry

    lax.fori_loop(0, blk // (c * GDN_CHUNKS_PER_STEP), chunk_body, 0)


def _gdn(proj, small, conv_w, alog_row, dtb_row, gn_row, batch, seq):
    t = proj.shape[0]
    blk = min(GDN_BLK, seq)
    ns = seq // blk
    row = lambda b, s: b * ns + s
    return pl.pallas_call(
        _gdn_kernel,
        grid=(batch, ns),
        in_specs=[pl.BlockSpec((blk, GDN_CONV_DIM), lambda b, s: (row(b, s), COL_QKV * LANES // GDN_CONV_DIM)),
                  pl.BlockSpec((blk, D_MODEL), lambda b, s: (row(b, s), COL_Z * LANES // D_MODEL)),
                  pl.BlockSpec((blk, LANES), lambda b, s: (row(b, s), 0)),
                  pl.BlockSpec((GDN_CONV, GDN_CONV_DIM), lambda b, s: (0, 0)),
                  pl.BlockSpec((1, LANES), lambda b, s: (0, 0)),
                  pl.BlockSpec((1, LANES), lambda b, s: (0, 0)),
                  pl.BlockSpec((1, LANES), lambda b, s: (0, 0))],
        out_specs=pl.BlockSpec((blk, D_MODEL), lambda b, s: (row(b, s), 0)),
        out_shape=jax.ShapeDtypeStruct((t, D_MODEL), BF16),
        scratch_shapes=[pltpu.VMEM((8, GDN_CONV_DIM), F32),
                        pltpu.VMEM((GDN_HEADS, GDN_HEAD, GDN_HEAD), F32),
                        pltpu.VMEM((blk, D_MODEL), F32),
                        pltpu.VMEM((blk, D_MODEL), F32),
                        pltpu.VMEM((blk, D_MODEL), F32),
                        pltpu.VMEM((blk, LANES), F32),
                        pltpu.VMEM((blk, LANES), F32)],
        compiler_params=_cparams(("arbitrary", "arbitrary")),
        name="gdn",
    )(proj, proj, small, conv_w, alog_row, dtb_row, gn_row)


def _rope(y, cos, sin_lo, sin_hi, half):
    return y * cos + pltpu.roll(y, LANES - half, axis=1) * sin_lo + pltpu.roll(y, half, axis=1) * sin_hi


def _prep_kernel(aq_ref, akv_ref, iq_ref, sm_ref, tab_ref, gq_ref, gk_ref,
                 q_out, k_out, v_out, qi_out, kit_out, w_out):
    half_a = ATT_HEAD // ROPE_FRACTION // 2
    half_i = IDX_DIM // ROPE_FRACTION // 2
    t = tab_ref[...]
    lane = lax.broadcasted_iota(I32, t.shape, 1)
    from_lane = lambda src: pltpu.roll(t, (-src) % LANES, axis=1)
    between = lambda lo, hi: (lane >= lo) & (lane < hi)
    ca = jnp.where(lane < 2 * half_a, t, 1.0)
    sa = from_lane(TAB_SIN_A)
    sa1 = jnp.where(lane < half_a, sa, 0.0)
    sa2 = jnp.where(between(half_a, 2 * half_a), sa, 0.0)
    lm = lane & (IDX_DIM - 1)
    low = lane < IDX_DIM
    ci = jnp.where(lm < 2 * half_i, jnp.where(low, from_lane(TAB_COS_I), from_lane(TAB_COS_I - IDX_DIM)), 1.0)
    si = jnp.where(low, from_lane(TAB_SIN_I), from_lane(TAB_SIN_I - IDX_DIM))
    si1 = jnp.where(lm < half_i, si, 0.0)
    si2 = jnp.where((lm >= half_i) & (lm < 2 * half_i), si, 0.0)
    ck = jnp.where(between(SM_IK, SM_IK + 2 * half_i), from_lane(TAB_COS_I - SM_IK), 1.0)
    sk = from_lane(TAB_SIN_I - SM_IK)
    sk1 = jnp.where(between(SM_IK, SM_IK + half_i), sk, 0.0)
    sk2 = jnp.where(between(SM_IK + half_i, SM_IK + 2 * half_i), sk, 0.0)
    gq = gq_ref[...]
    gk = gk_ref[...]
    for h in range(ATT_HEADS):
        hs = slice(h * LANES, (h + 1) * LANES)
        xh = aq_ref[:, hs].astype(F32)
        y = xh * lax.rsqrt(jnp.mean(xh * xh, axis=-1, keepdims=True) + EPS) * gq
        q_out[:, hs] = _rope(y, ca, sa1, sa2, half_a).astype(BF16)
    for cc in range(ATT_KV_HEADS):
        hs = slice(cc * LANES, (cc + 1) * LANES)
        xk = akv_ref[:, hs].astype(F32)
        y = xk * lax.rsqrt(jnp.mean(xk * xk, axis=-1, keepdims=True) + EPS) * gk
        k_out[:, hs] = _rope(y, ca, sa1, sa2, half_a).astype(BF16)
        v_out[:, hs] = akv_ref[:, ATT_KV_HEADS * LANES + cc * LANES:ATT_KV_HEADS * LANES + (cc + 1) * LANES].astype(BF16)
    for j in range(IDX_HEADS * IDX_DIM // LANES):
        hs = slice(j * LANES, (j + 1) * LANES)
        qi_out[:, hs] = _rope(iq_ref[:, hs].astype(F32), ci, si1, si2, half_i).astype(BF16)
    sm = sm_ref[...]
    smr = _rope(sm, ck, sk1, sk2, half_i)
    kit_out[...] = smr.T[SM_IK:SM_IK + IDX_DIM, :].astype(BF16)
    w_out[...] = sm * (IDX_HEADS ** -0.5 * IDX_DIM ** -0.5)


def _prep(proj, small, table, gq_row, gk_row, batch, seq):
    t = proj.shape[0]
    tm = min(TM_PREP, seq)
    nsb = seq // tm
    tab_spec = pl.BlockSpec((tm, LANES), lambda i: (i % nsb, 0))
    row_spec = pl.BlockSpec((1, LANES), lambda i: (0, 0))
    return pl.pallas_call(
        _prep_kernel,
        grid=(t // tm,),
        in_specs=[pl.BlockSpec((tm, D_MODEL), lambda i: (i, COL_AQ * LANES // D_MODEL)),
                  pl.BlockSpec((tm, 4 * LANES), lambda i: (i, COL_AKV // 4)),
                  pl.BlockSpec((tm, 4 * LANES), lambda i: (i, COL_IQ // 4)),
                  pl.BlockSpec((tm, LANES), lambda i: (i, 0))]
                 + [tab_spec, row_spec, row_spec],
        out_specs=[pl.BlockSpec((tm, D_MODEL), lambda i: (i, 0)),
                   pl.BlockSpec((tm, ATT_KV_HEADS * LANES), lambda i: (i, 0)),
                   pl.BlockSpec((tm, ATT_KV_HEADS * LANES), lambda i: (i, 0)),
                   pl.BlockSpec((tm, IDX_HEADS * IDX_DIM), lambda i: (i, 0)),
                   pl.BlockSpec((None, None, IDX_DIM, tm), lambda i: (i // nsb, i % nsb, 0, 0)),
                   pl.BlockSpec((tm, LANES), lambda i: (i, 0))],
        out_shape=[jax.ShapeDtypeStruct((t, D_MODEL), BF16),
                   jax.ShapeDtypeStruct((t, ATT_KV_HEADS * LANES), BF16),
                   jax.ShapeDtypeStruct((t, ATT_KV_HEADS * LANES), BF16),
                   jax.ShapeDtypeStruct((t, IDX_HEADS * IDX_DIM), BF16),
                   jax.ShapeDtypeStruct((batch, nsb, IDX_DIM, tm), BF16),
                   jax.ShapeDtypeStruct((t, LANES), F32)],
        compiler_params=_cparams(("arbitrary",)),
        name="prep",
    )(proj, proj, proj, small, table, gq_row, gk_row)


def _select_kernel(qi_ref, w_ref, kit_ref, ustrict_ref, bias_ref, keys_ref, qh_ref, *, k_sel):
    nt_all = keys_ref.shape[1]
    n_groups = KT // LANES
    lane = lax.broadcasted_iota(I32, (QB, LANES), 1)
    to_key = lambda x: (lambda b: b ^ ((b >> 31) & 0x7FFFFFFF))(pltpu.bitcast(x, I32))
    to_val = lambda kx: pltpu.bitcast(kx ^ ((kx >> 31) & 0x7FFFFFFF), F32)

    def count(sb, nt, cand, strict):
        cb = jnp.broadcast_to(cand, (QB, LANES))

        def tile(j, cnt):
            kk = keys_ref[sb, j]
            for g in range(n_groups):
                kg = kk[:, g * LANES:(g + 1) * LANES]
                cnt = cnt + jnp.where(kg > cb if strict else kg >= cb, 1.0, 0.0)
            return cnt

        cnt = lax.fori_loop(0, nt, tile, jnp.zeros((QB, LANES), F32))
        return jnp.sum(cnt, axis=1, keepdims=True)

    def score_block(sb):
        qblk = pl.program_id(1) * SEL_BLOCKS + sb
        nt = (qblk * QB + QB + KT - 1) // KT
        rows = slice(sb * QB, (sb + 1) * QB)
        for h in range(IDX_HEADS):
            qh_ref[h * QB:(h + 1) * QB, :] = qi_ref[rows, h * IDX_DIM:(h + 1) * IDX_DIM]
        wb = [jnp.broadcast_to(w_ref[rows, SM_IW + h:SM_IW + h + 1], (QB, LANES)) for h in range(IDX_HEADS)]
        row_g = qblk * QB + lax.broadcasted_iota(I32, (QB, LANES), 0)

        def score_tile(j, carry, diagonal):
            mx, mn, ge0, gt0 = carry
            lg = jnp.dot(qh_ref[...], kit_ref[j], preferred_element_type=F32)
            acc = [jnp.zeros((QB, LANES), F32) for _ in range(n_groups)]
            for h in range(IDX_HEADS):
                for g in range(n_groups):
                    acc[g] = acc[g] + jnp.maximum(lg[h * QB:(h + 1) * QB, g * LANES:(g + 1) * LANES], 0.0) * wb[h]
            for g in range(n_groups):
                if diagonal:
                    causal = (j * KT + g * LANES + lane) <= row_g
                    sc = jnp.where(causal, acc[g], -jnp.inf)
                    mn = jnp.minimum(mn, jnp.where(causal, acc[g], jnp.inf))
                else:
                    sc = acc[g]
                    mn = jnp.minimum(mn, sc)
                keys_ref[sb, j, :, g * LANES:(g + 1) * LANES] = sc
                mx = jnp.maximum(mx, sc)
                ge0 = ge0 + jnp.where(sc >= 0.0, 1.0, 0.0)
                gt0 = gt0 + jnp.where(sc > 0.0, 1.0, 0.0)
            return mx, mn, ge0, gt0

        zeros = jnp.zeros((QB, LANES), F32)
        carry = _loop_groups(nt - 1, functools.partial(score_tile, diagonal=False),
                             (jnp.full((QB, LANES), -jnp.inf, F32), jnp.full((QB, LANES), jnp.inf, F32),
                              zeros, zeros), 4)
        mx, mn, ge0, gt0 = score_tile(nt - 1, carry, diagonal=True)
        row_max = jnp.max(mx, axis=1, keepdims=True)
        row_min = jnp.min(mn, axis=1, keepdims=True)
        n_ge0 = jnp.sum(ge0, axis=1, keepdims=True)
        n_gt0 = jnp.sum(gt0, axis=1, keepdims=True)
        n_causal = (row_g[:, 0:1] + 1).astype(F32)
        k_eff = jnp.minimum(float(k_sel), n_causal)
        take_all = n_causal <= k_eff
        above = n_gt0 > k_eff
        below = n_ge0 < k_eff
        lo0 = jnp.where(below, to_val(to_key(row_min) - 1), 0.0)
        hi0 = jnp.where(above, to_val(to_key(row_max) + 1), 0.0)
        lo0 = jnp.where(take_all, -FLT_MAX, lo0)
        hi0 = jnp.where(take_all, -FLT_MAX, hi0)
        at_zero = jnp.logical_not(jnp.logical_or(jnp.logical_or(above, below), take_all))
        zero_tied = jnp.logical_and(at_zero, n_ge0 > k_eff)
        return dict(nt=nt, k_eff=k_eff, lo0=lo0, hi0=hi0, zero_tied=zero_tied)

    blocks = [score_block(sb) for sb in range(SEL_BLOCKS)]

    def unsettled(los, his):
        flags = []
        for lo, hi in zip(los, his):
            mid = 0.5 * lo + 0.5 * hi
            settled = jnp.logical_or(lo == hi, jnp.logical_or(mid == lo, mid == hi))
            flags.append(jnp.where(settled, 0.0, 1.0))
        return jnp.max(functools.reduce(jnp.maximum, flags))

    def search_cond(st):
        return jnp.logical_and(st[0] < MAX_SEARCH_STEPS, st[1] > 0.0)

    def search_body(st):
        it, _, los, his = st
        mids = [0.5 * lo + 0.5 * hi for lo, hi in zip(los, his)]
        ns = [count(sb, blocks[sb]["nt"], mids[sb], False) for sb in range(SEL_BLOCKS)]
        los = tuple(jnp.where(ns[sb] >= blocks[sb]["k_eff"], mids[sb], los[sb]) for sb in range(SEL_BLOCKS))
        his = tuple(jnp.where(ns[sb] <= blocks[sb]["k_eff"], mids[sb], his[sb]) for sb in range(SEL_BLOCKS))
        return it + 1, unsettled(los, his), los, his

    los0 = tuple(b["lo0"] for b in blocks)
    his0 = tuple(b["hi0"] for b in blocks)
    _, _, los, his = lax.while_loop(search_cond, search_body, (jnp.int32(0), unsettled(los0, his0), los0, his0))

    def finish_block(sb):
        nt, k_eff = blocks[sb]["nt"], blocks[sb]["k_eff"]
        vstar = los[sb]
        vsb = jnp.broadcast_to(vstar, (QB, KT))
        tied = jnp.logical_or(vstar != his[sb], blocks[sb]["zero_tied"])
        any_tied = jnp.max(jnp.where(tied, 1.0, 0.0)) > 0.0

        @pl.when(any_tied)
        def _():
            need = k_eff - count(sb, nt, vstar, True)
            ustrict = ustrict_ref[...]

            def final_tile(j, carry):
                kk = keys_ref[sb, j]
                tie = kk == vsb
                tie_b = jnp.where(tie, 1.0, 0.0).astype(BF16)
                rank = jnp.dot(tie_b, ustrict, preferred_element_type=F32) + carry
                keep_tie = jnp.where(rank < need, 0.0, NEG)
                bias = jnp.where(kk > vsb, 0.0, jnp.where(tie, keep_tie, NEG))
                bias_ref[sb, j] = bias.astype(BF16)
                return carry + jnp.sum(jnp.where(tie, 1.0, 0.0), axis=1, keepdims=True)

            _loop_groups(nt, final_tile, jnp.zeros((QB, 1), F32))

        @pl.when(jnp.logical_not(any_tied))
        def _():
            def final_tile(j, carry):
                bias_ref[sb, j] = jnp.where(keys_ref[sb, j] >= vsb, 0.0, NEG).astype(BF16)
                return carry

            lax.fori_loop(0, nt, final_tile, 0)

        def fill_tile(j, carry):
            bias_ref[sb, j] = jnp.full((QB, KT), NEG, BF16)
            return carry

        lax.fori_loop(nt, nt_all, fill_tile, 0)

    for sb in range(SEL_BLOCKS):
        finish_block(sb)


def _select(qi, wsm, kit, ustrict, batch, seq, k_sel):
    nq = seq // (QB * SEL_BLOCKS)
    nt = seq // KT
    rows = QB * SEL_BLOCKS
    return pl.pallas_call(
        functools.partial(_select_kernel, k_sel=k_sel),
        grid=(batch, nq),
        in_specs=[pl.BlockSpec((rows, IDX_HEADS * IDX_DIM), lambda b, q: (b * nq + q, 0)),
                  pl.BlockSpec((rows, LANES), lambda b, q: (b * nq + q, 0)),
                  pl.BlockSpec((None, nt, IDX_DIM, KT), lambda b, q: (b, 0, 0, 0)),
                  pl.BlockSpec((KT, KT), lambda b, q: (0, 0))],
        out_specs=pl.BlockSpec((None, SEL_BLOCKS, nt, QB, KT), lambda b, q: (b, q, 0, 0, 0)),
        out_shape=jax.ShapeDtypeStruct((batch, nq * SEL_BLOCKS, nt, QB, KT), BF16),
        scratch_shapes=[pltpu.VMEM((SEL_BLOCKS, nt, QB, KT), F32),
                        pltpu.VMEM((IDX_HEADS * QB, IDX_DIM), BF16)],
        compiler_params=_cparams(("arbitrary", "arbitrary")),
        name="select",
    )(qi, wsm, kit, ustrict)


def _attn_kernel(q_ref, k_ref, v_ref, bias_ref, bound_ref, o_ref, m_ref, l_ref, acc_ref, accw_ref):
    qblk = pl.program_id(1)
    nt = (qblk * QA + QA + KT - 1) // KT
    scale = ATT_HEAD ** -0.5
    q4 = [jnp.concatenate([q_ref[:, (cc * ATT_GROUP + g) * LANES:(cc * ATT_GROUP + g + 1) * LANES]
                           for g in range(ATT_GROUP)], axis=0) for cc in range(ATT_KV_HEADS)]

    def bias_rows(j):
        b = jnp.concatenate([bias_ref[r, j] for r in range(QA // QB)], axis=0).astype(F32)
        return jnp.concatenate([b] * ATT_GROUP, axis=0)

    def write_out(cc, o):
        for g in range(ATT_GROUP):
            h = cc * ATT_GROUP + g
            o_ref[:, h * LANES:(h + 1) * LANES] = o[g * QA:(g + 1) * QA, :].astype(o_ref.dtype)

    safe = bound_ref[0] <= SAFE_LOGIT_BOUND

    @pl.when(safe)
    def _():
        accw_ref[...] = jnp.zeros_like(accw_ref)
        ones = jnp.ones((KT, LANES), BF16)

        def kv_tile(j, carry):
            r0 = pl.multiple_of(j * KT, KT)
            bias4 = bias_rows(j)
            for cc in range(ATT_KV_HEADS):
                kc = k_ref[pl.ds(r0, KT), cc * LANES:(cc + 1) * LANES]
                vc = v_ref[pl.ds(r0, KT), cc * LANES:(cc + 1) * LANES]
                s = lax.dot_general(q4[cc], kc, (((1,), (1,)), ((), ())), preferred_element_type=F32)
                p = jnp.exp2(s * (scale * LOG2E) + bias4)
                accw_ref[cc] += jnp.dot(p.astype(BF16), jnp.concatenate([vc, ones], axis=1),
                                        preferred_element_type=F32)
            return carry

        _loop_groups(nt, kv_tile, 0)
        for cc in range(ATT_KV_HEADS):
            aw = accw_ref[cc]
            write_out(cc, aw[:, :LANES] / aw[:, LANES:])

    @pl.when(jnp.logical_not(safe))
    def _():
        m_ref[...] = jnp.full_like(m_ref, NEG)
        l_ref[...] = jnp.zeros_like(l_ref)
        acc_ref[...] = jnp.zeros_like(acc_ref)

        def kv_tile(j, carry):
            r0 = pl.multiple_of(j * KT, KT)
            bias4 = bias_rows(j)
            for cc in range(ATT_KV_HEADS):
                kc = k_ref[pl.ds(r0, KT), cc * LANES:(cc + 1) * LANES]
                vc = v_ref[pl.ds(r0, KT), cc * LANES:(cc + 1) * LANES]
                s = lax.dot_general(q4[cc], kc, (((1,), (1,)), ((), ())), preferred_element_type=F32)
                s = s * scale + bias4
                m_prev = m_ref[cc]
                m_new = jnp.maximum(m_prev, jnp.max(s, axis=1, keepdims=True))
                alpha = jnp.exp(m_prev - m_new)
                p = jnp.exp(s - m_new)
                l_ref[cc] = alpha * l_ref[cc] + jnp.sum(p, axis=1, keepdims=True)
                acc_ref[cc] = alpha * acc_ref[cc] + jnp.dot(p.astype(BF16), vc, preferred_element_type=F32)
                m_ref[cc] = m_new
            return carry

        lax.fori_loop(0, nt, kv_tile, 0)
        for cc in range(ATT_KV_HEADS):
            write_out(cc, acc_ref[cc] / l_ref[cc])


def _attention(q, k, v, bias, logit_bound, batch, seq):
    t = q.shape[0]
    nq = seq // QA
    nt = seq // KT
    rows = ATT_GROUP * QA
    return pl.pallas_call(
        _attn_kernel,
        grid=(batch, nq),
        in_specs=[pl.BlockSpec((QA, D_MODEL), lambda b, i: (b * nq + i, 0)),
                  pl.BlockSpec((seq, ATT_KV_HEADS * LANES), lambda b, i: (b, 0)),
                  pl.BlockSpec((seq, ATT_KV_HEADS * LANES), lambda b, i: (b, 0)),
                  pl.BlockSpec((None, QA // QB, nt, QB, KT), lambda b, i: (b, i, 0, 0, 0)),
                  pl.BlockSpec(memory_space=pltpu.SMEM)],
        out_specs=pl.BlockSpec((QA, D_MODEL), lambda b, i: (b * nq + i, 0)),
        out_shape=jax.ShapeDtypeStruct((t, D_MODEL), BF16),
        scratch_shapes=[pltpu.VMEM((ATT_KV_HEADS, rows, 1), F32),
                        pltpu.VMEM((ATT_KV_HEADS, rows, 1), F32),
                        pltpu.VMEM((ATT_KV_HEADS, rows, LANES), F32),
                        pltpu.VMEM((ATT_KV_HEADS, rows, 2 * LANES), F32)],
        compiler_params=_cparams(("arbitrary", "arbitrary")),
        name="attention",
    )(q, k, v, bias, logit_bound)


def _merge_kernel(ga_ref, gb_ref, og_ref, oa_ref, x_ref, w_ref, o_ref):
    merged = (jax.nn.sigmoid(ga_ref[...].astype(F32)) * og_ref[...].astype(F32)
              + jax.nn.sigmoid(gb_ref[...].astype(F32)) * oa_ref[...].astype(F32))
    o_ref[...] = x_ref[...] + jnp.dot(merged.astype(BF16), w_ref[...], preferred_element_type=F32)


def _merge(proj, o_gdn, o_att, x2, w_out):
    t = x2.shape[0]
    tm = TM_MERGE
    blk = lambda c: pl.BlockSpec((tm, D_MODEL), lambda i: (i, c))
    return pl.pallas_call(
        _merge_kernel,
        grid=(t // tm,),
        in_specs=[blk(COL_GA * LANES // D_MODEL), blk(COL_GB * LANES // D_MODEL), blk(0), blk(0), blk(0),
                  pl.BlockSpec((D_MODEL, D_MODEL), lambda i: (0, 0))],
        out_specs=blk(0),
        out_shape=jax.ShapeDtypeStruct((t, D_MODEL), F32),
        compiler_params=_cparams(("arbitrary",)),
        name="merge",
    )(proj, proj, o_gdn, o_att, x2, w_out)


def _mlp_kernel(x_ref, g_ref, wu_ref, wd_ref, o_ref, h_ref, acc_ref):
    f = pl.program_id(1)

    @pl.when(f == 0)
    def _():
        x = x_ref[...]
        ms = jnp.mean(x * x, axis=-1, keepdims=True)
        h_ref[...] = (x * lax.rsqrt(ms + EPS) * g_ref[...]).astype(BF16)
        acc_ref[...] = x

    up = jnp.dot(h_ref[...], wu_ref[...], preferred_element_type=F32)
    act = jnp.square(jnp.maximum(up, 0.0))
    acc_ref[...] += jnp.dot(act.astype(BF16), wd_ref[...], preferred_element_type=F32)

    @pl.when(f == pl.num_programs(1) - 1)
    def _():
        o_ref[...] = acc_ref[...]


def _mlp(x1, g_row, w_up, w_down):
    t = x1.shape[0]
    tm = min(TM_MLP, t)
    return pl.pallas_call(
        _mlp_kernel,
        grid=(t // tm, D_FF // TF_MLP),
        in_specs=[pl.BlockSpec((tm, D_MODEL), lambda i, f: (i, 0)),
                  pl.BlockSpec((1, D_MODEL), lambda i, f: (0, 0)),
                  pl.BlockSpec((D_MODEL, TF_MLP), lambda i, f: (0, f)),
                  pl.BlockSpec((TF_MLP, D_MODEL), lambda i, f: (f, 0))],
        out_specs=pl.BlockSpec((tm, D_MODEL), lambda i, f: (i, 0)),
        out_shape=jax.ShapeDtypeStruct((t, D_MODEL), F32),
        scratch_shapes=[pltpu.VMEM((tm, D_MODEL), BF16), pltpu.VMEM((tm, D_MODEL), F32)],
        compiler_params=_cparams(("arbitrary", "arbitrary")),
        name="mlp",
    )(x1, g_row, w_up, w_down)


def _pack_w_in(w_in):
    pts = np.cumsum(np.array(IN_SPLITS))[:-1].tolist()
    (g_qkv, g_z, g_a, g_b, a_q, a_k, a_v, i_q, i_k, i_w, gate_a, gate_b) = jnp.split(w_in, pts, axis=-1)
    small = jnp.concatenate([g_a, g_b, i_k, i_w], axis=-1)
    small = jnp.pad(small, ((0, 0), (0, LANES - small.shape[-1])))
    main = jnp.concatenate([g_qkv, g_z, a_q, a_k, a_v, i_q, gate_a, gate_b], axis=-1)
    return main.astype(BF16), small.astype(BF16)


def _lane_row(vec, offset=0):
    return jnp.zeros((1, LANES), F32).at[0, offset:offset + vec.shape[0]].set(vec.astype(F32))


def _rope_table(seq):
    tab = np.zeros((seq, LANES), np.float64)
    pos = np.arange(seq, dtype=np.float64)[:, None]
    for rot_dim, base in ((ATT_HEAD // ROPE_FRACTION, 0), (IDX_DIM // ROPE_FRACTION, TAB_COS_I)):
        half = rot_dim // 2
        ang = pos * ROPE_THETA ** (-np.arange(0, rot_dim, 2, dtype=np.float64) / rot_dim)[None, :]
        tab[:, base:base + half] = tab[:, base + half:base + 2 * half] = np.cos(ang)
        tab[:, base + 2 * half:base + 3 * half] = -np.sin(ang)
        tab[:, base + 3 * half:base + 4 * half] = np.sin(ang)
    return jnp.asarray(tab, dtype=F32)


def _layer(x2, batch, seq, norm_mix_g, w_in, conv_w, a_log, dt_bias, gdn_norm_g, q_norm_g, k_norm_g,
           w_out, norm_mlp_g, w_mlp_up, w_mlp_down):
    k_sel = min(TOPK_MAX, seq // 4)
    w_main, w_small = _pack_w_in(w_in)
    g_mix = norm_mix_g.reshape(1, D_MODEL)
    proj = _inproj(x2, g_mix, w_main, TN_IN, BF16, "inproj")
    small = _inproj(x2, g_mix, w_small, LANES, F32, "inproj_small")

    o_gdn = _gdn(proj, small, conv_w, _lane_row(a_log, SM_GA), _lane_row(dt_bias, SM_GA),
                 gdn_norm_g.reshape(1, LANES), batch, seq)

    q_att, k_att, v_att, q_idx, k_idx_t, w_idx = _prep(
        proj, small, _rope_table(seq), q_norm_g.reshape(1, LANES), k_norm_g.reshape(1, LANES), batch, seq)

    ii = np.arange(KT)
    ustrict = jnp.asarray((ii[:, None] < ii[None, :]).astype(np.float32), dtype=BF16)
    bias = _select(q_idx, w_idx, k_idx_t, ustrict, batch, seq, k_sel)
    logit_bound = (1.01 * ATT_HEAD * ATT_HEAD ** -0.5) * jnp.max(jnp.abs(q_norm_g)) * jnp.max(jnp.abs(k_norm_g))
    o_att = _attention(q_att, k_att, v_att, bias, logit_bound.reshape(1).astype(F32), batch, seq)

    x1 = _merge(proj, o_gdn, o_att, x2, w_out.astype(BF16))
    return _mlp(x1, norm_mlp_g.reshape(1, D_MODEL), w_mlp_up.astype(BF16), w_mlp_down.astype(BF16))


def kernel(x, norm_mix_g, w_in, conv_w, a_log, dt_bias, gdn_norm_g, q_norm_g, k_norm_g, w_out,
           norm_mlp_g, w_mlp_up, w_mlp_down):
    batch, seq, _ = x.shape
    x2 = x.reshape(batch * seq, D_MODEL)
    for layer in range(norm_mix_g.shape[0]):
        x2 = _layer(x2, batch, seq, norm_mix_g[layer], w_in[layer], conv_w[layer], a_log[layer],
                    dt_bias[layer], gdn_norm_g[layer], q_norm_g[layer], k_norm_g[layer], w_out[layer],
                    norm_mlp_g[layer], w_mlp_up[layer], w_mlp_down[layer])
    return x2.reshape(batch, seq, D_MODEL)
```

```python
import functools
import math

import numpy as np
import jax
import jax.numpy as jnp
from jax import lax
from jax.experimental import pallas as pl
from jax.experimental.pallas import tpu as pltpu

F32 = jnp.float32
BF16 = jnp.bfloat16
I32 = jnp.int32

D_MODEL = 1024
GDN_HEADS = 8
GDN_HEAD = 128
GDN_CONV = 4
GDN_CHUNK = 64
GDN_CONV_DIM = 3 * GDN_HEADS * GDN_HEAD
ATT_HEADS = 8
ATT_HEAD = 128
ATT_KV_HEADS = 2
ATT_GROUP = ATT_HEADS // ATT_KV_HEADS
IDX_HEADS = 8
IDX_DIM = 64
TOPK_MAX = 256
ROPE_THETA = 500000.0
ROPE_FRACTION = 4
D_FF = 4 * D_MODEL
EPS = 1e-6

IN_SPLITS = (GDN_CONV_DIM, D_MODEL, GDN_HEADS, GDN_HEADS, D_MODEL, 256, 256,
             IDX_HEADS * IDX_DIM, IDX_DIM, IDX_HEADS, D_MODEL, D_MODEL)

LANES = 128
assert 2 * GDN_CHUNK == LANES
COL_QKV = 0
COL_Z = 24
COL_AQ = 32
COL_AKV = 40
COL_IQ = 44
COL_GA = 48
COL_GB = 56
SM_GA = 0
SM_GB = 8
SM_IK = 16
SM_IW = 80
TAB_SIN_A = 2 * (ATT_HEAD // ROPE_FRACTION // 2)
TAB_COS_I = 4 * (ATT_HEAD // ROPE_FRACTION // 2)
TAB_SIN_I = TAB_COS_I + 2 * (IDX_DIM // ROPE_FRACTION // 2)

NEG = -1e30
FLT_MAX = 3.4028234663852886e38
MAX_SEARCH_STEPS = 254 + 24 + 8
LOG2E = 1.4426950408889634
SAFE_LOGIT_BOUND = 60.0

TM_IN = 1024
TN_IN = 16 * LANES
GDN_BLK = 512
GDN_CHUNKS_PER_STEP = 8
TM_PREP = 512
QB = 128
SEL_BLOCKS = 1
QA = 256
KT = 512
assert KT % QB == 0
TM_MERGE = 512
TM_MLP = 1024
TF_MLP = 512
VMEM_LIMIT = 56 * 2 ** 20


def _cparams(sem):
    return pltpu.CompilerParams(dimension_semantics=sem, vmem_limit_bytes=VMEM_LIMIT)


def _loop_groups(n, body, init, group=2):
    shift = group.bit_length() - 1

    def many(i, carry):
        for s in range(group):
            carry = body(group * i + s, carry)
        return carry

    n_groups = lax.shift_right_logical(n, shift)
    carry = lax.fori_loop(0, n_groups, many, init)
    return lax.fori_loop(lax.shift_left(n_groups, shift), n, body, carry)


def _inproj_kernel(x_ref, g_ref, w_ref, o_ref):
    x = x_ref[...]
    ms = jnp.mean(x * x, axis=-1, keepdims=True)
    h = x * lax.rsqrt(ms + EPS) * g_ref[...]
    o_ref[...] = jnp.dot(h.astype(BF16), w_ref[...], preferred_element_type=F32).astype(o_ref.dtype)


def _inproj(x2, g_row, w_p, tn, out_dtype, name):
    t = x2.shape[0]
    n_cols = w_p.shape[1]
    return pl.pallas_call(
        _inproj_kernel,
        grid=(n_cols // tn, t // TM_IN),
        in_specs=[pl.BlockSpec((TM_IN, D_MODEL), lambda n, m: (m, 0)),
                  pl.BlockSpec((1, D_MODEL), lambda n, m: (0, 0)),
                  pl.BlockSpec((D_MODEL, tn), lambda n, m: (0, n))],
        out_specs=pl.BlockSpec((TM_IN, tn), lambda n, m: (m, n)),
        out_shape=jax.ShapeDtypeStruct((t, n_cols), out_dtype),
        compiler_params=_cparams(("arbitrary", "arbitrary")),
        name=name,
    )(x2, g_row, w_p)


def _gdn_kernel(qkv_ref, z_ref, sm_ref, cw_ref, alog_ref, dtb_ref, gn_ref, o_ref,
                halo_ref, state_ref, qn_ref, kn_ref, vv_ref, gc_ref, bt_ref):
    blk = qkv_ref.shape[0]
    c = GDN_CHUNK

    @pl.when(pl.program_id(1) == 0)
    def _():
        halo_ref[...] = jnp.zeros_like(halo_ref)
        state_ref[...] = jnp.zeros_like(state_ref)

    row8 = lax.broadcasted_iota(I32, (8, LANES), 0)
    for j in range(GDN_CONV_DIM // LANES):
        lo = j * LANES
        u = qkv_ref[:, lo:lo + LANES].astype(F32)
        hal = halo_ref[:, lo:lo + LANES]
        w = cw_ref[:, lo:lo + LANES]
        y = u * w[GDN_CONV - 1:GDN_CONV, :]
        for sft in range(1, GDN_CONV):
            ur = pltpu.roll(u, sft, axis=0)
            hr = pltpu.roll(hal, sft, axis=0)
            top = jnp.where(row8 < sft, hr, ur[0:8, :])
            ush = jnp.concatenate([top, ur[8:, :]], axis=0)
            y = y + ush * w[GDN_CONV - 1 - sft:GDN_CONV - sft, :]
        halo_ref[:, lo:lo + LANES] = u[blk - 8:, :]
        a = y * jax.nn.sigmoid(y)
        head = j % GDN_HEADS
        if j < 2 * GDN_HEADS:
            a = a * lax.rsqrt(jnp.sum(a * a, axis=-1, keepdims=True) + EPS)
        if j < GDN_HEADS:
            qn_ref[:, head * LANES:(head + 1) * LANES] = a * (GDN_HEAD ** -0.5)
        elif j < 2 * GDN_HEADS:
            kn_ref[:, head * LANES:(head + 1) * LANES] = a
        else:
            vv_ref[:, head * LANES:(head + 1) * LANES] = a

    sm = sm_ref[...]
    xg = sm + dtb_ref[...]
    softplus = jnp.maximum(xg, 0.0) + jnp.log1p(jnp.exp(-jnp.abs(xg)))
    gc = -jnp.exp(alog_ref[...]) * softplus
    row_in_chunk = lax.broadcasted_iota(I32, (blk, LANES), 0) & (c - 1)
    d = 1
    while d < c:
        gc = gc + jnp.where(row_in_chunk >= d, pltpu.roll(gc, d, axis=0), 0.0)
        d *= 2
    gc_ref[...] = gc
    bt_ref[...] = jax.nn.sigmoid(sm)

    ii = lax.broadcasted_iota(I32, (c, 2 * c), 0)
    lane_cc = lax.broadcasted_iota(I32, (c, 2 * c), 1)
    jj = lane_cc & (c - 1)
    incl = ii >= jj
    strict = ii > jj
    left = lane_cc < c
    left_row = lax.broadcasted_iota(I32, (1, 2 * c), 1) < c
    gn = gn_ref[...]

    def chunk_body(ci, carry):
        nch = GDN_CHUNKS_PER_STEP
        r0 = [pl.multiple_of((ci * nch + s) * c, c) for s in range(nch)]
        items = [(s, h) for s in range(nch) for h in range(GDN_HEADS)]
        n_items = len(items)
        pairs = range(n_items // 2)
        hs = [slice(h * LANES, (h + 1) * LANES) for h in range(GDN_HEADS)]
        half = lambda i: slice((i % 2) * LANES, (i % 2 + 1) * LANES)
        mm = lambda a, b: jnp.dot(a.astype(BF16), b.astype(BF16), preferred_element_type=F32)

        def bdiag(xa, xb):
            return jnp.concatenate([jnp.concatenate([xa, jnp.zeros_like(xb)], axis=1),
                                    jnp.concatenate([jnp.zeros_like(xa), xb], axis=1)], axis=0)

        def bdiag2(x2):
            return jnp.concatenate([jnp.where(left, x2, 0.0), jnp.where(left, 0.0, x2)], axis=0)

        def side(xa, xb, axis):
            return jnp.concatenate([xa, xb], axis=axis)

        gcc = [gc_ref[pl.ds(r0[s], c), :] for s in range(nch)]
        btc = [bt_ref[pl.ds(r0[s], c), :] for s in range(nch)]
        gct2 = [jnp.concatenate([g, g], axis=0).T for g in gcc]
        gcol = [gcc[s][:, SM_GA + h:SM_GA + h + 1] for s, h in items]
        glast = [gcc[s][c - 1:c, SM_GA + h:SM_GA + h + 1] for s, h in items]
        bcol = [btc[s][:, SM_GB + h:SM_GB + h + 1] for s, h in items]
        q = [qn_ref[pl.ds(r0[s], c), hs[h]] for s, h in items]
        k = [kn_ref[pl.ds(r0[s], c), hs[h]] for s, h in items]
        v = [vv_ref[pl.ds(r0[s], c), hs[h]] for s, h in items]
        kb = [k[i] * bcol[i] for i in range(n_items)]
        eg = [jnp.exp(gcol[i]) for i in range(n_items)]
        rhs = [jnp.concatenate([v[i] * bcol[i], kb[i] * eg[i]], axis=1) for i in range(n_items)]
        ab = [lax.dot_general(
            side(side(kb[2 * p], q[2 * p], 0), side(kb[2 * p + 1], q[2 * p + 1], 0), 1).astype(BF16),
            bdiag(k[2 * p], k[2 * p + 1]).astype(BF16),
            (((1,), (1,)), ((), ())), preferred_element_type=F32) for p in pairs]
        decay = []
        for p in pairs:
            s, h = items[2 * p]
            ha, hb = SM_GA + h, SM_GA + h + 1
            diff = (jnp.where(left, gcol[2 * p], gcol[2 * p + 1])
                    - jnp.where(left_row, gct2[s][ha:ha + 1, :], gct2[s][hb:hb + 1, :]))
            decay.append(jnp.where(incl, jnp.exp(jnp.where(incl, diff, 0.0)), 0.0))
        qk = [jnp.where(incl, ab[p][c:, :] * decay[p], 0.0) for p in pairs]
        pw = [-jnp.where(strict, ab[p][0:c, :] * decay[p], 0.0) for p in pairs]
        r = list(pw)
        for _ in range(int(math.log2(c)) - 1):
            pw = [mm(pw[p], bdiag2(pw[p])) for p in pairs]
            rp = [mm(r[p], bdiag2(pw[p])) for p in pairs]
            r = [r[p] + pw[p] + rp[p] for p in pairs]
        rr = [mm(r[p], bdiag(rhs[2 * p], rhs[2 * p + 1])) for p in pairs]
        sol = [rhs[i] + rr[i // 2][:, (i % 2) * 2 * LANES:(i % 2 + 1) * 2 * LANES] for i in range(n_items)]
        q_dec = [q[i] * eg[i] for i in range(n_items)]
        k_dec_t = [(k[i] * jnp.exp(glast[i] - gcol[i])).T for i in range(n_items)]
        st = [state_ref[h] for h in range(GDN_HEADS)]
        for s in range(nch):
            base = s * GDN_HEADS
            hp = range(GDN_HEADS // 2)
            ws = [mm(side(side(sol[base + 2 * p][:, LANES:], q_dec[base + 2 * p], 0),
                          side(sol[base + 2 * p + 1][:, LANES:], q_dec[base + 2 * p + 1], 0), 1),
                     bdiag(st[2 * p], st[2 * p + 1])) for p in hp]
            v_new = [sol[base + h][:, :LANES] - ws[h // 2][0:c, half(h)] for h in range(GDN_HEADS)]
            qv = [mm(qk[base // 2 + p], bdiag(v_new[2 * p], v_new[2 * p + 1])) for p in hp]
            o_n = [ws[h // 2][c:, half(h)] + qv[h // 2][:, half(h)] for h in range(GDN_HEADS)]
            st = [st[h] * jnp.exp(glast[base + h]) + mm(k_dec_t[base + h], v_new[h]) for h in range(GDN_HEADS)]
            for h in range(GDN_HEADS):
                on = o_n[h] * lax.rsqrt(jnp.mean(o_n[h] * o_n[h], axis=-1, keepdims=True) + EPS) * gn
                zz = z_ref[pl.ds(r0[s], c), hs[h]].astype(F32)
                o_ref[pl.ds(r0[s], c), hs[h]] = (on * (zz * jax.nn.sigmoid(zz))).astype(o_ref.dtype)
        for h in range(GDN_HEADS):
            state_ref[h] = st[h]
        return carry

    lax.fori_loop(0, blk // (c * GDN_CHUNKS_PER_STEP), chunk_body, 0)


def _gdn(proj, small, conv_w, alog_row, dtb_row, gn_row, batch, seq):
    t = proj.shape[0]
    blk = min(GDN_BLK, seq)
    ns = seq // blk
    row = lambda b, s: b * ns + s
    return pl.pallas_call(
        _gdn_kernel,
        grid=(batch, ns),
        in_specs=[pl.BlockSpec((blk, GDN_CONV_DIM), lambda b, s: (row(b, s), COL_QKV * LANES // GDN_CONV_DIM)),
                  pl.BlockSpec((blk, D_MODEL), lambda b, s: (row(b, s), COL_Z * LANES // D_MODEL)),
                  pl.BlockSpec((blk, LANES), lambda b, s: (row(b, s), 0)),
                  pl.BlockSpec((GDN_CONV, GDN_CONV_DIM), lambda b, s: (0, 0)),
                  pl.BlockSpec((1, LANES), lambda b, s: (0, 0)),
                  pl.BlockSpec((1, LANES), lambda b, s: (0, 0)),
                  pl.BlockSpec((1, LANES), lambda b, s: (0, 0))],
        out_specs=pl.BlockSpec((blk, D_MODEL), lambda b, s: (row(b, s), 0)),
        out_shape=jax.ShapeDtypeStruct((t, D_MODEL), BF16),
        scratch_shapes=[pltpu.VMEM((8, GDN_CONV_DIM), F32),
                        pltpu.VMEM((GDN_HEADS, GDN_HEAD, GDN_HEAD), F32),
                        pltpu.VMEM((blk, D_MODEL), F32),
                        pltpu.VMEM((blk, D_MODEL), F32),
                        pltpu.VMEM((blk, D_MODEL), F32),
                        pltpu.VMEM((blk, LANES), F32),
                        pltpu.VMEM((blk, LANES), F32)],
        compiler_params=_cparams(("arbitrary", "arbitrary")),
        name="gdn",
    )(proj, proj, small, conv_w, alog_row, dtb_row, gn_row)


def _rope(y, cos, sin_lo, sin_hi, half):
    return y * cos + pltpu.roll(y, LANES - half, axis=1) * sin_lo + pltpu.roll(y, half, axis=1) * sin_hi


def _prep_kernel(aq_ref, akv_ref, iq_ref, sm_ref, tab_ref, gq_ref, gk_ref,
                 q_out, k_out, v_out, qi_out, kit_out, w_out):
    half_a = ATT_HEAD // ROPE_FRACTION // 2
    half_i = IDX_DIM // ROPE_FRACTION // 2
    t = tab_ref[...]
    lane = lax.broadcasted_iota(I32, t.shape, 1)
    from_lane = lambda src: pltpu.roll(t, (-src) % LANES, axis=1)
    between = lambda lo, hi: (lane >= lo) & (lane < hi)
    ca = jnp.where(lane < 2 * half_a, t, 1.0)
    sa = from_lane(TAB_SIN_A)
    sa1 = jnp.where(lane < half_a, sa, 0.0)
    sa2 = jnp.where(between(half_a, 2 * half_a), sa, 0.0)
    lm = lane & (IDX_DIM - 1)
    low = lane < IDX_DIM
    ci = jnp.where(lm < 2 * half_i, jnp.where(low, from_lane(TAB_COS_I), from_lane(TAB_COS_I - IDX_DIM)), 1.0)
    si = jnp.where(low, from_lane(TAB_SIN_I), from_lane(TAB_SIN_I - IDX_DIM))
    si1 = jnp.where(lm < half_i, si, 0.0)
    si2 = jnp.where((lm >= half_i) & (lm < 2 * half_i), si, 0.0)
    ck = jnp.where(between(SM_IK, SM_IK + 2 * half_i), from_lane(TAB_COS_I - SM_IK), 1.0)
    sk = from_lane(TAB_SIN_I - SM_IK)
    sk1 = jnp.where(between(SM_IK, SM_IK + half_i), sk, 0.0)
    sk2 = jnp.where(between(SM_IK + half_i, SM_IK + 2 * half_i), sk, 0.0)
    gq = gq_ref[...]
    gk = gk_ref[...]
    for h in range(ATT_HEADS):
        hs = slice(h * LANES, (h + 1) * LANES)
        xh = aq_ref[:, hs].astype(F32)
        y = xh * lax.rsqrt(jnp.mean(xh * xh, axis=-1, keepdims=True) + EPS) * gq
        q_out[:, hs] = _rope(y, ca, sa1, sa2, half_a).astype(BF16)
    for cc in range(ATT_KV_HEADS):
        hs = slice(cc * LANES, (cc + 1) * LANES)
        xk = akv_ref[:, hs].astype(F32)
        y = xk * lax.rsqrt(jnp.mean(xk * xk, axis=-1, keepdims=True) + EPS) * gk
        k_out[:, hs] = _rope(y, ca, sa1, sa2, half_a).astype(BF16)
        v_out[:, hs] = akv_ref[:, ATT_KV_HEADS * LANES + cc * LANES:ATT_KV_HEADS * LANES + (cc + 1) * LANES].astype(BF16)
    for j in range(IDX_HEADS * IDX_DIM // LANES):
        hs = slice(j * LANES, (j + 1) * LANES)
        qi_out[:, hs] = _rope(iq_ref[:, hs].astype(F32), ci, si1, si2, half_i).astype(BF16)
    sm = sm_ref[...]
    smr = _rope(sm, ck, sk1, sk2, half_i)
    kit_out[...] = smr.T[SM_IK:SM_IK + IDX_DIM, :].astype(BF16)
    w_out[...] = sm * (IDX_HEADS ** -0.5 * IDX_DIM ** -0.5)


def _prep(proj, small, table, gq_row, gk_row, batch, seq):
    t = proj.shape[0]
    tm = min(TM_PREP, seq)
    nsb = seq // tm
    tab_spec = pl.BlockSpec((tm, LANES), lambda i: (i % nsb, 0))
    row_spec = pl.BlockSpec((1, LANES), lambda i: (0, 0))
    return pl.pallas_call(
        _prep_kernel,
        grid=(t // tm,),
        in_specs=[pl.BlockSpec((tm, D_MODEL), lambda i: (i, COL_AQ * LANES // D_MODEL)),
                  pl.BlockSpec((tm, 4 * LANES), lambda i: (i, COL_AKV // 4)),
                  pl.BlockSpec((tm, 4 * LANES), lambda i: (i, COL_IQ // 4)),
                  pl.BlockSpec((tm, LANES), lambda i: (i, 0))]
                 + [tab_spec, row_spec, row_spec],
        out_specs=[pl.BlockSpec((tm, D_MODEL), lambda i: (i, 0)),
                   pl.BlockSpec((tm, ATT_KV_HEADS * LANES), lambda i: (i, 0)),
                   pl.BlockSpec((tm, ATT_KV_HEADS * LANES), lambda i: (i, 0)),
                   pl.BlockSpec((tm, IDX_HEADS * IDX_DIM), lambda i: (i, 0)),
                   pl.BlockSpec((None, None, IDX_DIM, tm), lambda i: (i // nsb, i % nsb, 0, 0)),
                   pl.BlockSpec((tm, LANES), lambda i: (i, 0))],
        out_shape=[jax.ShapeDtypeStruct((t, D_MODEL), BF16),
                   jax.ShapeDtypeStruct((t, ATT_KV_HEADS * LANES), BF16),
                   jax.ShapeDtypeStruct((t, ATT_KV_HEADS * LANES), BF16),
                   jax.ShapeDtypeStruct((t, IDX_HEADS * IDX_DIM), BF16),
                   jax.ShapeDtypeStruct((batch, nsb, IDX_DIM, tm), BF16),
                   jax.ShapeDtypeStruct((t, LANES), F32)],
        compiler_params=_cparams(("arbitrary",)),
        name="prep",
    )(proj, proj, proj, small, table, gq_row, gk_row)


def _select_kernel(qi_ref, w_ref, kit_ref, ustrict_ref, bias_ref, keys_ref, qh_ref, *, k_sel):
    nt_all = keys_ref.shape[1]
    n_groups = KT // LANES
    lane = lax.broadcasted_iota(I32, (QB, LANES), 1)
    to_key = lambda x: (lambda b: b ^ ((b >> 31) & 0x7FFFFFFF))(pltpu.bitcast(x, I32))
    to_val = lambda kx: pltpu.bitcast(kx ^ ((kx >> 31) & 0x7FFFFFFF), F32)

    def count(sb, nt, cand, strict):
        cb = jnp.broadcast_to(cand, (QB, LANES))

        def tile(j, cnt):
            kk = keys_ref[sb, j]
            for g in range(n_groups):
                kg = kk[:, g * LANES:(g + 1) * LANES]
                cnt = cnt + jnp.where(kg > cb if strict else kg >= cb, 1.0, 0.0)
            return cnt

        cnt = _loop_groups(nt, tile, jnp.zeros((QB, LANES), F32), 4)
        return jnp.sum(cnt, axis=1, keepdims=True)

    def score_block(sb):
        qblk = pl.program_id(1) * SEL_BLOCKS + sb
        nt = (qblk * QB + QB + KT - 1) // KT
        rows = slice(sb * QB, (sb + 1) * QB)
        for h in range(IDX_HEADS):
            qh_ref[h * QB:(h + 1) * QB, :] = qi_ref[rows, h * IDX_DIM:(h + 1) * IDX_DIM]
        wb = [jnp.broadcast_to(w_ref[rows, SM_IW + h:SM_IW + h + 1], (QB, LANES)) for h in range(IDX_HEADS)]
        row_g = qblk * QB + lax.broadcasted_iota(I32, (QB, LANES), 0)

        def score_tile(j, carry, diagonal):
            mx, mn, ge0, gt0 = carry
            lg = jnp.dot(qh_ref[...], kit_ref[j], preferred_element_type=F32)
            acc = [jnp.zeros((QB, LANES), F32) for _ in range(n_groups)]
            for h in range(IDX_HEADS):
                for g in range(n_groups):
                    acc[g] = acc[g] + jnp.maximum(lg[h * QB:(h + 1) * QB, g * LANES:(g + 1) * LANES], 0.0) * wb[h]
            for g in range(n_groups):
                if diagonal:
                    causal = (j * KT + g * LANES + lane) <= row_g
                    sc = jnp.where(causal, acc[g], -jnp.inf)
                    mn = jnp.minimum(mn, jnp.where(causal, acc[g], jnp.inf))
                else:
                    sc = acc[g]
                    mn = jnp.minimum(mn, sc)
                keys_ref[sb, j, :, g * LANES:(g + 1) * LANES] = sc
                mx = jnp.maximum(mx, sc)
                ge0 = ge0 + jnp.where(sc >= 0.0, 1.0, 0.0)
                gt0 = gt0 + jnp.where(sc > 0.0, 1.0, 0.0)
            return mx, mn, ge0, gt0

        zeros = jnp.zeros((QB, LANES), F32)
        carry = _loop_groups(nt - 1, functools.partial(score_tile, diagonal=False),
                             (jnp.full((QB, LANES), -jnp.inf, F32), jnp.full((QB, LANES), jnp.inf, F32),
                              zeros, zeros), 4)
        mx, mn, ge0, gt0 = score_tile(nt - 1, carry, diagonal=True)
        row_max = jnp.max(mx, axis=1, keepdims=True)
        row_min = jnp.min(mn, axis=1, keepdims=True)
        n_ge0 = jnp.sum(ge0, axis=1, keepdims=True)
        n_gt0 = jnp.sum(gt0, axis=1, keepdims=True)
        n_causal = (row_g[:, 0:1] + 1).astype(F32)
        k_eff = jnp.minimum(float(k_sel), n_causal)
        take_all = n_causal <= k_eff
        above = n_gt0 > k_eff
        below = n_ge0 < k_eff
        lo0 = jnp.where(below, to_val(to_key(row_min) - 1), 0.0)
        hi0 = jnp.where(above, to_val(to_key(row_max) + 1), 0.0)
        lo0 = jnp.where(take_all, -FLT_MAX, lo0)
        hi0 = jnp.where(take_all, -FLT_MAX, hi0)
        at_zero = jnp.logical_not(jnp.logical_or(jnp.logical_or(above, below), take_all))
        zero_tied = jnp.logical_and(at_zero, n_ge0 > k_eff)
        return dict(nt=nt, k_eff=k_eff, lo0=lo0, hi0=hi0, zero_tied=zero_tied)

    blocks = [score_block(sb) for sb in range(SEL_BLOCKS)]

    def unsettled(los, his):
        flags = []
        for lo, hi in zip(los, his):
            mid = 0.5 * lo + 0.5 * hi
            settled = jnp.logical_or(lo == hi, jnp.logical_or(mid == lo, mid == hi))
            flags.append(jnp.where(settled, 0.0, 1.0))
        return jnp.max(functools.reduce(jnp.maximum, flags))

    def search_cond(st):
        return jnp.logical_and(st[0] < MAX_SEARCH_STEPS, st[1] > 0.0)

    def search_body(st):
        it, _, los, his = st
        mids = [0.5 * lo + 0.5 * hi for lo, hi in zip(los, his)]
        ns = [count(sb, blocks[sb]["nt"], mids[sb], False) for sb in range(SEL_BLOCKS)]
        los = tuple(jnp.where(ns[sb] >= blocks[sb]["k_eff"], mids[sb], los[sb]) for sb in range(SEL_BLOCKS))
        his = tuple(jnp.where(ns[sb] <= blocks[sb]["k_eff"], mids[sb], his[sb]) for sb in range(SEL_BLOCKS))
        return it + 1, unsettled(los, his), los, his

    los0 = tuple(b["lo0"] for b in blocks)
    his0 = tuple(b["hi0"] for b in blocks)
    _, _, los, his = lax.while_loop(search_cond, search_body, (jnp.int32(0), unsettled(los0, his0), los0, his0))

    def finish_block(sb):
        nt, k_eff = blocks[sb]["nt"], blocks[sb]["k_eff"]
        vstar = los[sb]
        vsb = jnp.broadcast_to(vstar, (QB, KT))
        tied = jnp.logical_or(vstar != his[sb], blocks[sb]["zero_tied"])
        any_tied = jnp.max(jnp.where(tied, 1.0, 0.0)) > 0.0

        @pl.when(any_tied)
        def _():
            need = k_eff - count(sb, nt, vstar, True)
            ustrict = ustrict_ref[...]

            def final_tile(j, carry):
                kk = keys_ref[sb, j]
                tie = kk == vsb
                tie_b = jnp.where(tie, 1.0, 0.0).astype(BF16)
                rank = jnp.dot(tie_b, ustrict, preferred_element_type=F32) + carry
                keep_tie = jnp.where(rank < need, 0.0, NEG)
                bias = jnp.where(kk > vsb, 0.0, jnp.where(tie, keep_tie, NEG))
                bias_ref[sb, j] = bias.astype(BF16)
                return carry + jnp.sum(jnp.where(tie, 1.0, 0.0), axis=1, keepdims=True)

            _loop_groups(nt, final_tile, jnp.zeros((QB, 1), F32))

        @pl.when(jnp.logical_not(any_tied))
        def _():
            def final_tile(j, carry):
                bias_ref[sb, j] = jnp.where(keys_ref[sb, j] >= vsb, 0.0, NEG).astype(BF16)
                return carry

            lax.fori_loop(0, nt, final_tile, 0)

        def fill_tile(j, carry):
            bias_ref[sb, j] = jnp.full((QB, KT), NEG, BF16)
            return carry

        lax.fori_loop(nt, nt_all, fill_tile, 0)

    for sb in range(SEL_BLOCKS):
        finish_block(sb)


def _select(qi, wsm, kit, ustrict, batch, seq, k_sel):
    nq = seq // (QB * SEL_BLOCKS)
    nt = seq // KT
    rows = QB * SEL_BLOCKS
    return pl.pallas_call(
        functools.partial(_select_kernel, k_sel=k_sel),
        grid=(batch, nq),
        in_specs=[pl.BlockSpec((rows, IDX_HEADS * IDX_DIM), lambda b, q: (b * nq + q, 0)),
                  pl.BlockSpec((rows, LANES), lambda b, q: (b * nq + q, 0)),
                  pl.BlockSpec((None, nt, IDX_DIM, KT), lambda b, q: (b, 0, 0, 0)),
                  pl.BlockSpec((KT, KT), lambda b, q: (0, 0))],
        out_specs=pl.BlockSpec((None, SEL_BLOCKS, nt, QB, KT), lambda b, q: (b, q, 0, 0, 0)),
        out_shape=jax.ShapeDtypeStruct((batch, nq * SEL_BLOCKS, nt, QB, KT), BF16),
        scratch_shapes=[pltpu.VMEM((SEL_BLOCKS, nt, QB, KT), F32),
                        pltpu.VMEM((IDX_HEADS * QB, IDX_DIM), BF16)],
        compiler_params=_cparams(("arbitrary", "arbitrary")),
        name="select",
    )(qi, wsm, kit, ustrict)


def _attn_kernel(q_ref, k_ref, v_ref, bias_ref, bound_ref, o_ref, m_ref, l_ref, acc_ref, accw_ref):
    qblk = pl.program_id(1)
    nt = (qblk * QA + QA + KT - 1) // KT
    scale = ATT_HEAD ** -0.5
    q4 = [jnp.concatenate([q_ref[:, (cc * ATT_GROUP + g) * LANES:(cc * ATT_GROUP + g + 1) * LANES]
                           for g in range(ATT_GROUP)], axis=0) for cc in range(ATT_KV_HEADS)]

    def bias_rows(j):
        b = jnp.concatenate([bias_ref[r, j] for r in range(QA // QB)], axis=0).astype(F32)
        return jnp.concatenate([b] * ATT_GROUP, axis=0)

    def write_out(cc, o):
        for g in range(ATT_GROUP):
            h = cc * ATT_GROUP + g
            o_ref[:, h * LANES:(h + 1) * LANES] = o[g * QA:(g + 1) * QA, :].astype(o_ref.dtype)

    safe = bound_ref[0] <= SAFE_LOGIT_BOUND

    @pl.when(safe)
    def _():
        accw_ref[...] = jnp.zeros_like(accw_ref)
        ones = jnp.ones((KT, LANES), BF16)

        def kv_tile(j, carry):
            r0 = pl.multiple_of(j * KT, KT)
            bias4 = bias_rows(j)
            for cc in range(ATT_KV_HEADS):
                kc = k_ref[pl.ds(r0, KT), cc * LANES:(cc + 1) * LANES]
                vc = v_ref[pl.ds(r0, KT), cc * LANES:(cc + 1) * LANES]
                s = lax.dot_general(q4[cc], kc, (((1,), (1,)), ((), ())), preferred_element_type=F32)
                p = jnp.exp2(s * (scale * LOG2E) + bias4)
                accw_ref[cc] += jnp.dot(p.astype(BF16), jnp.concatenate([vc, ones], axis=1),
                                        preferred_element_type=F32)
            return carry

        _loop_groups(nt, kv_tile, 0)
        for cc in range(ATT_KV_HEADS):
            aw = accw_ref[cc]
            write_out(cc, aw[:, :LANES] / aw[:, LANES:])

    @pl.when(jnp.logical_not(safe))
    def _():
        m_ref[...] = jnp.full_like(m_ref, NEG)
        l_ref[...] = jnp.zeros_like(l_ref)
        acc_ref[...] = jnp.zeros_like(acc_ref)

        def kv_tile(j, carry):
            r0 = pl.multiple_of(j * KT, KT)
            bias4 = bias_rows(j)
            for cc in range(ATT_KV_HEADS):
                kc = k_ref[pl.ds(r0, KT), cc * LANES:(cc + 1) * LANES]
                vc = v_ref[pl.ds(r0, KT), cc * LANES:(cc + 1) * LANES]
                s = lax.dot_general(q4[cc], kc, (((1,), (1,)), ((), ())), preferred_element_type=F32)
                s = s * scale + bias4
                m_prev = m_ref[cc]
                m_new = jnp.maximum(m_prev, jnp.max(s, axis=1, keepdims=True))
                alpha = jnp.exp(m_prev - m_new)
                p = jnp.exp(s - m_new)
                l_ref[cc] = alpha * l_ref[cc] + jnp.sum(p, axis=1, keepdims=True)
                acc_ref[cc] = alpha * acc_ref[cc] + jnp.dot(p.astype(BF16), vc, preferred_element_type=F32)
                m_ref[cc] = m_new
            return carry

        lax.fori_loop(0, nt, kv_tile, 0)
        for cc in range(ATT_KV_HEADS):
            write_out(cc, acc_ref[cc] / l_ref[cc])


def _attention(q, k, v, bias, logit_bound, batch, seq):
    t = q.shape[0]
    nq = seq // QA
    nt = seq // KT
    rows = ATT_GROUP * QA
    return pl.pallas_call(
        _attn_kernel,
        grid=(batch, nq),
        in_specs=[pl.BlockSpec((QA, D_MODEL), lambda b, i: (b * nq + i, 0)),
                  pl.BlockSpec((seq, ATT_KV_HEADS * LANES), lambda b, i: (b, 0)),
                  pl.BlockSpec((seq, ATT_KV_HEADS * LANES), lambda b, i: (b, 0)),
                  pl.BlockSpec((None, QA // QB, nt, QB, KT), lambda b, i: (b, i, 0, 0, 0)),
                  pl.BlockSpec(memory_space=pltpu.SMEM)],
        out_specs=pl.BlockSpec((QA, D_MODEL), lambda b, i: (b * nq + i, 0)),
        out_shape=jax.ShapeDtypeStruct((t, D_MODEL), BF16),
        scratch_shapes=[pltpu.VMEM((ATT_KV_HEADS, rows, 1), F32),
                        pltpu.VMEM((ATT_KV_HEADS, rows, 1), F32),
                        pltpu.VMEM((ATT_KV_HEADS, rows, LANES), F32),
                        pltpu.VMEM((ATT_KV_HEADS, rows, 2 * LANES), F32)],
        compiler_params=_cparams(("arbitrary", "arbitrary")),
        name="attention",
    )(q, k, v, bias, logit_bound)


def _merge_kernel(ga_ref, gb_ref, og_ref, oa_ref, x_ref, w_ref, o_ref):
    merged = (jax.nn.sigmoid(ga_ref[...].astype(F32)) * og_ref[...].astype(F32)
              + jax.nn.sigmoid(gb_ref[...].astype(F32)) * oa_ref[...].astype(F32))
    o_ref[...] = x_ref[...] + jnp.dot(merged.astype(BF16), w_ref[...], preferred_element_type=F32)


def _merge(proj, o_gdn, o_att, x2, w_out):
    t = x2.shape[0]
    tm = TM_MERGE
    blk = lambda c: pl.BlockSpec((tm, D_MODEL), lambda i: (i, c))
    return pl.pallas_call(
        _merge_kernel,
        grid=(t // tm,),
        in_specs=[blk(COL_GA * LANES // D_MODEL), blk(COL_GB * LANES // D_MODEL), blk(0), blk(0), blk(0),
                  pl.BlockSpec((D_MODEL, D_MODEL), lambda i: (0, 0))],
        out_specs=blk(0),
        out_shape=jax.ShapeDtypeStruct((t, D_MODEL), F32),
        compiler_params=_cparams(("arbitrary",)),
        name="merge",
    )(proj, proj, o_gdn, o_att, x2, w_out)


def _mlp_kernel(x_ref, g_ref, wu_ref, wd_ref, o_ref, h_ref, acc_ref):
    f = pl.program_id(1)

    @pl.when(f == 0)
    def _():
        x = x_ref[...]
        ms = jnp.mean(x * x, axis=-1, keepdims=True)
        h_ref[...] = (x * lax.rsqrt(ms + EPS) * g_ref[...]).astype(BF16)
        acc_ref[...] = x

    up = jnp.dot(h_ref[...], wu_ref[...], preferred_element_type=F32)
    act = jnp.square(jnp.maximum(up, 0.0))
    acc_ref[...] += jnp.dot(act.astype(BF16), wd_ref[...], preferred_element_type=F32)

    @pl.when(f == pl.num_programs(1) - 1)
    def _():
        o_ref[...] = acc_ref[...]


def _mlp(x1, g_row, w_up, w_down):
    t = x1.shape[0]
    tm = min(TM_MLP, t)
    return pl.pallas_call(
        _mlp_kernel,
        grid=(t // tm, D_FF // TF_MLP),
        in_specs=[pl.BlockSpec((tm, D_MODEL), lambda i, f: (i, 0)),
                  pl.BlockSpec((1, D_MODEL), lambda i, f: (0, 0)),
                  pl.BlockSpec((D_MODEL, TF_MLP), lambda i, f: (0, f)),
                  pl.BlockSpec((TF_MLP, D_MODEL), lambda i, f: (f, 0))],
        out_specs=pl.BlockSpec((tm, D_MODEL), lambda i, f: (i, 0)),
        out_shape=jax.ShapeDtypeStruct((t, D_MODEL), F32),
        scratch_shapes=[pltpu.VMEM((tm, D_MODEL), BF16), pltpu.VMEM((tm, D_MODEL), F32)],
        compiler_params=_cparams(("arbitrary", "arbitrary")),
        name="mlp",
    )(x1, g_row, w_up, w_down)


def _pack_w_in(w_in):
    pts = np.cumsum(np.array(IN_SPLITS))[:-1].tolist()
    (g_qkv, g_z, g_a, g_b, a_q, a_k, a_v, i_q, i_k, i_w, gate_a, gate_b) = jnp.split(w_in, pts, axis=-1)
    small = jnp.concatenate([g_a, g_b, i_k, i_w], axis=-1)
    small = jnp.pad(small, ((0, 0), (0, LANES - small.shape[-1])))
    main = jnp.concatenate([g_qkv, g_z, a_q, a_k, a_v, i_q, gate_a, gate_b], axis=-1)
    return main.astype(BF16), small.astype(BF16)


def _lane_row(vec, offset=0):
    return jnp.zeros((1, LANES), F32).at[0, offset:offset + vec.shape[0]].set(vec.astype(F32))


def _rope_table(seq):
    tab = np.zeros((seq, LANES), np.float64)
    pos = np.arange(seq, dtype=np.float64)[:, None]
    for rot_dim, base in ((ATT_HEAD // ROPE_FRACTION, 0), (IDX_DIM // ROPE_FRACTION, TAB_COS_I)):
        half = rot_dim // 2
        ang = pos * ROPE_THETA ** (-np.arange(0, rot_dim, 2, dtype=np.float64) / rot_dim)[None, :]
        tab[:, base:base + half] = tab[:, base + half:base + 2 * half] = np.cos(ang)
        tab[:, base + 2 * half:base + 3 * half] = -np.sin(ang)
        tab[:, base + 3 * half:base + 4 * half] = np.sin(ang)
    return jnp.asarray(tab, dtype=F32)


def _layer(x2, batch, seq, norm_mix_g, w_in, conv_w, a_log, dt_bias, gdn_norm_g, q_norm_g, k_norm_g,
           w_out, norm_mlp_g, w_mlp_up, w_mlp_down):
    k_sel = min(TOPK_MAX, seq // 4)
    w_main, w_small = _pack_w_in(w_in)
    g_mix = norm_mix_g.reshape(1, D_MODEL)
    proj = _inproj(x2, g_mix, w_main, TN_IN, BF16, "inproj")
    small = _inproj(x2, g_mix, w_small, LANES, F32, "inproj_small")

    o_gdn = _gdn(proj, small, conv_w, _lane_row(a_log, SM_GA), _lane_row(dt_bias, SM_GA),
                 gdn_norm_g.reshape(1, LANES), batch, seq)

    q_att, k_att, v_att, q_idx, k_idx_t, w_idx = _prep(
        proj, small, _rope_table(seq), q_norm_g.reshape(1, LANES), k_norm_g.reshape(1, LANES), batch, seq)

    ii = np.arange(KT)
    ustrict = jnp.asarray((ii[:, None] < ii[None, :]).astype(np.float32), dtype=BF16)
    bias = _select(q_idx, w_idx, k_idx_t, ustrict, batch, seq, k_sel)
    logit_bound = (1.01 * ATT_HEAD * ATT_HEAD ** -0.5) * jnp.max(jnp.abs(q_norm_g)) * jnp.max(jnp.abs(k_norm_g))
    o_att = _attention(q_att, k_att, v_att, bias, logit_bound.reshape(1).astype(F32), batch, seq)

    x1 = _merge(proj, o_gdn, o_att, x2, w_out.astype(BF16))
    return _mlp(x1, norm_mlp_g.reshape(1, D_MODEL), w_mlp_up.astype(BF16), w_mlp_down.astype(BF16))


def kernel(x, norm_mix_g, w_in, conv_w, a_log, dt_bias, gdn_norm_g, q_norm_g, k_norm_g, w_out,
           norm_mlp_g, w_mlp_up, w_mlp_down):
    batch, seq, _ = x.shape
    x2 = x.reshape(batch * seq, D_MODEL)
    for layer in range(norm_mix_g.shape[0]):
        x2 = _layer(x2, batch, seq, norm_mix_g[layer], w_in[layer], conv_w[layer], a_log[layer],
                    dt_bias[layer], gdn_norm_g[layer], q_norm_g[layer], k_norm_g[layer], w_out[layer],
                    norm_mlp_g[layer], w_mlp_up[layer], w_mlp_down[layer])
    return x2.reshape(batch, seq, D_MODEL)
```

```python
import functools
import math

import numpy as np
import jax
import jax.numpy as jnp
from jax import lax
from jax.experimental import pallas as pl
from jax.experimental.pallas import tpu as pltpu

F32 = jnp.float32
BF16 = jnp.bfloat16
I32 = jnp.int32

D_MODEL = 1024
GDN_HEADS = 8
GDN_HEAD = 128
GDN_CONV = 4
GDN_CHUNK = 64
GDN_CONV_DIM = 3 * GDN_HEADS * GDN_HEAD
ATT_HEADS = 8
ATT_HEAD = 128
ATT_KV_HEADS = 2
ATT_GROUP = ATT_HEADS // ATT_KV_HEADS
IDX_HEADS = 8
IDX_DIM = 64
TOPK_MAX = 256
ROPE_THETA = 500000.0
ROPE_FRACTION = 4
D_FF = 4 * D_MODEL
EPS = 1e-6

IN_SPLITS = (GDN_CONV_DIM, D_MODEL, GDN_HEADS, GDN_HEADS, D_MODEL, 256, 256,
             IDX_HEADS * IDX_DIM, IDX_DIM, IDX_HEADS, D_MODEL, D_MODEL)

LANES = 128
assert 2 * GDN_CHUNK == LANES
COL_QKV = 0
COL_Z = 24
COL_AQ = 32
COL_AKV = 40
COL_IQ = 44
COL_GA = 48
COL_GB = 56
SM_GA = 0
SM_GB = 8
SM_IK = 16
SM_IW = 80
TAB_SIN_A = 2 * (ATT_HEAD // ROPE_FRACTION // 2)
TAB_COS_I = 4 * (ATT_HEAD // ROPE_FRACTION // 2)
TAB_SIN_I = TAB_COS_I + 2 * (IDX_DIM // ROPE_FRACTION // 2)

NEG = -1e30
FLT_MAX = 3.4028234663852886e38
MAX_SEARCH_STEPS = 254 + 24 + 8
LOG2E = 1.4426950408889634
SAFE_LOGIT_BOUND = 60.0

TM_IN = 1024
TN_IN = 16 * LANES
GDN_BLK = 512
GDN_CHUNKS_PER_STEP = 8
TM_PREP = 512
QB = 128
SEL_BLOCKS = 1
QA = 256
KT = 512
assert KT % QB == 0
TM_MERGE = 512
TM_MLP = 1024
TF_MLP = 512
VMEM_LIMIT = 56 * 2 ** 20


def _cparams(sem):
    return pltpu.CompilerParams(dimension_semantics=sem, vmem_limit_bytes=VMEM_LIMIT)


def _loop_groups(n, body, init, group=2):
    shift = group.bit_length() - 1

    def many(i, carry):
        for s in range(group):
            carry = body(group * i + s, carry)
        return carry

    n_groups = lax.shift_right_logical(n, shift)
    carry = lax.fori_loop(0, n_groups, many, init)
    return lax.fori_loop(lax.shift_left(n_groups, shift), n, body, carry)


def _inproj_kernel(x_ref, g_ref, w_ref, o_ref):
    x = x_ref[...]
    ms = jnp.mean(x * x, axis=-1, keepdims=True)
    h = x * lax.rsqrt(ms + EPS) * g_ref[...]
    o_ref[...] = jnp.dot(h.astype(BF16), w_ref[...], preferred_element_type=F32).astype(o_ref.dtype)


def _inproj(x2, g_row, w_p, tn, out_dtype, name):
    t = x2.shape[0]
    n_cols = w_p.shape[1]
    return pl.pallas_call(
        _inproj_kernel,
        grid=(n_cols // tn, t // TM_IN),
        in_specs=[pl.BlockSpec((TM_IN, D_MODEL), lambda n, m: (m, 0)),
                  pl.BlockSpec((1, D_MODEL), lambda n, m: (0, 0)),
                  pl.BlockSpec((D_MODEL, tn), lambda n, m: (0, n))],
        out_specs=pl.BlockSpec((TM_IN, tn), lambda n, m: (m, n)),
        out_shape=jax.ShapeDtypeStruct((t, n_cols), out_dtype),
        compiler_params=_cparams(("arbitrary", "arbitrary")),
        name=name,
    )(x2, g_row, w_p)


def _gdn_kernel(qkv_ref, z_ref, sm_ref, cw_ref, alog_ref, dtb_ref, gn_ref, o_ref,
                halo_ref, state_ref, qn_ref, kn_ref, vv_ref, gc_ref, bt_ref):
    blk = qkv_ref.shape[0]
    c = GDN_CHUNK

    @pl.when(pl.program_id(1) == 0)
    def _():
        halo_ref[...] = jnp.zeros_like(halo_ref)
        state_ref[...] = jnp.zeros_like(state_ref)

    row8 = lax.broadcasted_iota(I32, (8, LANES), 0)
    for j in range(GDN_CONV_DIM // LANES):
        lo = j * LANES
        u = qkv_ref[:, lo:lo + LANES].astype(F32)
        hal = halo_ref[:, lo:lo + LANES]
        w = cw_ref[:, lo:lo + LANES]
        y = u * w[GDN_CONV - 1:GDN_CONV, :]
        for sft in range(1, GDN_CONV):
            ur = pltpu.roll(u, sft, axis=0)
            hr = pltpu.roll(hal, sft, axis=0)
            top = jnp.where(row8 < sft, hr, ur[0:8, :])
            ush = jnp.concatenate([top, ur[8:, :]], axis=0)
            y = y + ush * w[GDN_CONV - 1 - sft:GDN_CONV - sft, :]
        halo_ref[:, lo:lo + LANES] = u[blk - 8:, :]
        a = y * jax.nn.sigmoid(y)
        head = j % GDN_HEADS
        if j < 2 * GDN_HEADS:
            a = a * lax.rsqrt(jnp.sum(a * a, axis=-1, keepdims=True) + EPS)
        if j < GDN_HEADS:
            qn_ref[:, head * LANES:(head + 1) * LANES] = a * (GDN_HEAD ** -0.5)
        elif j < 2 * GDN_HEADS:
            kn_ref[:, head * LANES:(head + 1) * LANES] = a
        else:
            vv_ref[:, head * LANES:(head + 1) * LANES] = a

    sm = sm_ref[...]
    xg = sm + dtb_ref[...]
    softplus = jnp.maximum(xg, 0.0) + jnp.log1p(jnp.exp(-jnp.abs(xg)))
    gc = -jnp.exp(alog_ref[...]) * softplus
    row_in_chunk = lax.broadcasted_iota(I32, (blk, LANES), 0) & (c - 1)
    d = 1
    while d < c:
        gc = gc + jnp.where(row_in_chunk >= d, pltpu.roll(gc, d, axis=0), 0.0)
        d *= 2
    gc_ref[...] = gc
    bt_ref[...] = jax.nn.sigmoid(sm)

    ii = lax.broadcasted_iota(I32, (c, 2 * c), 0)
    lane_cc = lax.broadcasted_iota(I32, (c, 2 * c), 1)
    jj = lane_cc & (c - 1)
    incl = ii >= jj
    strict = ii > jj
    left = lane_cc < c
    left_row = lax.broadcasted_iota(I32, (1, 2 * c), 1) < c
    gn = gn_ref[...]

    def chunk_body(ci, carry):
        nch = GDN_CHUNKS_PER_STEP
        r0 = [pl.multiple_of((ci * nch + s) * c, c) for s in range(nch)]
        items = [(s, h) for s in range(nch) for h in range(GDN_HEADS)]
        n_items = len(items)
        pairs = range(n_items // 2)
        hs = [slice(h * LANES, (h + 1) * LANES) for h in range(GDN_HEADS)]
        half = lambda i: slice((i % 2) * LANES, (i % 2 + 1) * LANES)
        mm = lambda a, b: jnp.dot(a.astype(BF16), b.astype(BF16), preferred_element_type=F32)

        def bdiag(xa, xb):
            return jnp.concatenate([jnp.concatenate([xa, jnp.zeros_like(xb)], axis=1),
                                    jnp.concatenate([jnp.zeros_like(xa), xb], axis=1)], axis=0)

        def bdiag2(x2):
            return jnp.concatenate([jnp.where(left, x2, 0.0), jnp.where(left, 0.0, x2)], axis=0)

        def side(xa, xb, axis):
            return jnp.concatenate([xa, xb], axis=axis)

        gcc = [gc_ref[pl.ds(r0[s], c), :] for s in range(nch)]
        btc = [bt_ref[pl.ds(r0[s], c), :] for s in range(nch)]
        gct2 = [jnp.concatenate([g, g], axis=0).T for g in gcc]
        gcol = [gcc[s][:, SM_GA + h:SM_GA + h + 1] for s, h in items]
        glast = [gcc[s][c - 1:c, SM_GA + h:SM_GA + h + 1] for s, h in items]
        bcol = [btc[s][:, SM_GB + h:SM_GB + h + 1] for s, h in items]
        q = [qn_ref[pl.ds(r0[s], c), hs[h]] for s, h in items]
        k = [kn_ref[pl.ds(r0[s], c), hs[h]] for s, h in items]
        v = [vv_ref[pl.ds(r0[s], c), hs[h]] for s, h in items]
        kb = [k[i] * bcol[i] for i in range(n_items)]
        eg = [jnp.exp(gcol[i]) for i in range(n_items)]
        rhs = [jnp.concatenate([v[i] * bcol[i], kb[i] * eg[i]], axis=1) for i in range(n_items)]
        ab = [lax.dot_general(
            side(side(kb[2 * p], q[2 * p], 0), side(kb[2 * p + 1], q[2 * p + 1], 0), 1).astype(BF16),
            bdiag(k[2 * p], k[2 * p + 1]).astype(BF16),
            (((1,), (1,)), ((), ())), preferred_element_type=F32) for p in pairs]
        decay = []
        for p in pairs:
            s, h = items[2 * p]
            ha, hb = SM_GA + h, SM_GA + h + 1
            diff = (jnp.where(left, gcol[2 * p], gcol[2 * p + 1])
                    - jnp.where(left_row, gct2[s][ha:ha + 1, :], gct2[s][hb:hb + 1, :]))
            decay.append(jnp.where(incl, jnp.exp(jnp.where(incl, diff, 0.0)), 0.0))
        qk = [jnp.where(incl, ab[p][c:, :] * decay[p], 0.0) for p in pairs]
        pw = [-jnp.where(strict, ab[p][0:c, :] * decay[p], 0.0) for p in pairs]
        r = list(pw)
        for _ in range(int(math.log2(c)) - 1):
            pw = [mm(pw[p], bdiag2(pw[p])) for p in pairs]
            rp = [mm(r[p], bdiag2(pw[p])) for p in pairs]
            r = [r[p] + pw[p] + rp[p] for p in pairs]
        rr = [mm(r[p], bdiag(rhs[2 * p], rhs[2 * p + 1])) for p in pairs]
        sol = [rhs[i] + rr[i // 2][:, (i % 2) * 2 * LANES:(i % 2 + 1) * 2 * LANES] for i in range(n_items)]
        q_dec = [q[i] * eg[i] for i in range(n_items)]
        k_dec_t = [(k[i] * jnp.exp(glast[i] - gcol[i])).T for i in range(n_items)]
        st = [state_ref[h] for h in range(GDN_HEADS)]
        for s in range(nch):
            base = s * GDN_HEADS
            hp = range(GDN_HEADS // 2)
            ws = [mm(side(side(sol[base + 2 * p][:, LANES:], q_dec[base + 2 * p], 0),
                          side(sol[base + 2 * p + 1][:, LANES:], q_dec[base + 2 * p + 1], 0), 1),
                     bdiag(st[2 * p], st[2 * p + 1])) for p in hp]
            v_new = [sol[base + h][:, :LANES] - ws[h // 2][0:c, half(h)] for h in range(GDN_HEADS)]
            qv = [mm(qk[base // 2 + p], bdiag(v_new[2 * p], v_new[2 * p + 1])) for p in hp]
            o_n = [ws[h // 2][c:, half(h)] + qv[h // 2][:, half(h)] for h in range(GDN_HEADS)]
            st = [st[h] * jnp.exp(glast[base + h]) + mm(k_dec_t[base + h], v_new[h]) for h in range(GDN_HEADS)]
            for h in range(GDN_HEADS):
                on = o_n[h] * lax.rsqrt(jnp.mean(o_n[h] * o_n[h], axis=-1, keepdims=True) + EPS) * gn
                zz = z_ref[pl.ds(r0[s], c), hs[h]].astype(F32)
                o_ref[pl.ds(r0[s], c), hs[h]] = (on * (zz * jax.nn.sigmoid(zz))).astype(o_ref.dtype)
        for h in range(GDN_HEADS):
            state_ref[h] = st[h]
        return carry

    lax.fori_loop(0, blk // (c * GDN_CHUNKS_PER_STEP), chunk_body, 0)


def _gdn(proj, small, conv_w, alog_row, dtb_row, gn_row, batch, seq):
    t = proj.shape[0]
    blk = min(GDN_BLK, seq)
    ns = seq // blk
    row = lambda b, s: b * ns + s
    return pl.pallas_call(
        _gdn_kernel,
        grid=(batch, ns),
        in_specs=[pl.BlockSpec((blk, GDN_CONV_DIM), lambda b, s: (row(b, s), COL_QKV * LANES // GDN_CONV_DIM)),
                  pl.BlockSpec((blk, D_MODEL), lambda b, s: (row(b, s), COL_Z * LANES // D_MODEL)),
                  pl.BlockSpec((blk, LANES), lambda b, s: (row(b, s), 0)),
                  pl.BlockSpec((GDN_CONV, GDN_CONV_DIM), lambda b, s: (0, 0)),
                  pl.BlockSpec((1, LANES), lambda b, s: (0, 0)),
                  pl.BlockSpec((1, LANES), lambda b, s: (0, 0)),
                  pl.BlockSpec((1, LANES), lambda b, s: (0, 0))],
        out_specs=pl.BlockSpec((blk, D_MODEL), lambda b, s: (row(b, s), 0)),
        out_shape=jax.ShapeDtypeStruct((t, D_MODEL), BF16),
        scratch_shapes=[pltpu.VMEM((8, GDN_CONV_DIM), F32),
                        pltpu.VMEM((GDN_HEADS, GDN_HEAD, GDN_HEAD), F32),
                        pltpu.VMEM((blk, D_MODEL), F32),
                        pltpu.VMEM((blk, D_MODEL), F32),
                        pltpu.VMEM((blk, D_MODEL), F32),
                        pltpu.VMEM((blk, LANES), F32),
                        pltpu.VMEM((blk, LANES), F32)],
        compiler_params=_cparams(("arbitrary", "arbitrary")),
        name="gdn",
    )(proj, proj, small, conv_w, alog_row, dtb_row, gn_row)


def _rope(y, cos, sin_lo, sin_hi, half):
    return y * cos + pltpu.roll(y, LANES - half, axis=1) * sin_lo + pltpu.roll(y, half, axis=1) * sin_hi


def _prep_kernel(aq_ref, akv_ref, iq_ref, sm_ref, tab_ref, gq_ref, gk_ref,
                 q_out, k_out, v_out, qi_out, kit_out, w_out):
    half_a = ATT_HEAD // ROPE_FRACTION // 2
    half_i = IDX_DIM // ROPE_FRACTION // 2
    t = tab_ref[...]
    lane = lax.broadcasted_iota(I32, t.shape, 1)
    from_lane = lambda src: pltpu.roll(t, (-src) % LANES, axis=1)
    between = lambda lo, hi: (lane >= lo) & (lane < hi)
    ca = jnp.where(lane < 2 * half_a, t, 1.0)
    sa = from_lane(TAB_SIN_A)
    sa1 = jnp.where(lane < half_a, sa, 0.0)
    sa2 = jnp.where(between(half_a, 2 * half_a), sa, 0.0)
    lm = lane & (IDX_DIM - 1)
    low = lane < IDX_DIM
    ci = jnp.where(lm < 2 * half_i, jnp.where(low, from_lane(TAB_COS_I), from_lane(TAB_COS_I - IDX_DIM)), 1.0)
    si = jnp.where(low, from_lane(TAB_SIN_I), from_lane(TAB_SIN_I - IDX_DIM))
    si1 = jnp.where(lm < half_i, si, 0.0)
    si2 = jnp.where((lm >= half_i) & (lm < 2 * half_i), si, 0.0)
    ck = jnp.where(between(SM_IK, SM_IK + 2 * half_i), from_lane(TAB_COS_I - SM_IK), 1.0)
    sk = from_lane(TAB_SIN_I - SM_IK)
    sk1 = jnp.where(between(SM_IK, SM_IK + half_i), sk, 0.0)
    sk2 = jnp.where(between(SM_IK + half_i, SM_IK + 2 * half_i), sk, 0.0)
    gq = gq_ref[...]
    gk = gk_ref[...]
    for h in range(ATT_HEADS):
        hs = slice(h * LANES, (h + 1) * LANES)
        xh = aq_ref[:, hs].astype(F32)
        y = xh * lax.rsqrt(jnp.mean(xh * xh, axis=-1, keepdims=True) + EPS) * gq
        q_out[:, hs] = _rope(y, ca, sa1, sa2, half_a).astype(BF16)
    for cc in range(ATT_KV_HEADS):
        hs = slice(cc * LANES, (cc + 1) * LANES)
        xk = akv_ref[:, hs].astype(F32)
        y = xk * lax.rsqrt(jnp.mean(xk * xk, axis=-1, keepdims=True) + EPS) * gk
        k_out[:, hs] = _rope(y, ca, sa1, sa2, half_a).astype(BF16)
        v_out[:, hs] = akv_ref[:, ATT_KV_HEADS * LANES + cc * LANES:ATT_KV_HEADS * LANES + (cc + 1) * LANES].astype(BF16)
    for j in range(IDX_HEADS * IDX_DIM // LANES):
        hs = slice(j * LANES, (j + 1) * LANES)
        qi_out[:, hs] = _rope(iq_ref[:, hs].astype(F32), ci, si1, si2, half_i).astype(BF16)
    sm = sm_ref[...]
    smr = _rope(sm, ck, sk1, sk2, half_i)
    kit_out[...] = smr.T[SM_IK:SM_IK + IDX_DIM, :].astype(BF16)
    w_out[...] = sm * (IDX_HEADS ** -0.5 * IDX_DIM ** -0.5)


def _prep(proj, small, table, gq_row, gk_row, batch, seq):
    t = proj.shape[0]
    tm = min(TM_PREP, seq)
    nsb = seq // tm
    tab_spec = pl.BlockSpec((tm, LANES), lambda i: (i % nsb, 0))
    row_spec = pl.BlockSpec((1, LANES), lambda i: (0, 0))
    return pl.pallas_call(
        _prep_kernel,
        grid=(t // tm,),
        in_specs=[pl.BlockSpec((tm, D_MODEL), lambda i: (i, COL_AQ * LANES // D_MODEL)),
                  pl.BlockSpec((tm, 4 * LANES), lambda i: (i, COL_AKV // 4)),
                  pl.BlockSpec((tm, 4 * LANES), lambda i: (i, COL_IQ // 4)),
                  pl.BlockSpec((tm, LANES), lambda i: (i, 0))]
                 + [tab_spec, row_spec, row_spec],
        out_specs=[pl.BlockSpec((tm, D_MODEL), lambda i: (i, 0)),
                   pl.BlockSpec((tm, ATT_KV_HEADS * LANES), lambda i: (i, 0)),
                   pl.BlockSpec((tm, ATT_KV_HEADS * LANES), lambda i: (i, 0)),
                   pl.BlockSpec((tm, IDX_HEADS * IDX_DIM), lambda i: (i, 0)),
                   pl.BlockSpec((None, None, IDX_DIM, tm), lambda i: (i // nsb, i % nsb, 0, 0)),
                   pl.BlockSpec((tm, LANES), lambda i: (i, 0))],
        out_shape=[jax.ShapeDtypeStruct((t, D_MODEL), BF16),
                   jax.ShapeDtypeStruct((t, ATT_KV_HEADS * LANES), BF16),
                   jax.ShapeDtypeStruct((t, ATT_KV_HEADS * LANES), BF16),
                   jax.ShapeDtypeStruct((t, IDX_HEADS * IDX_DIM), BF16),
                   jax.ShapeDtypeStruct((batch, nsb, IDX_DIM, tm), BF16),
                   jax.ShapeDtypeStruct((t, LANES), F32)],
        compiler_params=_cparams(("arbitrary",)),
        name="prep",
    )(proj, proj, proj, small, table, gq_row, gk_row)


def _select_kernel(qi_ref, w_ref, kit_ref, ustrict_ref, bias_ref, keys_ref, qh_ref, *, k_sel):
    nt_all = keys_ref.shape[1]
    n_groups = KT // LANES
    lane = lax.broadcasted_iota(I32, (QB, LANES), 1)
    to_key = lambda x: (lambda b: b ^ ((b >> 31) & 0x7FFFFFFF))(pltpu.bitcast(x, I32))
    to_val = lambda kx: pltpu.bitcast(kx ^ ((kx >> 31) & 0x7FFFFFFF), F32)

    def count(sb, nt, cand, strict):
        cb = jnp.broadcast_to(cand, (QB, LANES))

        def tile(j, cnt):
            kk = keys_ref[sb, j]
            for g in range(n_groups):
                kg = kk[:, g * LANES:(g + 1) * LANES]
                cnt = cnt + jnp.where(kg > cb if strict else kg >= cb, 1.0, 0.0)
            return cnt

        cnt = _loop_groups(nt, tile, jnp.zeros((QB, LANES), F32), 4)
        return jnp.sum(cnt, axis=1, keepdims=True)

    def score_block(sb):
        qblk = pl.program_id(1) * SEL_BLOCKS + sb
        nt = (qblk * QB + QB + KT - 1) // KT
        rows = slice(sb * QB, (sb + 1) * QB)
        for h in range(IDX_HEADS):
            qh_ref[h * QB:(h + 1) * QB, :] = qi_ref[rows, h * IDX_DIM:(h + 1) * IDX_DIM]
        wb = [jnp.broadcast_to(w_ref[rows, SM_IW + h:SM_IW + h + 1], (QB, LANES)) for h in range(IDX_HEADS)]
        row_g = qblk * QB + lax.broadcasted_iota(I32, (QB, LANES), 0)

        def score_tile(j, carry, diagonal):
            mx, mn, ge0, gt0 = carry
            lg = jnp.dot(qh_ref[...], kit_ref[j], preferred_element_type=F32)
            acc = [jnp.zeros((QB, LANES), F32) for _ in range(n_groups)]
            for h in range(IDX_HEADS):
                for g in range(n_groups):
                    acc[g] = acc[g] + jnp.maximum(lg[h * QB:(h + 1) * QB, g * LANES:(g + 1) * LANES], 0.0) * wb[h]
            for g in range(n_groups):
                if diagonal:
                    causal = (j * KT + g * LANES + lane) <= row_g
                    sc = jnp.where(causal, acc[g], -jnp.inf)
                    mn = jnp.minimum(mn, jnp.where(causal, acc[g], jnp.inf))
                else:
                    sc = acc[g]
                    mn = jnp.minimum(mn, sc)
                keys_ref[sb, j, :, g * LANES:(g + 1) * LANES] = sc
                mx = jnp.maximum(mx, sc)
                ge0 = ge0 + jnp.where(sc >= 0.0, 1.0, 0.0)
                gt0 = gt0 + jnp.where(sc > 0.0, 1.0, 0.0)
            return mx, mn, ge0, gt0

        zeros = jnp.zeros((QB, LANES), F32)
        carry = _loop_groups(nt - 1, functools.partial(score_tile, diagonal=False),
                             (jnp.full((QB, LANES), -jnp.inf, F32), jnp.full((QB, LANES), jnp.inf, F32),
                              zeros, zeros), 4)
        mx, mn, ge0, gt0 = score_tile(nt - 1, carry, diagonal=True)
        row_max = jnp.max(mx, axis=1, keepdims=True)
        row_min = jnp.min(mn, axis=1, keepdims=True)
        n_ge0 = jnp.sum(ge0, axis=1, keepdims=True)
        n_gt0 = jnp.sum(gt0, axis=1, keepdims=True)
        n_causal = (row_g[:, 0:1] + 1).astype(F32)
        k_eff = jnp.minimum(float(k_sel), n_causal)
        take_all = n_causal <= k_eff
        above = n_gt0 > k_eff
        below = n_ge0 < k_eff
        lo0 = jnp.where(below, to_val(to_key(row_min) - 1), 0.0)
        hi0 = jnp.where(above, to_val(to_key(row_max) + 1), 0.0)
        lo0 = jnp.where(take_all, -FLT_MAX, lo0)
        hi0 = jnp.where(take_all, -FLT_MAX, hi0)
        at_zero = jnp.logical_not(jnp.logical_or(jnp.logical_or(above, below), take_all))
        zero_tied = jnp.logical_and(at_zero, n_ge0 > k_eff)
        return dict(nt=nt, k_eff=k_eff, lo0=lo0, hi0=hi0, zero_tied=zero_tied)

    blocks = [score_block(sb) for sb in range(SEL_BLOCKS)]

    def unsettled(los, his):
        flags = []
        for lo, hi in zip(los, his):
            mid = 0.5 * lo + 0.5 * hi
            settled = jnp.logical_or(lo == hi, jnp.logical_or(mid == lo, mid == hi))
            flags.append(jnp.where(settled, 0.0, 1.0))
        return jnp.max(functools.reduce(jnp.maximum, flags))

    def search_cond(st):
        return jnp.logical_and(st[0] < MAX_SEARCH_STEPS, st[1] > 0.0)

    def search_body(st):
        it, _, los, his = st
        mids = [0.5 * lo + 0.5 * hi for lo, hi in zip(los, his)]
        ns = [count(sb, blocks[sb]["nt"], mids[sb], False) for sb in range(SEL_BLOCKS)]
        los = tuple(jnp.where(ns[sb] >= blocks[sb]["k_eff"], mids[sb], los[sb]) for sb in range(SEL_BLOCKS))
        his = tuple(jnp.where(ns[sb] <= blocks[sb]["k_eff"], mids[sb], his[sb]) for sb in range(SEL_BLOCKS))
        return it + 1, unsettled(los, his), los, his

    los0 = tuple(b["lo0"] for b in blocks)
    his0 = tuple(b["hi0"] for b in blocks)
    _, _, los, his = lax.while_loop(search_cond, search_body, (jnp.int32(0), unsettled(los0, his0), los0, his0))

    def finish_block(sb):
        nt, k_eff = blocks[sb]["nt"], blocks[sb]["k_eff"]
        vstar = los[sb]
        vsb = jnp.broadcast_to(vstar, (QB, KT))
        tied = jnp.logical_or(vstar != his[sb], blocks[sb]["zero_tied"])
        any_tied = jnp.max(jnp.where(tied, 1.0, 0.0)) > 0.0

        @pl.when(any_tied)
        def _():
            need = k_eff - count(sb, nt, vstar, True)
            ustrict = ustrict_ref[...]

            def final_tile(j, carry):
                kk = keys_ref[sb, j]
                tie = kk == vsb
                tie_b = jnp.where(tie, 1.0, 0.0).astype(BF16)
                rank = jnp.dot(tie_b, ustrict, preferred_element_type=F32) + carry
                keep_tie = jnp.where(rank < need, 0.0, NEG)
                bias = jnp.where(kk > vsb, 0.0, jnp.where(tie, keep_tie, NEG))
                bias_ref[sb, j] = bias.astype(BF16)
                return carry + jnp.sum(jnp.where(tie, 1.0, 0.0), axis=1, keepdims=True)

            _loop_groups(nt, final_tile, jnp.zeros((QB, 1), F32))

        @pl.when(jnp.logical_not(any_tied))
        def _():
            def final_tile(j, carry):
                bias_ref[sb, j] = jnp.where(keys_ref[sb, j] >= vsb, 0.0, NEG).astype(BF16)
                return carry

            lax.fori_loop(0, nt, final_tile, 0)

        def fill_tile(j, carry):
            bias_ref[sb, j] = jnp.full((QB, KT), NEG, BF16)
            return carry

        lax.fori_loop(nt, nt_all, fill_tile, 0)

    for sb in range(SEL_BLOCKS):
        finish_block(sb)


def _select(qi, wsm, kit, ustrict, batch, seq, k_sel):
    nq = seq // (QB * SEL_BLOCKS)
    nt = seq // KT
    rows = QB * SEL_BLOCKS
    return pl.pallas_call(
        functools.partial(_select_kernel, k_sel=k_sel),
        grid=(batch, nq),
        in_specs=[pl.BlockSpec((rows, IDX_HEADS * IDX_DIM), lambda b, q: (b * nq + q, 0)),
                  pl.BlockSpec((rows, LANES), lambda b, q: (b * nq + q, 0)),
                  pl.BlockSpec((None, nt, IDX_DIM, KT), lambda b, q: (b, 0, 0, 0)),
                  pl.BlockSpec((KT, KT), lambda b, q: (0, 0))],
        out_specs=pl.BlockSpec((None, SEL_BLOCKS, nt, QB, KT), lambda b, q: (b, q, 0, 0, 0)),
        out_shape=jax.ShapeDtypeStruct((batch, nq * SEL_BLOCKS, nt, QB, KT), BF16),
        scratch_shapes=[pltpu.VMEM((SEL_BLOCKS, nt, QB, KT), F32),
                        pltpu.VMEM((IDX_HEADS * QB, IDX_DIM), BF16)],
        compiler_params=_cparams(("arbitrary", "arbitrary")),
        name="select",
    )(qi, wsm, kit, ustrict)


def _attn_kernel(q_ref, k_ref, v_ref, bias_ref, bound_ref, o_ref, m_ref, l_ref, acc_ref, accw_ref):
    qblk = pl.program_id(1)
    nt = (qblk * QA + QA + KT - 1) // KT
    scale = ATT_HEAD ** -0.5
    q4 = [jnp.concatenate([q_ref[:, (cc * ATT_GROUP + g) * LANES:(cc * ATT_GROUP + g + 1) * LANES]
                           for g in range(ATT_GROUP)], axis=0) for cc in range(ATT_KV_HEADS)]

    def bias_rows(j):
        b = jnp.concatenate([bias_ref[r, j] for r in range(QA // QB)], axis=0).astype(F32)
        return jnp.concatenate([b] * ATT_GROUP, axis=0)

    def write_out(cc, o):
        for g in range(ATT_GROUP):
            h = cc * ATT_GROUP + g
            o_ref[:, h * LANES:(h + 1) * LANES] = o[g * QA:(g + 1) * QA, :].astype(o_ref.dtype)

    safe = bound_ref[0] <= SAFE_LOGIT_BOUND

    @pl.when(safe)
    def _():
        accw_ref[...] = jnp.zeros_like(accw_ref)
        ones = jnp.ones((KT, LANES), BF16)

        def kv_tile(j, carry):
            r0 = pl.multiple_of(j * KT, KT)
            bias4 = bias_rows(j)
            for cc in range(ATT_KV_HEADS):
                kc = k_ref[pl.ds(r0, KT), cc * LANES:(cc + 1) * LANES]
                vc = v_ref[pl.ds(r0, KT), cc * LANES:(cc + 1) * LANES]
                s = lax.dot_general(q4[cc], kc, (((1,), (1,)), ((), ())), preferred_element_type=F32)
                p = jnp.exp2(s * (scale * LOG2E) + bias4)
                accw_ref[cc] += jnp.dot(p.astype(BF16), jnp.concatenate([vc, ones], axis=1),
                                        preferred_element_type=F32)
            return carry

        _loop_groups(nt, kv_tile, 0, 4)
        for cc in range(ATT_KV_HEADS):
            aw = accw_ref[cc]
            write_out(cc, aw[:, :LANES] / aw[:, LANES:])

    @pl.when(jnp.logical_not(safe))
    def _():
        m_ref[...] = jnp.full_like(m_ref, NEG)
        l_ref[...] = jnp.zeros_like(l_ref)
        acc_ref[...] = jnp.zeros_like(acc_ref)

        def kv_tile(j, carry):
            r0 = pl.multiple_of(j * KT, KT)
            bias4 = bias_rows(j)
            for cc in range(ATT_KV_HEADS):
                kc = k_ref[pl.ds(r0, KT), cc * LANES:(cc + 1) * LANES]
                vc = v_ref[pl.ds(r0, KT), cc * LANES:(cc + 1) * LANES]
                s = lax.dot_general(q4[cc], kc, (((1,), (1,)), ((), ())), preferred_element_type=F32)
                s = s * scale + bias4
                m_prev = m_ref[cc]
                m_new = jnp.maximum(m_prev, jnp.max(s, axis=1, keepdims=True))
                alpha = jnp.exp(m_prev - m_new)
                p = jnp.exp(s - m_new)
                l_ref[cc] = alpha * l_ref[cc] + jnp.sum(p, axis=1, keepdims=True)
                acc_ref[cc] = alpha * acc_ref[cc] + jnp.dot(p.astype(BF16), vc, preferred_element_type=F32)
                m_ref[cc] = m_new
            return carry

        lax.fori_loop(0, nt, kv_tile, 0)
        for cc in range(ATT_KV_HEADS):
            write_out(cc, acc_ref[cc] / l_ref[cc])


def _attention(q, k, v, bias, logit_bound, batch, seq):
    t = q.shape[0]
    nq = seq // QA
    nt = seq // KT
    rows = ATT_GROUP * QA
    return pl.pallas_call(
        _attn_kernel,
        grid=(batch, nq),
        in_specs=[pl.BlockSpec((QA, D_MODEL), lambda b, i: (b * nq + i, 0)),
                  pl.BlockSpec((seq, ATT_KV_HEADS * LANES), lambda b, i: (b, 0)),
                  pl.BlockSpec((seq, ATT_KV_HEADS * LANES), lambda b, i: (b, 0)),
                  pl.BlockSpec((None, QA // QB, nt, QB, KT), lambda b, i: (b, i, 0, 0, 0)),
                  pl.BlockSpec(memory_space=pltpu.SMEM)],
        out_specs=pl.BlockSpec((QA, D_MODEL), lambda b, i: (b * nq + i, 0)),
        out_shape=jax.ShapeDtypeStruct((t, D_MODEL), BF16),
        scratch_shapes=[pltpu.VMEM((ATT_KV_HEADS, rows, 1), F32),
                        pltpu.VMEM((ATT_KV_HEADS, rows, 1), F32),
                        pltpu.VMEM((ATT_KV_HEADS, rows, LANES), F32),
                        pltpu.VMEM((ATT_KV_HEADS, rows, 2 * LANES), F32)],
        compiler_params=_cparams(("arbitrary", "arbitrary")),
        name="attention",
    )(q, k, v, bias, logit_bound)


def _merge_kernel(ga_ref, gb_ref, og_ref, oa_ref, x_ref, w_ref, o_ref):
    merged = (jax.nn.sigmoid(ga_ref[...].astype(F32)) * og_ref[...].astype(F32)
              + jax.nn.sigmoid(gb_ref[...].astype(F32)) * oa_ref[...].astype(F32))
    o_ref[...] = x_ref[...] + jnp.dot(merged.astype(BF16), w_ref[...], preferred_element_type=F32)


def _merge(proj, o_gdn, o_att, x2, w_out):
    t = x2.shape[0]
    tm = TM_MERGE
    blk = lambda c: pl.BlockSpec((tm, D_MODEL), lambda i: (i, c))
    return pl.pallas_call(
        _merge_kernel,
        grid=(t // tm,),
        in_specs=[blk(COL_GA * LANES // D_MODEL), blk(COL_GB * LANES // D_MODEL), blk(0), blk(0), blk(0),
                  pl.BlockSpec((D_MODEL, D_MODEL), lambda i: (0, 0))],
        out_specs=blk(0),
        out_shape=jax.ShapeDtypeStruct((t, D_MODEL), F32),
        compiler_params=_cparams(("arbitrary",)),
        name="merge",
    )(proj, proj, o_gdn, o_att, x2, w_out)


def _mlp_kernel(x_ref, g_ref, wu_ref, wd_ref, o_ref, h_ref, acc_ref):
    f = pl.program_id(1)

    @pl.when(f == 0)
    def _():
        x = x_ref[...]
        ms = jnp.mean(x * x, axis=-1, keepdims=True)
        h_ref[...] = (x * lax.rsqrt(ms + EPS) * g_ref[...]).astype(BF16)
        acc_ref[...] = x

    up = jnp.dot(h_ref[...], wu_ref[...], preferred_element_type=F32)
    act = jnp.square(jnp.maximum(up, 0.0))
    acc_ref[...] += jnp.dot(act.astype(BF16), wd_ref[...], preferred_element_type=F32)

    @pl.when(f == pl.num_programs(1) - 1)
    def _():
        o_ref[...] = acc_ref[...]


def _mlp(x1, g_row, w_up, w_down):
    t = x1.shape[0]
    tm = min(TM_MLP, t)
    return pl.pallas_call(
        _mlp_kernel,
        grid=(t // tm, D_FF // TF_MLP),
        in_specs=[pl.BlockSpec((tm, D_MODEL), lambda i, f: (i, 0)),
                  pl.BlockSpec((1, D_MODEL), lambda i, f: (0, 0)),
                  pl.BlockSpec((D_MODEL, TF_MLP), lambda i, f: (0, f)),
                  pl.BlockSpec((TF_MLP, D_MODEL), lambda i, f: (f, 0))],
        out_specs=pl.BlockSpec((tm, D_MODEL), lambda i, f: (i, 0)),
        out_shape=jax.ShapeDtypeStruct((t, D_MODEL), F32),
        scratch_shapes=[pltpu.VMEM((tm, D_MODEL), BF16), pltpu.VMEM((tm, D_MODEL), F32)],
        compiler_params=_cparams(("arbitrary", "arbitrary")),
        name="mlp",
    )(x1, g_row, w_up, w_down)


def _pack_w_in(w_in):
    pts = np.cumsum(np.array(IN_SPLITS))[:-1].tolist()
    (g_qkv, g_z, g_a, g_b, a_q, a_k, a_v, i_q, i_k, i_w, gate_a, gate_b) = jnp.split(w_in, pts, axis=-1)
    small = jnp.concatenate([g_a, g_b, i_k, i_w], axis=-1)
    small = jnp.pad(small, ((0, 0), (0, LANES - small.shape[-1])))
    main = jnp.concatenate([g_qkv, g_z, a_q, a_k, a_v, i_q, gate_a, gate_b], axis=-1)
    return main.astype(BF16), small.astype(BF16)


def _lane_row(vec, offset=0):
    return jnp.zeros((1, LANES), F32).at[0, offset:offset + vec.shape[0]].set(vec.astype(F32))


def _rope_table(seq):
    tab = np.zeros((seq, LANES), np.float64)
    pos = np.arange(seq, dtype=np.float64)[:, None]
    for rot_dim, base in ((ATT_HEAD // ROPE_FRACTION, 0), (IDX_DIM // ROPE_FRACTION, TAB_COS_I)):
        half = rot_dim // 2
        ang = pos * ROPE_THETA ** (-np.arange(0, rot_dim, 2, dtype=np.float64) / rot_dim)[None, :]
        tab[:, base:base + half] = tab[:, base + half:base + 2 * half] = np.cos(ang)
        tab[:, base + 2 * half:base + 3 * half] = -np.sin(ang)
        tab[:, base + 3 * half:base + 4 * half] = np.sin(ang)
    return jnp.asarray(tab, dtype=F32)


def _layer(x2, batch, seq, norm_mix_g, w_in, conv_w, a_log, dt_bias, gdn_norm_g, q_norm_g, k_norm_g,
           w_out, norm_mlp_g, w_mlp_up, w_mlp_down):
    k_sel = min(TOPK_MAX, seq // 4)
    w_main, w_small = _pack_w_in(w_in)
    g_mix = norm_mix_g.reshape(1, D_MODEL)
    proj = _inproj(x2, g_mix, w_main, TN_IN, BF16, "inproj")
    small = _inproj(x2, g_mix, w_small, LANES, F32, "inproj_small")

    o_gdn = _gdn(proj, small, conv_w, _lane_row(a_log, SM_GA), _lane_row(dt_bias, SM_GA),
                 gdn_norm_g.reshape(1, LANES), batch, seq)

    q_att, k_att, v_att, q_idx, k_idx_t, w_idx = _prep(
        proj, small, _rope_table(seq), q_norm_g.reshape(1, LANES), k_norm_g.reshape(1, LANES), batch, seq)

    ii = np.arange(KT)
    ustrict = jnp.asarray((ii[:, None] < ii[None, :]).astype(np.float32), dtype=BF16)
    bias = _select(q_idx, w_idx, k_idx_t, ustrict, batch, seq, k_sel)
    logit_bound = (1.01 * ATT_HEAD * ATT_HEAD ** -0.5) * jnp.max(jnp.abs(q_norm_g)) * jnp.max(jnp.abs(k_norm_g))
    o_att = _attention(q_att, k_att, v_att, bias, logit_bound.reshape(1).astype(F32), batch, seq)

    x1 = _merge(proj, o_gdn, o_att, x2, w_out.astype(BF16))
    return _mlp(x1, norm_mlp_g.reshape(1, D_MODEL), w_mlp_up.astype(BF16), w_mlp_down.astype(BF16))


def kernel(x, norm_mix_g, w_in, conv_w, a_log, dt_bias, gdn_norm_g, q_norm_g, k_norm_g, w_out,
           norm_mlp_g, w_mlp_up, w_mlp_down):
    batch, seq, _ = x.shape
    x2 = x.reshape(batch * seq, D_MODEL)
    for layer in range(norm_mix_g.shape[0]):
        x2 = _layer(x2, batch, seq, norm_mix_g[layer], w_in[layer], conv_w[layer], a_log[layer],
                    dt_bias[layer], gdn_norm_g[layer], q_norm_g[layer], k_norm_g[layer], w_out[layer],
                    norm_mlp_g[layer], w_mlp_up[layer], w_mlp_down[layer])
    return x2.reshape(batch, seq, D_MODEL)
```
